```python
import math
import jax, jax.numpy as jnp
from jax import lax
import numpy as np

D_MODEL = 1024
BATCH = 4
SEQ = 4096
DEPTH = 1
DEC_BATCH = 128
DEC_SEQ = 4
PAST_LEN = 8192
PAGE_SIZE = 128

MIX_WIDTH = D_MODEL
MLA_HEADS = 8
MLA_V_DIM = 64
MLA_NOPE_DIM = 64
MLA_ROPE_DIM = 32
MLA_Q_LORA = 256
MLA_KV_LORA = 128
MLA_WIDTH = MLA_HEADS * MLA_V_DIM
NSA_HEADS = 8
NSA_KV_HEADS = 2
NSA_HEAD_DIM = 64
NSA_GROUP = NSA_HEADS // NSA_KV_HEADS
NSA_WIDTH = NSA_HEADS * NSA_HEAD_DIM
NSA_KV_WIDTH = 2 * NSA_KV_HEADS * NSA_HEAD_DIM
CMP_BLOCK = 32
CMP_STRIDE = 16
SEL_BLOCK = 64
SEL_TOP_N = 16
WINDOW = 512
Q_BLOCK = 128
ROPE_THETA = 10000.0
RMS_EPS = 1e-6
LN_EPS = 1e-5
NEG_INF = -1e30
MLA_SCALE = (MLA_NOPE_DIM + MLA_ROPE_DIM) ** -0.5
NSA_SCALE = NSA_HEAD_DIM ** -0.5
ALPHA = (2 * DEPTH) ** 0.25
BETA = (8 * DEPTH) ** -0.25
IN_SPLITS = (MLA_Q_LORA, MLA_KV_LORA, MLA_ROPE_DIM, MLA_WIDTH, NSA_WIDTH, NSA_KV_WIDTH, NSA_KV_WIDTH, NSA_KV_WIDTH, 3 * NSA_HEADS, NSA_WIDTH)
IN_WIDTH = MLA_Q_LORA + MLA_KV_LORA + MLA_ROPE_DIM + MLA_WIDTH + NSA_WIDTH + 3 * NSA_KV_WIDTH + 3 * NSA_HEADS + NSA_WIDTH

kernel_name = 'hymba_mla_nsa_deepnorm_step'


def rms_norm(x, g):
    xf = x.astype(jnp.float32)
    y = xf * lax.rsqrt(jnp.mean(xf * xf, -1, keepdims=True) + RMS_EPS)
    return (y * g.astype(jnp.float32)).astype(x.dtype)


def layer_norm(x, g, b):
    xf = x.astype(jnp.float32)
    mu = jnp.mean(xf, -1, keepdims=True)
    var = jnp.mean(jnp.square(xf - mu), -1, keepdims=True)
    y = (xf - mu) * lax.rsqrt(var + LN_EPS) * g.astype(jnp.float32) + b.astype(jnp.float32)
    return y.astype(x.dtype)


def rope(x, pos):
    d = x.shape[-1]
    inv = 1.0 / (ROPE_THETA ** (jnp.arange(0, d, 2, dtype=jnp.float32) / d))
    ang = pos.astype(jnp.float32)[:, None] * inv[None, :]
    shp = (1, pos.shape[0]) + (1,) * (x.ndim - 3) + (d // 2,)
    cos = jnp.cos(ang).reshape(shp).astype(x.dtype)
    sin = jnp.sin(ang).reshape(shp).astype(x.dtype)
    x1, x2 = x[..., : d // 2], x[..., d // 2:]
    return jnp.concatenate([x1 * cos - x2 * sin, x2 * cos + x1 * sin], axis=-1)


def rope_k(kv, pos):
    return jnp.stack([rope(kv[:, :, 0], pos), kv[:, :, 1]], axis=2)


def masked_softmax(s, mask):
    p = jax.nn.softmax(jnp.where(mask, s, NEG_INF), axis=-1)
    return jnp.where(mask, p, 0.0)


def pad_to(a, m):
    r = (-a.shape[1]) % m
    return jnp.pad(a, [(0, 0), (0, r)] + [(0, 0)] * (a.ndim - 2))


def importance_map(nc, ns):
    i = np.arange(nc)[:, None]
    j = np.arange(ns)[None, :]
    lo = np.maximum(i * CMP_STRIDE, j * SEL_BLOCK)
    hi = np.minimum(i * CMP_STRIDE + CMP_BLOCK, (j + 1) * SEL_BLOCK)
    return jnp.asarray(np.maximum(hi - lo, 0).astype(np.float32) / CMP_BLOCK)


def project(x, pos, w):
    B, T, _ = x.shape
    cuts = np.cumsum(np.array(IN_SPLITS))[:-1].tolist()
    c_q, c_kv, k_r, z_mla, q_n, cmp_kv, sel_kv, win_kv, g_br, z_nsa = jnp.split(x @ w['w_in'], cuts, axis=-1)
    q = (rms_norm(c_q, w['q_norm_g']) @ w['w_uq']).reshape(B, T, MLA_HEADS, MLA_NOPE_DIM + MLA_ROPE_DIM)
    w_uk = w['w_uk'].reshape(MLA_KV_LORA, MLA_HEADS, MLA_NOPE_DIM)
    q_lat = jnp.einsum('bthn,chn->bthc', q[..., :MLA_NOPE_DIM], w_uk)
    q_rope = rope(q[..., MLA_NOPE_DIM:], pos)
    mla_row = jnp.concatenate([rms_norm(c_kv, w['kv_norm_g']), rope(k_r, pos)], axis=-1)
    q_n = q_n.reshape(B, T, NSA_KV_HEADS, NSA_GROUP, NSA_HEAD_DIM)
    kv_shape = (B, T, 2, NSA_KV_HEADS, NSA_HEAD_DIM)
    gates = jax.nn.sigmoid(g_br.astype(jnp.float32)).astype(x.dtype).reshape(B, T, 3, NSA_KV_HEADS, NSA_GROUP)
    return dict(q_lat=q_lat, q_rope=q_rope, mla_row=mla_row, z_mla=z_mla,
                q_n=q_n, q_rot=rope(q_n, pos), cmp_kv=cmp_kv.reshape(kv_shape),
                sel_kv=rope_k(sel_kv.reshape(kv_shape), pos), win_kv=rope_k(win_kv.reshape(kv_shape), pos),
                gates=gates, z_nsa=z_nsa)


def compress(kv, w):
    B, T = kv.shape[:2]
    n_chunk = T // CMP_STRIDE
    ratio = CMP_BLOCK // CMP_STRIDE
    nc = n_chunk - ratio + 1
    ch = kv.reshape(B, n_chunk, CMP_STRIDE, 2, NSA_KV_HEADS, NSA_HEAD_DIM)
    w1 = w['cmp_w1']
    w1c = w1.reshape(2, ratio, CMP_STRIDE, NSA_HEAD_DIM, NSA_HEAD_DIM)
    bias = w['cmp_b1'] + jnp.einsum('pkd,kpde->ke', w['cmp_pos_emb'], w1)
    hid = bias[None, None, :, None, :]
    for r in range(ratio):
        proj = jnp.einsum('bcpkgd,kpde->bckge', ch, w1c[:, r])
        hid = hid + proj[:, r:r + nc]
    return jnp.einsum('bckge,ked->bckgd', jax.nn.silu(hid), w['cmp_w2'])


def mla_attend(q_lat, q_rope, q_pos, kv_row, k_pos):
    ckv, krope = kv_row[..., :MLA_KV_LORA], kv_row[..., MLA_KV_LORA:]
    s = jnp.einsum('bthc,bkc->bhtk', q_lat, ckv) + jnp.einsum('bthr,bkr->bhtk', q_rope, krope)
    p = masked_softmax(s.astype(jnp.float32) * MLA_SCALE, k_pos[None, :] <= q_pos[:, None])
    return jnp.einsum('bhtk,bkc->bthc', p.astype(ckv.dtype), ckv)


def nsa_attend(q, q_rot, q_pos, gates, kv_c, cmp_end, imp_map, gather_sel, win_kv, win_pos):
    dt = q.dtype
    s = jnp.einsum('btghd,bngd->bghtn', q, kv_c[:, :, 0]).astype(jnp.float32) * NSA_SCALE
    p_cmp = masked_softmax(s, cmp_end[None, :] <= q_pos[:, None])
    o_cmp = jnp.einsum('bghtn,bngd->btghd', p_cmp.astype(dt), kv_c[:, :, 1])
    ns = imp_map.shape[1]
    imp = jnp.einsum('bghtn,nj->bgtj', p_cmp, imp_map)
    j = jnp.arange(ns, dtype=jnp.int32)[None, :]
    cur = (q_pos // SEL_BLOCK)[:, None]
    avail = j * SEL_BLOCK <= q_pos[:, None]
    forced = (j == 0) | (j == cur) | (j == cur - 1)
    imp = jnp.where(avail, jnp.where(forced, jnp.inf, imp), -jnp.inf)
    top_val, top_idx = lax.top_k(imp, min(SEL_TOP_N, ns))
    kv_s = gather_sel(top_idx)
    tok_pos = top_idx[..., None] * SEL_BLOCK + jnp.arange(SEL_BLOCK, dtype=jnp.int32)
    sel_mask = (top_val > -jnp.inf)[..., None] & (tok_pos <= q_pos[None, None, :, None, None])
    s = jnp.einsum('btghd,bgtnpd->bghtnp', q_rot, kv_s[..., 0, :]).astype(jnp.float32) * NSA_SCALE
    B, G, hg, Tq, n, sb = s.shape
    p = masked_softmax(s.reshape(B, G, hg, Tq, n * sb), sel_mask.reshape(B, G, 1, Tq, n * sb)).reshape(s.shape)
    o_sel = jnp.einsum('bghtnp,bgtnpd->btghd', p.astype(dt), kv_s[..., 1, :])
    s = jnp.einsum('btghd,bkgd->bghtk', q_rot, win_kv[:, :, 0]).astype(jnp.float32) * NSA_SCALE
    dpos = q_pos[:, None] - win_pos[None, :]
    p = masked_softmax(s, (dpos >= 0) & (dpos < WINDOW) & (win_pos >= 0)[None, :])
    o_win = jnp.einsum('bghtk,bkgd->btghd', p.astype(dt), win_kv[:, :, 1])
    g = gates[..., None]
    o = g[:, :, 0] * o_cmp + g[:, :, 1] * o_sel + g[:, :, 2] * o_win
    return o.reshape(o.shape[0], o.shape[1], NSA_WIDTH)


def finish(x, o_lat, o_nsa, z_mla, z_nsa, w):
    B, T, _ = x.shape
    w_uv = w['w_uv'].reshape(MLA_KV_LORA, MLA_HEADS, MLA_V_DIM)
    o_mla = jnp.einsum('bthc,chv->bthv', o_lat, w_uv).reshape(B, T, MLA_WIDTH)
    mixed = jnp.concatenate([o_mla * jax.nn.silu(z_mla), o_nsa * jax.nn.silu(z_nsa)], axis=-1)
    return layer_norm(ALPHA * x + mixed @ w['w_o'], w['ln_g'], w['ln_b'])


def prompt_layer(x, w):
    B, T, _ = x.shape
    pos = jnp.arange(T, dtype=jnp.int32)
    P = project(x, pos, w)
    kv_c = compress(pad_to(P['cmp_kv'], CMP_STRIDE), w)
    nc = kv_c.shape[1]
    cmp_end = jnp.arange(nc, dtype=jnp.int32) * CMP_STRIDE + (CMP_BLOCK - 1)
    sel = pad_to(P['sel_kv'], SEL_BLOCK)
    ns = sel.shape[1] // SEL_BLOCK
    sel_blocks = sel.reshape(B, ns, SEL_BLOCK, 2, NSA_KV_HEADS, NSA_HEAD_DIM)
    imp_map = importance_map(nc, ns)
    bi = jnp.arange(B)[:, None, None, None]
    gi = jnp.arange(NSA_KV_HEADS)[None, :, None, None]
    gather_sel = lambda idx: sel_blocks[bi, idx, :, :, gi]
    win_pad = jnp.pad(P['win_kv'], ((0, 0), (WINDOW, 0), (0, 0), (0, 0), (0, 0)))

    def block(s):
        q_pos = s + jnp.arange(Q_BLOCK, dtype=jnp.int32)
        qs = lambda a: lax.dynamic_slice_in_dim(a, s, Q_BLOCK, axis=1)
        o_lat = mla_attend(qs(P['q_lat']), qs(P['q_rope']), q_pos, P['mla_row'], pos)
        win = lax.dynamic_slice_in_dim(win_pad, s, WINDOW + Q_BLOCK, axis=1)
        win_pos = s - WINDOW + jnp.arange(WINDOW + Q_BLOCK, dtype=jnp.int32)
        o_nsa = nsa_attend(qs(P['q_n']), qs(P['q_rot']), q_pos, qs(P['gates']), kv_c, cmp_end,
                           imp_map, gather_sel, win, win_pos)
        return o_lat, o_nsa

    o_lat, o_nsa = lax.map(block, jnp.arange(T // Q_BLOCK, dtype=jnp.int32) * Q_BLOCK)
    unblock = lambda o: jnp.moveaxis(o, 0, 1).reshape((B, T) + o.shape[3:])
    y = finish(x, unblock(o_lat), unblock(o_nsa), P['z_mla'], P['z_nsa'], w)
    new = (P['mla_row'], P['cmp_kv'], P['sel_kv'], P['win_kv'][:, T - min(WINDOW, T):])
    return y, new


def sample_layer(x, cache_mla, cache_cmp, cache_sel, win_state, page_table, w):
    B, T, _ = x.shape
    n_pages = page_table.shape[1]
    past = n_pages * PAGE_SIZE
    pos = past + jnp.arange(T, dtype=jnp.int32)
    P = project(x, pos, w)

    def gather_past(cache):
        return cache[page_table].reshape((B, past) + cache.shape[2:])

    k_pos = jnp.arange(past + T, dtype=jnp.int32)
    mla_all = jnp.concatenate([gather_past(cache_mla), P['mla_row']], axis=1)
    o_lat = mla_attend(P['q_lat'], P['q_rope'], pos, mla_all, k_pos)
    cmp_all = jnp.concatenate([gather_past(cache_cmp), P['cmp_kv']], axis=1)
    kv_c = compress(pad_to(cmp_all, CMP_STRIDE), w)
    nc = kv_c.shape[1]
    cmp_end = jnp.arange(nc, dtype=jnp.int32) * CMP_STRIDE + (CMP_BLOCK - 1)
    nb_past = past // SEL_BLOCK
    new_sel = pad_to(P['sel_kv'], SEL_BLOCK)
    nnb = new_sel.shape[1] // SEL_BLOCK
    new_blocks = new_sel.reshape(B, nnb, SEL_BLOCK, 2, NSA_KV_HEADS, NSA_HEAD_DIM)
    sub = PAGE_SIZE // SEL_BLOCK
    pool = cache_sel.reshape((cache_sel.shape[0], sub, SEL_BLOCK) + cache_sel.shape[2:])
    bi = jnp.arange(B)[:, None, None, None]
    gi = jnp.arange(NSA_KV_HEADS)[None, :, None, None]

    def gather_sel(idx):
        jp = jnp.minimum(idx, nb_past - 1)
        phys = page_table[bi, jp // sub]
        old = pool[phys, jp % sub, :, :, gi]
        new = new_blocks[bi, jnp.clip(idx - nb_past, 0, nnb - 1), :, :, gi]
        return jnp.where((idx < nb_past)[..., None, None, None], old, new)

    imp_map = importance_map(nc, nb_past + nnb)
    w_buf = win_state.shape[1]
    win_all = jnp.concatenate([win_state, P['win_kv']], axis=1)
    win_pos = past - w_buf + jnp.arange(w_buf + T, dtype=jnp.int32)
    o_nsa = nsa_attend(P['q_n'], P['q_rot'], pos, P['gates'], kv_c, cmp_end, imp_map, gather_sel, win_all, win_pos)
    y = finish(x, o_lat, o_nsa, P['z_mla'], P['z_nsa'], w)
    new = (P['mla_row'], P['cmp_kv'], P['sel_kv'], win_all[:, T:])
    return y, new


def setup_inputs(seed: int = 0) -> dict:
    key = jax.random.key(seed)
    ks = jax.random.split(key, 24)
    n_pages = PAST_LEN // PAGE_SIZE
    n_used = DEC_BATCH * n_pages
    n_pool = n_used + max(1, n_used // 4)
    w_buf = min(WINDOW, PAST_LEN)
    nrm = lambda k, shape, scale: jax.random.normal(k, shape, jnp.float32) * scale
    page_table = jax.random.permutation(ks[6], n_pool)[:n_used].reshape(DEC_BATCH, n_pages).astype(jnp.int32)
    kvd = (2, NSA_KV_HEADS, NSA_HEAD_DIM)
    return {
        'x_prompt': nrm(ks[0], (BATCH, SEQ, D_MODEL), 1.0),
        'x_sample': nrm(ks[1], (DEC_BATCH, DEC_SEQ, D_MODEL), 1.0),
        'cache_mla': nrm(ks[2], (DEPTH, n_pool, PAGE_SIZE, MLA_KV_LORA + MLA_ROPE_DIM), 1.0),
        'cache_cmp_kv': nrm(ks[3], (DEPTH, n_pool, PAGE_SIZE) + kvd, 1.0),
        'cache_sel_kv': nrm(ks[4], (DEPTH, n_pool, PAGE_SIZE) + kvd, 1.0),
        'state_win_kv': nrm(ks[5], (DEPTH, DEC_BATCH, w_buf) + kvd, 1.0),
        'page_table': page_table,
        'w_in': nrm(ks[7], (DEPTH, D_MODEL, IN_WIDTH), D_MODEL ** -0.5),
        'q_norm_g': 1.0 + nrm(ks[8], (DEPTH, MLA_Q_LORA), 0.02),
        'w_uq': nrm(ks[9], (DEPTH, MLA_Q_LORA, MLA_HEADS * (MLA_NOPE_DIM + MLA_ROPE_DIM)), MLA_Q_LORA ** -0.5),
        'kv_norm_g': 1.0 + nrm(ks[10], (DEPTH, MLA_KV_LORA), 0.02),
        'w_uk': nrm(ks[11], (DEPTH, MLA_KV_LORA, MLA_HEADS * MLA_NOPE_DIM), MLA_KV_LORA ** -0.5),
        'w_uv': nrm(ks[12], (DEPTH, MLA_KV_LORA, MLA_HEADS * MLA_V_DIM), MLA_KV_LORA ** -0.5),
        'cmp_pos_emb': nrm(ks[13], (DEPTH, CMP_BLOCK, 2, NSA_HEAD_DIM), 0.1),
        'cmp_w1': nrm(ks[14], (DEPTH, 2, CMP_BLOCK, NSA_HEAD_DIM, NSA_HEAD_DIM), (CMP_BLOCK * NSA_HEAD_DIM) ** -0.5),
        'cmp_b1': nrm(ks[15], (DEPTH, 2, NSA_HEAD_DIM), 0.02),
        'cmp_w2': nrm(ks[16], (DEPTH, 2, NSA_HEAD_DIM, NSA_HEAD_DIM), NSA_HEAD_DIM ** -0.5),
        'w_o': nrm(ks[17], (DEPTH, MIX_WIDTH, D_MODEL), MIX_WIDTH ** -0.5 * BETA),
        'ln_g': 1.0 + nrm(ks[18], (DEPTH, D_MODEL), 0.02),
        'ln_b': nrm(ks[19], (DEPTH, D_MODEL), 0.02),
    }


def reference(x_prompt, x_sample, cache_mla, cache_cmp_kv, cache_sel_kv, state_win_kv, page_table,
              w_in, q_norm_g, w_uq, kv_norm_g, w_uk, w_uv, cmp_pos_emb, cmp_w1, cmp_b1, cmp_w2,
              w_o, ln_g, ln_b):
    hp, hs = x_prompt, x_sample
    rows = [[] for _ in range(8)]
    for l in range(DEPTH):
        w = dict(w_in=w_in[l], q_norm_g=q_norm_g[l], w_uq=w_uq[l], kv_norm_g=kv_norm_g[l], w_uk=w_uk[l],
                 w_uv=w_uv[l], cmp_pos_emb=cmp_pos_emb[l], cmp_w1=cmp_w1[l], cmp_b1=cmp_b1[l],
                 cmp_w2=cmp_w2[l], w_o=w_o[l], ln_g=ln_g[l], ln_b=ln_b[l])
        hp, new_p = prompt_layer(hp, w)
        hs, new_s = sample_layer(hs, cache_mla[l], cache_cmp_kv[l], cache_sel_kv[l], state_win_kv[l], page_table, w)
        for k, a in enumerate(new_p + new_s):
            rows[k].append(a)
    p_mla, p_cmp, p_sel, p_win, s_mla, s_cmp, s_sel, s_win = [jnp.stack(r) for r in rows]
    return (hp, hs, p_mla, p_cmp, p_sel, p_win, s_mla, s_cmp, s_sel, s_win)
```

```python
import functools

import numpy as np
import jax
import jax.numpy as jnp
from jax import lax
from jax.experimental import pallas as pl
from jax.experimental.pallas import tpu as pltpu

F32 = jnp.float32
BF16 = jnp.bfloat16

D_MODEL = 1024
PAGE_SIZE = 128
MLA_HEADS = 8
MLA_V_DIM = 64
MLA_NOPE_DIM = 64
MLA_ROPE_DIM = 32
MLA_Q_LORA = 256
MLA_KV_LORA = 128
MLA_WIDTH = MLA_HEADS * MLA_V_DIM
MLA_ROW = MLA_KV_LORA + MLA_ROPE_DIM
NSA_HEADS = 8
NSA_KV_HEADS = 2
NSA_HEAD_DIM = 64
NSA_GROUP = NSA_HEADS // NSA_KV_HEADS
NSA_WIDTH = NSA_HEADS * NSA_HEAD_DIM
NSA_KV_WIDTH = 2 * NSA_KV_HEADS * NSA_HEAD_DIM
CMP_BLOCK = 32
CMP_STRIDE = 16
SEL_BLOCK = 64
SEL_TOP_N = 16
WINDOW = 512
Q_BLOCK = 128
ROPE_THETA = 10000.0
RMS_EPS = 1e-6
LN_EPS = 1e-5
NEG_INF = -1e30
MLA_SCALE = (MLA_NOPE_DIM + MLA_ROPE_DIM) ** -0.5
NSA_SCALE = NSA_HEAD_DIM ** -0.5
DEPTH = 1
ALPHA = (2 * DEPTH) ** 0.25

LANES = 128

SEG_CQ = (0, 256)
SEG_CKV = (256, 128)
SEG_ZMLA = (384, 512)
SEG_QN = (896, 512)
SEG_CMP = (1408, 256)
SEG_SEL = (1664, 256)
SEG_WIN = (1920, 256)
SEG_ZNSA = (2176, 512)
SEG_MISC = (2688, 128)
IN_WIDTH_P = 2816
MISC_GATE0 = MLA_ROPE_DIM

VMEM_LIMIT = 48 * 1024 * 1024


def _cparams(sem):
    return pltpu.CompilerParams(dimension_semantics=sem, vmem_limit_bytes=VMEM_LIMIT)


def _sigmoid(x):
    return 1.0 / (1.0 + jnp.exp(-x))


def _silu(x):
    return x * _sigmoid(x)


def _rope_slab(x, c, sa, sb, half):
    return x * c + pltpu.roll(x, LANES - half, 1) * sa + pltpu.roll(x, half, 1) * sb


def _dot_nt(a, b):
    return lax.dot_general(a, b, (((1,), (1,)), ((), ())), preferred_element_type=F32)


def _dot(a, b):
    return jnp.dot(a, b, preferred_element_type=F32)


def _masked_softmax(s, mask):
    sm = jnp.where(mask, s, NEG_INF)
    m = jnp.max(sm, axis=-1, keepdims=True)
    e = jnp.exp(sm - m)
    p = e / jnp.sum(e, axis=-1, keepdims=True)
    return jnp.where(mask, p, 0.0)


def _online_update(s, mask, v, m, l, acc):
    sm = jnp.where(mask, s, NEG_INF)
    m_new = jnp.maximum(m, jnp.max(sm, axis=-1, keepdims=True))
    alpha = jnp.exp(m - m_new)
    p = jnp.where(mask, jnp.exp(sm - m_new), 0.0)
    l_new = alpha * l + jnp.sum(p, axis=-1, keepdims=True)
    acc_new = alpha * acc + _dot(p.astype(BF16), v)
    return m_new, l_new, acc_new


def _finalize(l, acc):
    ok = l > 0.0
    return jnp.where(ok, acc / jnp.where(ok, l, 1.0), 0.0)


def _select_blocks(imp, q_pos, n_sel_blocks):
    rows, width = imp.shape
    j = lax.broadcasted_iota(jnp.int32, (rows, width), 1)
    cur = lax.shift_right_logical(q_pos, 6)
    avail = (j * SEL_BLOCK <= q_pos) & (j < n_sel_blocks)
    forced = (j == 0) | (j == cur) | (j == cur - 1)
    val = jnp.where(avail, jnp.where(forced, jnp.inf, imp), -jnp.inf)
    rank = jnp.zeros((rows, width), F32)
    for jp in range(n_sel_blocks):
        col = val[:, jp:jp + 1]
        ahead = (col > val) | ((col == val) & (j > jp))
        rank = rank + jnp.where(ahead, 1.0, 0.0)
    keep = (rank < float(min(SEL_TOP_N, n_sel_blocks))) & avail
    return jnp.where(keep, 1.0, 0.0)


def _proj_kernel(x_ref, w_ref, qg_ref, wuq_ref, wuk_ref, kvg_ref, tab_ref,
                 qmla_ref, mrow_ref, mrowb_ref, szm_ref, qn_ref, qrot_ref, cmp_ref, cmps_ref,
                 sel_ref, selb_ref, win_ref, winb_ref, gates_ref, szn_ref):
    xb = x_ref[...].astype(BF16)

    def seg(s):
        return _dot(xb, w_ref[:, s[0]:s[0] + s[1]])

    cn, san, sbn = tab_ref[0], tab_ref[1], tab_ref[2]
    cm, sam, sbm = tab_ref[3], tab_ref[4], tab_ref[5]

    cq = seg(SEG_CQ)
    r = cq * lax.rsqrt(jnp.mean(cq * cq, axis=-1, keepdims=True) + RMS_EPS) * qg_ref[...]
    q = _dot(r.astype(BF16), wuq_ref[...])
    nope_w = MLA_HEADS * MLA_NOPE_DIM
    for h in range(MLA_HEADS):
        ql = _dot(q[:, h * MLA_NOPE_DIM:(h + 1) * MLA_NOPE_DIM].astype(BF16), wuk_ref[h])
        qmla_ref[h, :, 0:MLA_KV_LORA] = ql.astype(BF16)
    for jj in range(2):
        qr = _rope_slab(q[:, nope_w + jj * LANES: nope_w + (jj + 1) * LANES], cm, sam, sbm, MLA_ROPE_DIM // 2)
        for hh in range(LANES // MLA_ROPE_DIM):
            h = jj * (LANES // MLA_ROPE_DIM) + hh
            qmla_ref[h, :, MLA_KV_LORA:MLA_ROW] = qr[:, hh * MLA_ROPE_DIM:(hh + 1) * MLA_ROPE_DIM].astype(BF16)

    ckv = seg(SEG_CKV)
    lat = ckv * lax.rsqrt(jnp.mean(ckv * ckv, axis=-1, keepdims=True) + RMS_EPS) * kvg_ref[...]
    misc = seg(SEG_MISC)
    kr = _rope_slab(misc, cm, sam, sbm, MLA_ROPE_DIM // 2)
    mrow_ref[:, 0:MLA_KV_LORA] = lat
    mrow_ref[:, MLA_KV_LORA:MLA_ROW] = kr[:, 0:MLA_ROPE_DIM]
    mrowb_ref[:, 0:MLA_KV_LORA] = lat.astype(BF16)
    mrowb_ref[:, MLA_KV_LORA:MLA_ROW] = kr[:, 0:MLA_ROPE_DIM].astype(BF16)
    gates_ref[...] = _sigmoid(misc)

    szm_ref[...] = _silu(seg(SEG_ZMLA))
    szn_ref[...] = _silu(seg(SEG_ZNSA))

    qn = seg(SEG_QN)
    lane = lax.broadcasted_iota(jnp.int32, (qn.shape[0], LANES), 1)
    for jj in range(NSA_WIDTH // LANES):
        raw = qn[:, jj * LANES:(jj + 1) * LANES] * NSA_SCALE
        rot = _rope_slab(qn[:, jj * LANES:(jj + 1) * LANES], cn, san, sbn, NSA_HEAD_DIM // 2) * NSA_SCALE
        for src, dst in ((raw, qn_ref), (rot, qrot_ref)):
            swapped = pltpu.roll(src, NSA_HEAD_DIM, 1)
            for half in range(2):
                hd = 2 * jj + half
                g = hd // NSA_GROUP
                keep = (lane >= g * NSA_HEAD_DIM) & (lane < (g + 1) * NSA_HEAD_DIM)
                dst[hd] = jnp.where(keep, src if half == g else swapped, 0.0).astype(BF16)

    cmpv = seg(SEG_CMP)
    cmp_ref[...] = cmpv
    cmps_ref[0] = cmpv[:, 0:LANES]
    cmps_ref[1] = cmpv[:, LANES:2 * LANES]

    for src, dst, dstb in ((SEG_SEL, sel_ref, selb_ref), (SEG_WIN, win_ref, winb_ref)):
        kv = seg(src)
        k = _rope_slab(kv[:, 0:LANES], cn, san, sbn, NSA_HEAD_DIM // 2)
        v = kv[:, LANES:2 * LANES]
        dst[:, 0:LANES] = k
        dst[:, LANES:2 * LANES] = v
        dstb[:, 0:LANES] = k.astype(BF16)
        dstb[:, LANES:2 * LANES] = v.astype(BF16)


def _rope_tables(pos):
    def tab(d):
        inv = 1.0 / (ROPE_THETA ** (jnp.arange(0, d, 2, dtype=F32) / d))
        ang = pos.astype(F32)[:, None] * inv[None, :]
        cos, sin = jnp.cos(ang), jnp.sin(ang)
        zero = jnp.zeros_like(sin)
        reps = LANES // d
        return (jnp.tile(jnp.concatenate([cos, cos], -1), (1, reps)),
                jnp.tile(jnp.concatenate([-sin, zero], -1), (1, reps)),
                jnp.tile(jnp.concatenate([zero, sin], -1), (1, reps)))
    return jnp.stack(tab(NSA_HEAD_DIM) + tab(MLA_ROPE_DIM))


def _project(x2d, tabs, wts, tm):
    n = x2d.shape[0]
    p_rows = tabs.shape[1]
    assert n % tm == 0 and p_rows % tm == 0
    nt = p_rows // tm
    row = lambda w: pl.BlockSpec((tm, w), lambda i: (i, 0))
    full = lambda a: pl.BlockSpec(a.shape, lambda i: (0,) * a.ndim)
    out_shapes = dict(
        qmla=((MLA_HEADS, n, MLA_ROW), BF16, pl.BlockSpec((MLA_HEADS, tm, MLA_ROW), lambda i: (0, i, 0))),
        mrow=((n, MLA_ROW), F32, row(MLA_ROW)),
        mrowb=((n, MLA_ROW), BF16, row(MLA_ROW)),
        szm=((n, MLA_WIDTH), F32, row(MLA_WIDTH)),
        qn=((NSA_HEADS, n, LANES), BF16, pl.BlockSpec((NSA_HEADS, tm, LANES), lambda i: (0, i, 0))),
        qrot=((NSA_HEADS, n, LANES), BF16, pl.BlockSpec((NSA_HEADS, tm, LANES), lambda i: (0, i, 0))),
        cmp=((n, NSA_KV_WIDTH), F32, row(NSA_KV_WIDTH)),
        cmps=((2, n, LANES), F32, pl.BlockSpec((2, tm, LANES), lambda i: (0, i, 0))),
        sel=((n, NSA_KV_WIDTH), F32, row(NSA_KV_WIDTH)),
        selb=((n, NSA_KV_WIDTH), BF16, row(NSA_KV_WIDTH)),
        win=((n, NSA_KV_WIDTH), F32, row(NSA_KV_WIDTH)),
        winb=((n, NSA_KV_WIDTH), BF16, row(NSA_KV_WIDTH)),
        gates=((n, LANES), F32, row(LANES)),
        szn=((n, NSA_WIDTH), F32, row(NSA_WIDTH)),
    )
    names = list(out_shapes)
    outs = pl.pallas_call(
        _proj_kernel,
        grid=(n // tm,),
        in_specs=[row(D_MODEL), full(wts['w_in']), full(wts['q_norm_g']), full(wts['w_uq']), full(wts['w_uk']),
                  full(wts['kv_norm_g']), pl.BlockSpec((6, tm, LANES), lambda i: (0, i % nt, 0))],
        out_specs=[out_shapes[k][2] for k in names],
        out_shape=[jax.ShapeDtypeStruct(out_shapes[k][0], out_shapes[k][1]) for k in names],
        compiler_params=_cparams(("parallel",)),
        name="projection",
    )(x2d, wts['w_in'], wts['q_norm_g'], wts['w_uq'], wts['w_uk'], wts['kv_norm_g'], tabs)
    return dict(zip(names, outs))


def _uv_project(o, wuv_ref, rows):
    return jnp.concatenate(
        [_dot(o[h * rows:(h + 1) * rows].astype(BF16), wuv_ref[h]) for h in range(MLA_HEADS)], axis=-1)


def _mla_prompt_kernel(q_ref, k_ref, wuv_ref, o_ref, *, tq, tk):
    qi = pl.program_id(1)
    rows = MLA_HEADS * tq
    q = q_ref[...].reshape(rows, MLA_ROW)
    q_pos = qi * tq + (lax.broadcasted_iota(jnp.int32, (rows, tk), 0) & (tq - 1))
    col = lax.broadcasted_iota(jnp.int32, (rows, tk), 1)
    n_chunks = (qi * tq + tq + tk - 1) // tk

    def body(c, carry):
        m, l, acc = carry
        kc = k_ref[pl.ds(pl.multiple_of(c * tk, tk), tk), :]
        s = _dot_nt(q, kc) * MLA_SCALE
        mask = (c * tk + col) <= q_pos
        return _online_update(s, mask, kc[:, 0:MLA_KV_LORA], m, l, acc)

    init = (jnp.full((rows, 1), NEG_INF, F32), jnp.zeros((rows, 1), F32), jnp.zeros((rows, MLA_KV_LORA), F32))
    _, l, acc = lax.fori_loop(0, n_chunks, body, init)
    o_ref[...] = _uv_project(_finalize(l, acc), wuv_ref, tq)


def _mla_prompt(qmla, mrowb, wuv, batch, seq):
    tq = min(128, seq)
    tk = min(512, seq)
    nq = seq // tq
    return pl.pallas_call(
        functools.partial(_mla_prompt_kernel, tq=tq, tk=tk),
        grid=(batch, nq),
        in_specs=[pl.BlockSpec((MLA_HEADS, tq, MLA_ROW), lambda b, i: (0, b * nq + i, 0)),
                  pl.BlockSpec((None, seq, MLA_ROW), lambda b, i: (b, 0, 0)),
                  pl.BlockSpec(wuv.shape, lambda b, i: (0, 0, 0))],
        out_specs=pl.BlockSpec((tq, MLA_WIDTH), lambda b, i: (b * nq + i, 0)),
        out_shape=jax.ShapeDtypeStruct((batch * seq, MLA_WIDTH), F32),
        compiler_params=_cparams(("parallel", "parallel")),
        name="mla_prompt",
    )(qmla, mrowb.reshape(batch, seq, MLA_ROW), wuv)


def _compress_slab(load_rows, wc1_ref, pe_ref, b1_ref, w2_ref, s, n_chunk):
    proj = jnp.zeros((n_chunk, 2 * LANES), F32)
    pe = jnp.zeros((8, 2 * LANES), F32)
    for p in range(CMP_STRIDE):
        w = wc1_ref[s, p]
        proj = proj + _dot(load_rows(p).astype(BF16), w)
        pe = pe + _dot(pe_ref[s, p].astype(BF16), w)
    bias = b1_ref[s] + pe[0:1, 0:LANES] + pe[1:2, LANES:2 * LANES]
    hid = proj[:, 0:LANES] + pltpu.roll(proj[:, LANES:2 * LANES], n_chunk - 1, 0) + bias
    return _dot(_silu(hid).astype(BF16), w2_ref[s])


def _compress_prompt_kernel(x_ref, wc1_ref, pe_ref, b1_ref, w2_ref, o_ref, *, n_chunk):
    for s in range(2):
        load = lambda p, s=s: x_ref[s, pl.ds(p, n_chunk, stride=CMP_STRIDE), :]
        o_ref[s] = _compress_slab(load, wc1_ref, pe_ref, b1_ref, w2_ref, s, n_chunk).astype(BF16)


def _compress_prompt(cmps, wts, batch, seq):
    n_chunk = seq // CMP_STRIDE
    full = lambda a: pl.BlockSpec(a.shape, lambda b: (0,) * a.ndim)
    return pl.pallas_call(
        functools.partial(_compress_prompt_kernel, n_chunk=n_chunk),
        grid=(batch,),
        in_specs=[pl.BlockSpec((2, seq, LANES), lambda b: (0, b, 0)),
                  full(wts['wc1']), full(wts['pe']), full(wts['b1']), full(wts['w2'])],
        out_specs=pl.BlockSpec((None, 2, n_chunk, LANES), lambda b: (b, 0, 0, 0)),
        out_shape=jax.ShapeDtypeStruct((batch, 2, n_chunk, LANES), BF16),
        compiler_params=_cparams(("parallel",)),
        name="compress_prompt",
    )(cmps, wts['wc1'], wts['pe'], wts['b1'], wts['w2'])


def _cmp_branch(q, kc, vc, q_pos_rows, n_cmp_blocks, impmap_ref, rows_t):
    s = _dot_nt(q, kc)
    n = lax.broadcasted_iota(jnp.int32, s.shape, 1)
    mask = (n * CMP_STRIDE + (CMP_BLOCK - 1) <= q_pos_rows) & (n < n_cmp_blocks)
    p = _masked_softmax(s, mask)
    o_cmp = _dot(p.astype(BF16), vc)
    imps = []
    for g in range(NSA_KV_HEADS):
        base = g * NSA_GROUP * rows_t
        psum = p[base:base + rows_t]
        for hh in range(1, NSA_GROUP):
            psum = psum + p[base + hh * rows_t: base + (hh + 1) * rows_t]
        hi = psum.astype(BF16)
        lo = (psum - hi.astype(F32)).astype(BF16)
        imps.append(_dot(hi, impmap_ref[...]) + _dot(lo, impmap_ref[...]))
    return o_cmp, imps


def _mix_heads(gates, o_cmp, o_sel, o_win, rows_t):
    outs = []
    for hd in range(NSA_HEADS):
        g = hd // NSA_GROUP
        rs = slice(hd * rows_t, (hd + 1) * rows_t)
        ls = slice(g * NSA_HEAD_DIM, (g + 1) * NSA_HEAD_DIM)
        gcol = lambda br: gates[:, MISC_GATE0 + br * NSA_HEADS + hd: MISC_GATE0 + br * NSA_HEADS + hd + 1]
        outs.append(gcol(0) * o_cmp[rs, ls] + gcol(1) * o_sel[rs, ls] + gcol(2) * o_win[rs, ls])
    return jnp.concatenate(outs, axis=-1)


def _group_rows(per_group, reps):
    return jnp.concatenate([per_group[g] for g in range(NSA_KV_HEADS) for _ in range(reps)], axis=0)


def _nsa_prompt_kernel(qn_ref, qrot_ref, kvc_ref, sel_ref, win_ref, gates_ref, impmap_ref, o_ref,
                       *, tq, tk, n_cmp_blocks, n_sel_blocks, wlen):
    qi = pl.program_id(1)
    rows = NSA_HEADS * tq
    q_pos_t = qi * tq + lax.broadcasted_iota(jnp.int32, (tq, 1), 0)
    q_pos_r = qi * tq + (lax.broadcasted_iota(jnp.int32, (rows, 1), 0) & (tq - 1))
    qn = qn_ref[...].reshape(rows, LANES)
    qrot = qrot_ref[...].reshape(rows, LANES)

    o_cmp, imps = _cmp_branch(qn, kvc_ref[0], kvc_ref[1], q_pos_r, n_cmp_blocks, impmap_ref, tq)
    selms = [_select_blocks(imp, q_pos_t, n_sel_blocks).astype(BF16) for imp in imps]

    blk_per_chunk = tk // SEL_BLOCK
    n_chunks = (qi * tq + tq + tk - 1) // tk
    e_row = lax.broadcasted_iota(jnp.int32, (LANES, tk), 0)
    e_col = lax.shift_right_logical(lax.broadcasted_iota(jnp.int32, (LANES, tk), 1), 6)
    col_t = lax.broadcasted_iota(jnp.int32, (tq, tk), 1)

    def sel_body(c, carry):
        m, l, acc = carry
        start = pl.multiple_of(c * tk, tk)
        k = sel_ref[pl.ds(start, tk), 0:LANES]
        v = sel_ref[pl.ds(start, tk), LANES:2 * LANES]
        expand = jnp.where(e_row == c * blk_per_chunk + e_col, 1.0, 0.0).astype(BF16)
        causal = (c * tk + col_t) <= q_pos_t
        mask = _group_rows([(_dot(sm, expand) > 0.5) & causal for sm in selms], NSA_GROUP)
        return _online_update(_dot_nt(qrot, k), mask, v, m, l, acc)

    init = (jnp.full((rows, 1), NEG_INF, F32), jnp.zeros((rows, 1), F32), jnp.zeros((rows, LANES), F32))
    _, l, acc = lax.fori_loop(0, n_chunks, sel_body, init)
    o_sel = _finalize(l, acc)

    w_start = pl.multiple_of(jnp.maximum(qi * tq + tq - wlen, 0), tq)
    k = win_ref[pl.ds(w_start, wlen), 0:LANES]
    v = win_ref[pl.ds(w_start, wlen), LANES:2 * LANES]
    d = q_pos_r - (w_start + lax.broadcasted_iota(jnp.int32, (rows, wlen), 1))
    p = _masked_softmax(_dot_nt(qrot, k), (d >= 0) & (d < WINDOW))
    o_win = _dot(p.astype(BF16), v)

    o_ref[...] = _mix_heads(gates_ref[...], o_cmp, o_sel, o_win, tq)


def _importance_map(nc_rows, nc, ns, width):
    i = np.arange(nc_rows)[:, None]
    j = np.arange(width)[None, :]
    lo = np.maximum(i * CMP_STRIDE, j * SEL_BLOCK)
    hi = np.minimum(i * CMP_STRIDE + CMP_BLOCK, (j + 1) * SEL_BLOCK)
    m = np.maximum(hi - lo, 0).astype(np.float32) / CMP_BLOCK
    m = m * (i < nc) * (j < ns)
    return jnp.asarray(m, dtype=BF16)


def _nsa_prompt(P, kvc, batch, seq):
    tq = min(128, seq)
    tk = min(512, seq)
    nq = seq // tq
    n_chunk = seq // CMP_STRIDE
    nc = n_chunk - CMP_BLOCK // CMP_STRIDE + 1
    ns = seq // SEL_BLOCK
    assert ns <= LANES and seq % tk == 0 and seq % CMP_STRIDE == 0 and seq % SEL_BLOCK == 0
    wlen = min(WINDOW + tq, seq)
    impmap = _importance_map(n_chunk, nc, ns, LANES)
    blk = lambda w: pl.BlockSpec((tq, w), lambda b, i: (b * nq + i, 0))
    qblk = pl.BlockSpec((NSA_HEADS, tq, LANES), lambda b, i: (0, b * nq + i, 0))
    per_b = lambda a: pl.BlockSpec((None,) + a.shape[1:], lambda b, i: (b,) + (0,) * (a.ndim - 1))
    selb = P['selb'].reshape(batch, seq, NSA_KV_WIDTH)
    winb = P['winb'].reshape(batch, seq, NSA_KV_WIDTH)
    return pl.pallas_call(
        functools.partial(_nsa_prompt_kernel, tq=tq, tk=tk, n_cmp_blocks=nc, n_sel_blocks=ns, wlen=wlen),
        grid=(batch, nq),
        in_specs=[qblk, qblk, per_b(kvc), per_b(selb), per_b(winb), blk(LANES),
                  pl.BlockSpec(impmap.shape, lambda b, i: (0, 0))],
        out_specs=blk(NSA_WIDTH),
        out_shape=jax.ShapeDtypeStruct((batch * seq, NSA_WIDTH), F32),
        compiler_params=_cparams(("parallel", "parallel")),
        name="nsa_prompt",
    )(P['qn'], P['qrot'], kvc, selb, winb, P['gates'], impmap)


def _finish_kernel(x_ref, omla_ref, onsa_ref, szm_ref, szn_ref, wo_ref, g_ref, b_ref, y_ref):
    mixed = jnp.concatenate([omla_ref[...] * szm_ref[...], onsa_ref[...] * szn_ref[...]], axis=-1)
    h = ALPHA * x_ref[...] + _dot(mixed.astype(BF16), wo_ref[...])
    mu = jnp.mean(h, axis=-1, keepdims=True)
    d = h - mu
    var = jnp.mean(d * d, axis=-1, keepdims=True)
    y_ref[...] = d * lax.rsqrt(var + LN_EPS) * g_ref[...] + b_ref[...]


def _finish(x2d, omla, onsa, szm, szn, wts, tm):
    n = x2d.shape[0]
    row = lambda w: pl.BlockSpec((tm, w), lambda i: (i, 0))
    full = lambda a: pl.BlockSpec(a.shape, lambda i: (0,) * a.ndim)
    return pl.pallas_call(
        _finish_kernel,
        grid=(n // tm,),
        in_specs=[row(D_MODEL), row(MLA_WIDTH), row(NSA_WIDTH), row(MLA_WIDTH), row(NSA_WIDTH),
                  full(wts['w_o']), full(wts['ln_g']), full(wts['ln_b'])],
        out_specs=row(D_MODEL),
        out_shape=jax.ShapeDtypeStruct((n, D_MODEL), F32),
        compiler_params=_cparams(("parallel",)),
        name="finish",
    )(x2d, omla, onsa, szm, szn, wts['w_o'], wts['ln_g'], wts['ln_b'])


def _page_copies(pt_ref, b, n_pages, make_copy):
    def start(p, _):
        make_copy(pt_ref[b, p], p).start()
        return 0
    lax.fori_loop(0, n_pages, start, 0)


def _page_waits(n_pages, make_copy):
    def wait(p, _):
        make_copy(0, p).wait()
        return 0
    lax.fori_loop(0, n_pages, wait, 0)


def _mla_decode_kernel(pt_ref, q_ref, knew_ref, pool_ref, o_ref, kbuf, sem, *, n_pages, t_new, tk):
    b = pl.program_id(0)
    nb = pl.num_programs(0)
    slot = b % 2
    past = n_pages * PAGE_SIZE

    def copy(sl):
        return lambda page, p: pltpu.make_async_copy(
            pool_ref.at[page], kbuf.at[sl, pl.ds(pl.multiple_of(p * PAGE_SIZE, PAGE_SIZE), PAGE_SIZE), :], sem.at[sl])

    @pl.when(b == 0)
    def _():
        _page_copies(pt_ref, 0, n_pages, copy(0))

    @pl.when(b + 1 < nb)
    def _():
        _page_copies(pt_ref, b + 1, n_pages, copy(1 - slot))

    _page_waits(n_pages, copy(slot))

    q = q_ref[...]
    rows = q.shape[0]
    m = jnp.full((rows, 1), NEG_INF, F32)
    l = jnp.zeros((rows, 1), F32)
    acc = jnp.zeros((rows, MLA_KV_LORA), F32)
    true_mask = jnp.full((rows, tk), True)
    for c in range(past // tk):
        kc = kbuf[slot, pl.ds(c * tk, tk), :].astype(BF16)
        m, l, acc = _online_update(_dot_nt(q, kc) * MLA_SCALE, true_mask, kc[:, 0:MLA_KV_LORA], m, l, acc)
    kn = knew_ref[...]
    t_q = lax.broadcasted_iota(jnp.int32, (rows, kn.shape[0]), 0) & (t_new - 1)
    t_k = lax.broadcasted_iota(jnp.int32, (rows, kn.shape[0]), 1)
    m, l, acc = _online_update(_dot_nt(q, kn) * MLA_SCALE, t_k <= t_q, kn[:, 0:MLA_KV_LORA], m, l, acc)
    o_ref[...] = _finalize(l, acc)


def _mla_decode(page_table, q_b, knew_b, pool):
    nb, n_pages = page_table.shape
    rows = q_b.shape[1]
    t_new = rows // MLA_HEADS
    tk = 1024
    past = n_pages * PAGE_SIZE
    assert past % tk == 0 and (t_new & (t_new - 1)) == 0
    grid_spec = pltpu.PrefetchScalarGridSpec(
        num_scalar_prefetch=1,
        grid=(nb,),
        in_specs=[pl.BlockSpec((None, rows, MLA_ROW), lambda b, pt: (b, 0, 0)),
                  pl.BlockSpec((None,) + knew_b.shape[1:], lambda b, pt: (b, 0, 0)),
                  pl.BlockSpec(memory_space=pl.ANY)],
        out_specs=pl.BlockSpec((None, rows, MLA_KV_LORA), lambda b, pt: (b, 0, 0)),
        scratch_shapes=[pltpu.VMEM((2, past, MLA_ROW), F32), pltpu.SemaphoreType.DMA((2,))],
    )
    return pl.pallas_call(
        functools.partial(_mla_decode_kernel, n_pages=n_pages, t_new=t_new, tk=tk),
        grid_spec=grid_spec,
        out_shape=jax.ShapeDtypeStruct((nb, rows, MLA_KV_LORA), F32),
        compiler_params=_cparams(("arbitrary",)),
        name="mla_decode",
    )(page_table, q_b, knew_b, pool)


def _uv_kernel(o_ref, wuv_ref, y_ref):
    rows = o_ref.shape[1]
    y_ref[...] = _uv_project(o_ref[...].reshape(MLA_HEADS * rows, MLA_KV_LORA), wuv_ref, rows)


def _uv(o_heads, wuv):
    rows = o_heads.shape[1]
    return pl.pallas_call(
        _uv_kernel,
        out_shape=jax.ShapeDtypeStruct((rows, MLA_WIDTH), F32),
        name="mla_value_up",
    )(o_heads, wuv)


def _nsa_dec_cmp_kernel(pt_ref, qn_ref, cnew_ref, pool_ref, wc1_ref, pe_ref, b1_ref, w2_ref, impmap_ref,
                        ocmp_ref, selm_ref, xbuf, sem, *, n_pages, t_new, t_pad, n_chunk, n_cmp_blocks, n_sel_blocks):
    b = pl.program_id(0)
    nb = pl.num_programs(0)
    slot = b % 2
    past = n_pages * PAGE_SIZE
    buf_rows = xbuf.shape[2]

    def copy(sl, s):
        return lambda page, p: pltpu.make_async_copy(
            pool_ref.at[page, :, pl.ds(s * LANES, LANES)],
            xbuf.at[sl, s, pl.ds(pl.multiple_of(p * PAGE_SIZE, PAGE_SIZE), PAGE_SIZE), :], sem.at[sl])

    def start_all(bb, sl):
        for s in range(2):
            _page_copies(pt_ref, bb, n_pages, copy(sl, s))

    @pl.when(b == 0)
    def _():
        start_all(0, 0)

    @pl.when(b + 1 < nb)
    def _():
        start_all(b + 1, 1 - slot)

    cnew = cnew_ref[...]
    for s in range(2):
        xbuf[slot, s, pl.ds(past, buf_rows - past), :] = jnp.zeros((buf_rows - past, LANES), F32)
        xbuf[slot, s, pl.ds(past, t_new), :] = cnew[:, s * LANES:(s + 1) * LANES]

    for s in range(2):
        _page_waits(n_pages, copy(slot, s))

    kvc = []
    for s in range(2):
        load = lambda p, s=s: xbuf[slot, s, pl.ds(p, n_chunk, stride=CMP_STRIDE), :]
        kvc.append(_compress_slab(load, wc1_ref, pe_ref, b1_ref, w2_ref, s, n_chunk).astype(BF16))

    rows = NSA_HEADS * t_pad
    q_pos_r = past + (lax.broadcasted_iota(jnp.int32, (rows, 1), 0) & (t_pad - 1))
    q_pos_t = past + lax.broadcasted_iota(jnp.int32, (t_pad, 1), 0)
    o_cmp, imps = _cmp_branch(qn_ref[...], kvc[0][0:n_cmp_blocks], kvc[1][0:n_cmp_blocks], q_pos_r,
                              n_cmp_blocks, impmap_ref, t_pad)
    ocmp_ref[...] = o_cmp
    selm_ref[...] = _group_rows([_select_blocks(imp, q_pos_t, n_sel_blocks) for imp in imps], NSA_GROUP)


def _nsa_decode_cmp(page_table, qn_b, cnew_b, pool, wts, t_new, t_pad):
    nb, n_pages = page_table.shape
    past = n_pages * PAGE_SIZE
    total = past + t_new
    n_chunk_true = -(-total // CMP_STRIDE)
    nc = n_chunk_true - CMP_BLOCK // CMP_STRIDE + 1
    n_chunk = -(-n_chunk_true // 8) * 8
    assert nc % LANES == 0, "compressed-block count must be lane aligned"
    ns = past // SEL_BLOCK + (-(-t_new // SEL_BLOCK))
    sel_w = -(-ns // LANES) * LANES
    impmap = _importance_map(nc, nc, ns, sel_w)
    rows = NSA_HEADS * t_pad
    full = lambda a: pl.BlockSpec(a.shape, lambda b, pt: (0,) * a.ndim)
    per_b = lambda a: pl.BlockSpec((None,) + a.shape[1:], lambda b, pt: (b,) + (0,) * (a.ndim - 1))
    grid_spec = pltpu.PrefetchScalarGridSpec(
        num_scalar_prefetch=1,
        grid=(nb,),
        in_specs=[per_b(qn_b), per_b(cnew_b), pl.BlockSpec(memory_space=pl.ANY),
                  full(wts['wc1']), full(wts['pe']), full(wts['b1']), full(wts['w2']), full(impmap)],
        out_specs=[pl.BlockSpec((None, rows, LANES), lambda b, pt: (b, 0, 0)),
                   pl.BlockSpec((None, rows, sel_w), lambda b, pt: (b, 0, 0))],
        scratch_shapes=[pltpu.VMEM((2, 2, n_chunk * CMP_STRIDE, LANES), F32), pltpu.SemaphoreType.DMA((2,))],
    )
    return pl.pallas_call(
        functools.partial(_nsa_dec_cmp_kernel, n_pages=n_pages, t_new=t_new, t_pad=t_pad, n_chunk=n_chunk,
                          n_cmp_blocks=nc, n_sel_blocks=ns),
        grid_spec=grid_spec,
        out_shape=[jax.ShapeDtypeStruct((nb, rows, LANES), F32),
                   jax.ShapeDtypeStruct((nb, rows, sel_w), F32)],
        compiler_params=_cparams(("arbitrary",)),
        name="nsa_decode_compress",
    )(page_table, qn_b, cnew_b, pool, wts['wc1'], wts['pe'], wts['b1'], wts['w2'], impmap)


def _nsa_dec_sel_kernel(pt_ref, qr_ref, snew_ref, wnew_ref, selm_ref, ocmp_ref, gates_ref, wst_ref, pool_ref,
                        o_ref, kbuf, sem, *, n_pages, t_new, t_pad, tk):
    b = pl.program_id(0)
    nb = pl.num_programs(0)
    slot = b % 2
    past = n_pages * PAGE_SIZE
    rows = NSA_HEADS * t_pad

    def copy(sl):
        return lambda page, p: pltpu.make_async_copy(
            pool_ref.at[page], kbuf.at[sl, pl.ds(pl.multiple_of(p * PAGE_SIZE, PAGE_SIZE), PAGE_SIZE), :], sem.at[sl])

    @pl.when(b == 0)
    def _():
        _page_copies(pt_ref, 0, n_pages, copy(0))

    @pl.when(b + 1 < nb)
    def _():
        _page_copies(pt_ref, b + 1, n_pages, copy(1 - slot))

    _page_waits(n_pages, copy(slot))

    blk_per_chunk = tk // SEL_BLOCK
    lane_lo = lax.broadcasted_iota(jnp.int32, (rows, LANES), 1) < SEL_BLOCK
    n_new = snew_ref.shape[0]
    t_q = lax.broadcasted_iota(jnp.int32, (rows, n_new), 0) & (t_pad - 1)
    t_k = lax.broadcasted_iota(jnp.int32, (rows, n_new), 1)
    new_causal = (t_k <= t_q) & (t_k < t_new)
    w_buf = wst_ref.shape[0]
    t_qw = lax.broadcasted_iota(jnp.int32, (rows, w_buf), 0) & (t_pad - 1)
    i_w = lax.broadcasted_iota(jnp.int32, (rows, w_buf), 1)
    win_mask = (t_qw + w_buf - i_w) < WINDOW
    snew = snew_ref[...]
    wnew = wnew_ref[...]
    qr = qr_ref[...]
    selm = selm_ref[...]

    def fresh():
        return jnp.full((rows, 1), NEG_INF, F32), jnp.zeros((rows, 1), F32), jnp.zeros((rows, LANES), F32)

    m, l, acc = fresh()
    for c in range(past // tk):
        k = kbuf[slot, pl.ds(c * tk, tk), 0:LANES].astype(BF16)
        v = kbuf[slot, pl.ds(c * tk, tk), LANES:2 * LANES].astype(BF16)
        pieces = []
        for i in range(tk // LANES):
            jb = c * blk_per_chunk + 2 * i
            pieces.append(jnp.where(lane_lo, selm[:, jb:jb + 1], selm[:, jb + 1:jb + 2]))
        mask = jnp.concatenate(pieces, axis=-1) > 0.5
        m, l, acc = _online_update(_dot_nt(qr, k), mask, v, m, l, acc)
    nb_past = past // SEL_BLOCK
    mask = (selm[:, nb_past:nb_past + 1] > 0.5) & new_causal
    m, l, acc = _online_update(_dot_nt(qr, snew[:, 0:LANES].astype(BF16)), mask,
                               snew[:, LANES:2 * LANES].astype(BF16), m, l, acc)
    o_sel = _finalize(l, acc)

    m, l, acc = fresh()
    m, l, acc = _online_update(_dot_nt(qr, wst_ref[:, 0:LANES].astype(BF16)), win_mask,
                               wst_ref[:, LANES:2 * LANES].astype(BF16), m, l, acc)
    m, l, acc = _online_update(_dot_nt(qr, wnew[:, 0:LANES].astype(BF16)), new_causal,
                               wnew[:, LANES:2 * LANES].astype(BF16), m, l, acc)
    o_win = _finalize(l, acc)

    o_ref[...] = _mix_heads(gates_ref[...], ocmp_ref[...], o_sel, o_win, t_pad)


def _nsa_decode_sel(page_table, qr_b, snew_b, wnew_b, selm, ocmp, gates_b, win_state, pool, t_new, t_pad):
    nb, n_pages = page_table.shape
    past = n_pages * PAGE_SIZE
    tk = 1024
    assert past % tk == 0
    per_b = lambda a: pl.BlockSpec((None,) + a.shape[1:], lambda b, pt: (b,) + (0,) * (a.ndim - 1))
    grid_spec = pltpu.PrefetchScalarGridSpec(
        num_scalar_prefetch=1,
        grid=(nb,),
        in_specs=[per_b(qr_b), per_b(snew_b), per_b(wnew_b), per_b(selm), per_b(ocmp), per_b(gates_b),
                  per_b(win_state), pl.BlockSpec(memory_space=pl.ANY)],
        out_specs=pl.BlockSpec((None, t_pad, NSA_WIDTH), lambda b, pt: (b, 0, 0)),
        scratch_shapes=[pltpu.VMEM((2, past, NSA_KV_WIDTH), F32), pltpu.SemaphoreType.DMA((2,))],
    )
    return pl.pallas_call(
        functools.partial(_nsa_dec_sel_kernel, n_pages=n_pages, t_new=t_new, t_pad=t_pad, tk=tk),
        grid_spec=grid_spec,
        out_shape=jax.ShapeDtypeStruct((nb, t_pad, NSA_WIDTH), F32),
        compiler_params=_cparams(("arbitrary",)),
        name="nsa_decode_select_window",
    )(page_table, qr_b, snew_b, wnew_b, selm, ocmp, gates_b, win_state, pool)


def _prep_weights(w_in, q_norm_g, w_uq, kv_norm_g, w_uk, w_uv, cmp_pos_emb, cmp_w1, cmp_b1, cmp_w2, w_o, ln_g, ln_b):
    cuts = np.cumsum([MLA_Q_LORA, MLA_KV_LORA, MLA_ROPE_DIM, MLA_WIDTH, NSA_WIDTH, NSA_KV_WIDTH, NSA_KV_WIDTH,
                      NSA_KV_WIDTH, 3 * NSA_HEADS])[:].tolist()
    c_q, c_kv, k_r, z_mla, q_n, cmp_kv, sel_kv, win_kv, g_br, z_nsa = jnp.split(w_in, cuts, axis=1)
    pad = jnp.zeros((D_MODEL, LANES - MLA_ROPE_DIM - 3 * NSA_HEADS), w_in.dtype)
    w_in_p = jnp.concatenate([c_q, c_kv, z_mla, q_n, cmp_kv, sel_kv, win_kv, z_nsa, k_r, g_br, pad], axis=1)
    assert w_in_p.shape[1] == IN_WIDTH_P
    uq = w_uq.reshape(MLA_Q_LORA, MLA_HEADS, MLA_NOPE_DIM + MLA_ROPE_DIM)
    w_uq_p = jnp.concatenate([uq[:, :, :MLA_NOPE_DIM].reshape(MLA_Q_LORA, -1),
                              uq[:, :, MLA_NOPE_DIM:].reshape(MLA_Q_LORA, -1)], axis=1)
    w_uk_t = jnp.transpose(w_uk.reshape(MLA_KV_LORA, MLA_HEADS, MLA_NOPE_DIM), (1, 2, 0))
    w_uv_h = jnp.transpose(w_uv.reshape(MLA_KV_LORA, MLA_HEADS, MLA_V_DIM), (1, 0, 2))
    ratio = CMP_BLOCK // CMP_STRIDE
    eye = jnp.eye(NSA_KV_HEADS, dtype=w_in.dtype)
    w1r = cmp_w1.reshape(2, ratio, CMP_STRIDE, NSA_HEAD_DIM, NSA_HEAD_DIM)
    wc1 = jnp.einsum('gh,srpde->spgdrhe', eye, w1r).reshape(2, CMP_STRIDE, LANES, ratio * LANES)
    w2 = jnp.einsum('gh,sde->sgdhe', eye, cmp_w2).reshape(2, LANES, LANES)
    pe = jnp.transpose(cmp_pos_emb, (1, 0, 2))
    pe = jnp.concatenate([pe, pe], axis=-1).reshape(2, ratio, CMP_STRIDE, LANES)
    pe = jnp.transpose(pe, (0, 2, 1, 3))
    pe = jnp.pad(pe, ((0, 0), (0, 0), (0, 8 - ratio), (0, 0)))
    b1 = jnp.concatenate([cmp_b1, cmp_b1], axis=-1).reshape(2, 1, LANES)
    return dict(w_in=w_in_p.astype(BF16), q_norm_g=q_norm_g.reshape(1, -1), w_uq=w_uq_p.astype(BF16),
                w_uk=w_uk_t.astype(BF16), kv_norm_g=kv_norm_g.reshape(1, -1), w_uv=w_uv_h.astype(BF16),
                wc1=wc1.astype(BF16), w2=w2.astype(BF16), pe=pe, b1=b1,
                w_o=w_o.astype(BF16), ln_g=ln_g.reshape(1, -1), ln_b=ln_b.reshape(1, -1))


def _prompt_layer(x, wts):
    batch, seq, _ = x.shape
    x2d = x.reshape(batch * seq, D_MODEL)
    tm = min(256, seq)
    P = _project(x2d, _rope_tables(jnp.arange(seq, dtype=jnp.int32)), wts, tm)
    o_mla = _mla_prompt(P['qmla'], P['mrowb'], wts['w_uv'], batch, seq)
    kvc = _compress_prompt(P['cmps'], wts, batch, seq)
    o_nsa = _nsa_prompt(P, kvc, batch, seq)
    y = _finish(x2d, o_mla, o_nsa, P['szm'], P['szn'], wts, tm)
    kvd = (2, NSA_KV_HEADS, NSA_HEAD_DIM)
    w_keep = min(WINDOW, seq)
    return (y.reshape(batch, seq, D_MODEL),
            P['mrow'].reshape(batch, seq, MLA_ROW),
            P['cmp'].reshape((batch, seq) + kvd),
            P['sel'].reshape((batch, seq) + kvd),
            P['win'].reshape((batch, seq) + kvd)[:, seq - w_keep:])


def _sample_layer(x, cache_mla, cache_cmp, cache_sel, win_state, page_table, wts):
    nb, t_new, _ = x.shape
    n_pages = page_table.shape[1]
    past = n_pages * PAGE_SIZE
    n = nb * t_new
    t_pad = 8
    x2d = x.reshape(n, D_MODEL)
    pos = past + (jnp.arange(n, dtype=jnp.int32) % t_new)
    P = _project(x2d, _rope_tables(pos), wts, n)
    kvd = (2, NSA_KV_HEADS, NSA_HEAD_DIM)
    n_pool = cache_mla.shape[0]

    q_b = jnp.transpose(P['qmla'].reshape(MLA_HEADS, nb, t_new, MLA_ROW), (1, 0, 2, 3)).reshape(nb, -1, MLA_ROW)
    knew_b = jnp.pad(P['mrowb'].reshape(nb, t_new, MLA_ROW), ((0, 0), (0, 16 - t_new), (0, 0)))
    o_lat = _mla_decode(page_table, q_b, knew_b, cache_mla)
    o_heads = jnp.transpose(o_lat.reshape(nb, MLA_HEADS, t_new, MLA_KV_LORA), (1, 0, 2, 3)).reshape(MLA_HEADS, n, -1)
    o_mla = _uv(o_heads, wts['w_uv'])

    def q_rows(q):
        q = jnp.transpose(q.reshape(NSA_HEADS, nb, t_new, LANES), (1, 0, 2, 3))
        q = jnp.pad(q, ((0, 0), (0, 0), (0, t_pad - t_new), (0, 0)))
        return q.reshape(nb, NSA_HEADS * t_pad, LANES)

    pool_cmp = cache_cmp.reshape(n_pool, PAGE_SIZE, NSA_KV_WIDTH)
    pool_sel = cache_sel.reshape(n_pool, PAGE_SIZE, NSA_KV_WIDTH)
    cnew_b = P['cmp'].reshape(nb, t_new, NSA_KV_WIDTH)
    ocmp, selm = _nsa_decode_cmp(page_table, q_rows(P['qn']), cnew_b, pool_cmp, wts, t_new, t_pad)
    pad_rows = lambda a, r: jnp.pad(a.reshape(nb, t_new, -1), ((0, 0), (0, r - t_new), (0, 0)))
    wst = win_state.reshape(nb, -1, NSA_KV_WIDTH)
    o_nsa = _nsa_decode_sel(page_table, q_rows(P['qrot']), pad_rows(P['sel'], 16), pad_rows(P['win'], 16), selm,
                            ocmp, pad_rows(P['gates'], t_pad), wst, pool_sel, t_new, t_pad)
    o_nsa = o_nsa[:, :t_new].reshape(n, NSA_WIDTH)

    y = _finish(x2d, o_mla, o_nsa, P['szm'], P['szn'], wts, n)
    win_all = jnp.concatenate([wst, P['win'].reshape(nb, t_new, NSA_KV_WIDTH)], axis=1)[:, t_new:]
    return (y.reshape(nb, t_new, D_MODEL),
            P['mrow'].reshape(nb, t_new, MLA_ROW),
            P['cmp'].reshape((nb, t_new) + kvd),
            P['sel'].reshape((nb, t_new) + kvd),
            win_all.reshape((nb, win_all.shape[1]) + kvd))


def kernel(x_prompt, x_sample, cache_mla, cache_cmp_kv, cache_sel_kv, state_win_kv, page_table, w_in, q_norm_g, w_uq,
           kv_norm_g, w_uk, w_uv, cmp_pos_emb, cmp_w1, cmp_b1, cmp_w2, w_o, ln_g, ln_b):
    assert w_in.shape[0] == DEPTH
    wts = _prep_weights(w_in[0], q_norm_g[0], w_uq[0], kv_norm_g[0], w_uk[0], w_uv[0], cmp_pos_emb[0], cmp_w1[0],
                        cmp_b1[0], cmp_w2[0], w_o[0], ln_g[0], ln_b[0])
    yp, p_mla, p_cmp, p_sel, p_win = _prompt_layer(x_prompt, wts)
    ys, s_mla, s_cmp, s_sel, s_win = _sample_layer(x_sample, cache_mla[0], cache_cmp_kv[0], cache_sel_kv[0],
                                                   state_win_kv[0], page_table, wts)
    add_depth = lambda a: a[None]
    return (yp, ys) + tuple(add_depth(a) for a in (p_mla, p_cmp, p_sel, p_win, s_mla, s_cmp, s_sel, s_win))
```

```python
import functools

import numpy as np
import jax
import jax.numpy as jnp
from jax import lax
from jax.experimental import pallas as pl
from jax.experimental.pallas import tpu as pltpu

F32 = jnp.float32
BF16 = jnp.bfloat16

D_MODEL = 1024
PAGE_SIZE = 128
MLA_HEADS = 8
MLA_V_DIM = 64
MLA_NOPE_DIM = 64
MLA_ROPE_DIM = 32
MLA_Q_LORA = 256
MLA_KV_LORA = 128
MLA_WIDTH = MLA_HEADS * MLA_V_DIM
MLA_ROW = MLA_KV_LORA + MLA_ROPE_DIM
NSA_HEADS = 8
NSA_KV_HEADS = 2
NSA_HEAD_DIM = 64
NSA_GROUP = NSA_HEADS // NSA_KV_HEADS
NSA_WIDTH = NSA_HEADS * NSA_HEAD_DIM
NSA_KV_WIDTH = 2 * NSA_KV_HEADS * NSA_HEAD_DIM
CMP_BLOCK = 32
CMP_STRIDE = 16
SEL_BLOCK = 64
SEL_TOP_N = 16
WINDOW = 512
Q_BLOCK = 128
ROPE_THETA = 10000.0
RMS_EPS = 1e-6
LN_EPS = 1e-5
NEG_INF = -1e30
MLA_SCALE = (MLA_NOPE_DIM + MLA_ROPE_DIM) ** -0.5
NSA_SCALE = NSA_HEAD_DIM ** -0.5
DEPTH = 1
ALPHA = (2 * DEPTH) ** 0.25

LANES = 128
LOG2E = 1.4426950408889634
MLA_KROW = 2 * LANES
MLA_ONES_LANE = MLA_ROW
NSA_KROW = 3 * LANES

SEG_CQ = (0, 256)
SEG_CKV = (256, 128)
SEG_ZMLA = (384, 512)
SEG_QN = (896, 512)
SEG_CMP = (1408, 256)
SEG_SEL = (1664, 256)
SEG_WIN = (1920, 256)
SEG_ZNSA = (2176, 512)
SEG_MISC = (2688, 128)
IN_WIDTH_P = 2816
MISC_GATE0 = MLA_ROPE_DIM

VMEM_LIMIT = 48 * 1024 * 1024


def _cparams(sem):
    return pltpu.CompilerParams(dimension_semantics=sem, vmem_limit_bytes=VMEM_LIMIT)


def _sigmoid(x):
    return 1.0 / (1.0 + jnp.exp(-x))


def _silu(x):
    return x * _sigmoid(x)


def _rope_slab(x, c, sa, sb, half):
    return x * c + pltpu.roll(x, LANES - half, 1) * sa + pltpu.roll(x, half, 1) * sb


def _dot_nt(a, b):
    return lax.dot_general(a, b, (((1,), (1,)), ((), ())), preferred_element_type=F32)


def _dot(a, b):
    return jnp.dot(a, b, preferred_element_type=F32)


def _masked_softmax(s, mask):
    sm = jnp.where(mask, s, NEG_INF)
    m = jnp.max(sm, axis=-1, keepdims=True)
    e = jnp.exp2(sm - m)
    p = e / jnp.sum(e, axis=-1, keepdims=True)
    return jnp.where(mask, p, 0.0)


def _online_update(s, mask, v, m, l, acc, v_transposed=False):
    sm = s if mask is None else jnp.where(mask, s, NEG_INF)
    m_new = jnp.maximum(m, jnp.max(sm, axis=-1, keepdims=True))
    alpha = jnp.exp2(m - m_new)
    p = jnp.exp2(sm - m_new)
    if mask is not None:
        p = jnp.where(mask, p, 0.0)
    l_new = alpha * l + jnp.sum(p, axis=-1, keepdims=True)
    pv = _dot_nt(p.astype(BF16), v) if v_transposed else _dot(p.astype(BF16), v)
    return m_new, l_new, alpha * acc + pv


def _finalize(l, acc):
    ok = l > 0.0
    return jnp.where(ok, acc / jnp.where(ok, l, 1.0), 0.0)


def _flash_step(s, bias, v_ext, m, acc):
    sm = s if bias is None else s + bias
    m_new = jnp.maximum(m, jnp.max(sm, axis=-1, keepdims=True))
    p = jnp.exp2(sm - m_new)
    return m_new, jnp.exp2(m - m_new) * acc + _dot(p.astype(BF16), v_ext)


def _flash_result(acc, ones_lane):
    return acc[:, 0:LANES] * (1.0 / acc[:, ones_lane:ones_lane + 1])


def _tile_rows(x, reps):
    return jnp.concatenate([x] * reps, axis=0)


def _select_blocks(imp, q_pos, n_sel_blocks):
    rows, width = imp.shape
    j = lax.broadcasted_iota(jnp.int32, (rows, width), 1)
    cur = lax.shift_right_logical(q_pos, 6)
    avail = (j * SEL_BLOCK <= q_pos) & (j < n_sel_blocks)
    forced = (j == 0) | (j == cur) | (j == cur - 1)
    val = jnp.where(avail, jnp.where(forced, jnp.inf, imp), -jnp.inf)
    rank = jnp.zeros((rows, width), F32)
    for jp in range(n_sel_blocks):
        col = val[:, jp:jp + 1]
        ahead = (col > val) | ((col == val) & (j > jp))
        rank = rank + jnp.where(ahead, 1.0, 0.0)
    keep = (rank < float(min(SEL_TOP_N, n_sel_blocks))) & avail
    return jnp.where(keep, 1.0, 0.0)


def _select_blocks_t(imp_t, q_pos, n_sel_blocks):
    n_rows, width = imp_t.shape
    j = lax.broadcasted_iota(jnp.int32, (n_rows, width), 0)
    cur = lax.shift_right_logical(q_pos, 6)
    avail = (j * SEL_BLOCK <= q_pos) & (j < n_sel_blocks)
    forced = (j == 0) | (j == cur) | (j == cur - 1)
    val = jnp.where(avail, jnp.where(forced, jnp.inf, imp_t), -jnp.inf)
    n_pieces = -(-n_sel_blocks // 8)
    pieces = [val[8 * v:8 * v + 8] for v in range(n_pieces)]
    jrow = lax.broadcasted_iota(jnp.int32, (8, width), 0)
    ranks = [jnp.zeros((8, width), F32) for _ in range(n_pieces)]
    for jp in range(n_sel_blocks):
        cand = jnp.broadcast_to(val[jp:jp + 1], (8, width))
        for v in range(n_pieces):
            if 8 * v > jp:
                ahead = cand >= pieces[v]
            elif 8 * v + 7 < jp:
                ahead = cand > pieces[v]
            else:
                ahead = (cand > pieces[v]) | ((cand == pieces[v]) & (jrow > jp - 8 * v))
            ranks[v] = ranks[v] + jnp.where(ahead, 1.0, 0.0)
    if n_rows > 8 * n_pieces:
        ranks.append(jnp.zeros((n_rows - 8 * n_pieces, width), F32))
    rank = jnp.concatenate(ranks, axis=0)
    keep = (rank < float(min(SEL_TOP_N, n_sel_blocks))) & avail
    return jnp.where(keep, 1.0, 0.0)


def _proj_kernel(x_ref, w_ref, qg_ref, wuq_ref, wuk_ref, kvg_ref, tab_ref,
                 qmla_ref, mrow_ref, mrowb_ref, szm_ref, qn_ref, qrot_ref, cmp_ref, cmps_ref,
                 sel_ref, selb_ref, win_ref, winb_ref, gates_ref, szn_ref):
    xb = x_ref[...].astype(BF16)

    def seg(s):
        return _dot(xb, w_ref[:, s[0]:s[0] + s[1]])

    cn, san, sbn = tab_ref[0], tab_ref[1], tab_ref[2]
    cm, sam, sbm = tab_ref[3], tab_ref[4], tab_ref[5]

    cq = seg(SEG_CQ)
    r = cq * lax.rsqrt(jnp.mean(cq * cq, axis=-1, keepdims=True) + RMS_EPS) * qg_ref[...]
    q = _dot(r.astype(BF16), wuq_ref[...])
    nope_w = MLA_HEADS * MLA_NOPE_DIM
    lane = lax.broadcasted_iota(jnp.int32, (q.shape[0], LANES), 1)
    for h in range(MLA_HEADS):
        ql = _dot(q[:, h * MLA_NOPE_DIM:(h + 1) * MLA_NOPE_DIM].astype(BF16), wuk_ref[h])
        qmla_ref[h, :, 0:LANES] = (ql * (MLA_SCALE * LOG2E)).astype(BF16)
    heads_per_slab = LANES // MLA_ROPE_DIM
    for jj in range(MLA_HEADS // heads_per_slab):
        qr = _rope_slab(q[:, nope_w + jj * LANES: nope_w + (jj + 1) * LANES], cm, sam, sbm, MLA_ROPE_DIM // 2)
        qr = qr * (MLA_SCALE * LOG2E)
        for hh in range(heads_per_slab):
            front = qr if hh == 0 else pltpu.roll(qr, LANES - hh * MLA_ROPE_DIM, 1)
            qmla_ref[jj * heads_per_slab + hh, :, LANES:2 * LANES] = (
                jnp.where(lane < MLA_ROPE_DIM, front, 0.0).astype(BF16))

    ckv = seg(SEG_CKV)
    lat = ckv * lax.rsqrt(jnp.mean(ckv * ckv, axis=-1, keepdims=True) + RMS_EPS) * kvg_ref[...]
    misc = seg(SEG_MISC)
    kr = _rope_slab(misc, cm, sam, sbm, MLA_ROPE_DIM // 2)
    mrow_ref[:, 0:MLA_KV_LORA] = lat
    mrow_ref[:, MLA_KV_LORA:MLA_ROW] = kr[:, 0:MLA_ROPE_DIM]
    mrowb_ref[:, 0:LANES] = lat.astype(BF16)
    ones_col = jnp.where(lane == MLA_ONES_LANE - LANES, 1.0, 0.0)
    mrowb_ref[:, LANES:2 * LANES] = jnp.where(lane < MLA_ROPE_DIM, kr, ones_col).astype(BF16)
    gates_ref[...] = _sigmoid(misc)

    szm_ref[...] = _silu(seg(SEG_ZMLA))
    szn_ref[...] = _silu(seg(SEG_ZNSA))

    qn = seg(SEG_QN)
    for jj in range(NSA_WIDTH // LANES):
        raw = qn[:, jj * LANES:(jj + 1) * LANES] * (NSA_SCALE * LOG2E)
        rot = _rope_slab(qn[:, jj * LANES:(jj + 1) * LANES], cn, san, sbn, NSA_HEAD_DIM // 2) * (NSA_SCALE * LOG2E)
        for src, dst in ((raw, qn_ref), (rot, qrot_ref)):
            swapped = pltpu.roll(src, NSA_HEAD_DIM, 1)
            for half in range(2):
                hd = 2 * jj + half
                g = hd // NSA_GROUP
                keep = (lane >= g * NSA_HEAD_DIM) & (lane < (g + 1) * NSA_HEAD_DIM)
                dst[hd] = jnp.where(keep, src if half == g else swapped, 0.0).astype(BF16)

    cmpv = seg(SEG_CMP)
    cmp_ref[...] = cmpv
    cmps_ref[0] = cmpv[:, 0:LANES]
    cmps_ref[1] = cmpv[:, LANES:2 * LANES]

    for src, dst, dstb in ((SEG_SEL, sel_ref, selb_ref), (SEG_WIN, win_ref, winb_ref)):
        kv = seg(src)
        k = _rope_slab(kv[:, 0:LANES], cn, san, sbn, NSA_HEAD_DIM // 2)
        v = kv[:, LANES:2 * LANES]
        dst[:, 0:LANES] = k
        dst[:, LANES:2 * LANES] = v
        dstb[:, 0:LANES] = k.astype(BF16)
        dstb[:, LANES:2 * LANES] = v.astype(BF16)
        dstb[:, 2 * LANES:3 * LANES] = jnp.where(lane == 0, 1.0, 0.0).astype(BF16)


def _rope_tables(pos):
    def tab(d):
        inv = 1.0 / (ROPE_THETA ** (jnp.arange(0, d, 2, dtype=F32) / d))
        ang = pos.astype(F32)[:, None] * inv[None, :]
        cos, sin = jnp.cos(ang), jnp.sin(ang)
        zero = jnp.zeros_like(sin)
        reps = LANES // d
        return (jnp.tile(jnp.concatenate([cos, cos], -1), (1, reps)),
                jnp.tile(jnp.concatenate([-sin, zero], -1), (1, reps)),
                jnp.tile(jnp.concatenate([zero, sin], -1), (1, reps)))
    return jnp.stack(tab(NSA_HEAD_DIM) + tab(MLA_ROPE_DIM))


def _project(x2d, tabs, wts, tm):
    n = x2d.shape[0]
    p_rows = tabs.shape[1]
    assert n % tm == 0 and p_rows % tm == 0
    nt = p_rows // tm
    row = lambda w: pl.BlockSpec((tm, w), lambda i: (i, 0))
    full = lambda a: pl.BlockSpec(a.shape, lambda i: (0,) * a.ndim)
    out_shapes = dict(
        qmla=((MLA_HEADS, n, MLA_KROW), BF16, pl.BlockSpec((MLA_HEADS, tm, MLA_KROW), lambda i: (0, i, 0))),
        mrow=((n, MLA_ROW), F32, row(MLA_ROW)),
        mrowb=((n, MLA_KROW), BF16, row(MLA_KROW)),
        szm=((n, MLA_WIDTH), F32, row(MLA_WIDTH)),
        qn=((NSA_HEADS, n, LANES), BF16, pl.BlockSpec((NSA_HEADS, tm, LANES), lambda i: (0, i, 0))),
        qrot=((NSA_HEADS, n, LANES), BF16, pl.BlockSpec((NSA_HEADS, tm, LANES), lambda i: (0, i, 0))),
        cmp=((n, NSA_KV_WIDTH), F32, row(NSA_KV_WIDTH)),
        cmps=((2, n, LANES), F32, pl.BlockSpec((2, tm, LANES), lambda i: (0, i, 0))),
        sel=((n, NSA_KV_WIDTH), F32, row(NSA_KV_WIDTH)),
        selb=((n, NSA_KROW), BF16, row(NSA_KROW)),
        win=((n, NSA_KV_WIDTH), F32, row(NSA_KV_WIDTH)),
        winb=((n, NSA_KROW), BF16, row(NSA_KROW)),
        gates=((n, LANES), F32, row(LANES)),
        szn=((n, NSA_WIDTH), F32, row(NSA_WIDTH)),
    )
    names = list(out_shapes)
    outs = pl.pallas_call(
        _proj_kernel,
        grid=(n // tm,),
        in_specs=[row(D_MODEL), full(wts['w_in']), full(wts['q_norm_g']), full(wts['w_uq']), full(wts['w_uk']),
                  full(wts['kv_norm_g']), pl.BlockSpec((6, tm, LANES), lambda i: (0, i % nt, 0))],
        out_specs=[out_shapes[k][2] for k in names],
        out_shape=[jax.ShapeDtypeStruct(out_shapes[k][0], out_shapes[k][1]) for k in names],
        compiler_params=_cparams(("parallel",)),
        name="projection",
    )(x2d, wts['w_in'], wts['q_norm_g'], wts['w_uq'], wts['w_uk'], wts['kv_norm_g'], tabs)
    return dict(zip(names, outs))


def _uv_project(o, wuv_ref, rows):
    return jnp.concatenate(
        [_dot(o[h * rows:(h + 1) * rows].astype(BF16), wuv_ref[h]) for h in range(MLA_HEADS)], axis=-1)


def _mla_prompt_kernel(q_ref, k_ref, wuv_ref, o_ref, *, tq, tk):
    qi = pl.program_id(1)
    rows = MLA_HEADS * tq
    q = q_ref[...].reshape(rows, MLA_KROW)
    n_full = (qi * tq) // tk
    n_tail = qi - n_full * (tk // tq) + 1

    def full_body(c, carry):
        kc = k_ref[pl.ds(pl.multiple_of(c * tk, tk), tk), :]
        return _flash_step(_dot_nt(q, kc), None, kc, *carry)

    def tail_body(r, carry):
        start = pl.multiple_of(n_full * tk + r * tq, tq)
        kc = k_ref[pl.ds(start, tq), :]
        k_pos = start + lax.broadcasted_iota(jnp.int32, (tq, tq), 1)
        q_pos = qi * tq + lax.broadcasted_iota(jnp.int32, (tq, tq), 0)
        bias = _tile_rows(jnp.where(k_pos <= q_pos, 0.0, NEG_INF), MLA_HEADS)
        return _flash_step(_dot_nt(q, kc), bias, kc, *carry)

    carry = (jnp.full((rows, 1), NEG_INF, F32), jnp.zeros((rows, MLA_KROW), F32))
    carry = lax.fori_loop(0, n_full, full_body, carry)
    _, acc = lax.fori_loop(0, n_tail, tail_body, carry)
    o_ref[...] = _uv_project(_flash_result(acc, MLA_ONES_LANE), wuv_ref, tq)


def _mla_prompt(qmla, mrowb, wuv, batch, seq):
    tq = min(128, seq)
    tk = min(512, seq)
    nq = seq // tq
    assert tk % tq == 0 and seq % tk == 0
    return pl.pallas_call(
        functools.partial(_mla_prompt_kernel, tq=tq, tk=tk),
        grid=(batch, nq),
        in_specs=[pl.BlockSpec((MLA_HEADS, tq, MLA_KROW), lambda b, i: (0, b * nq + i, 0)),
                  pl.BlockSpec((None, seq, MLA_KROW), lambda b, i: (b, 0, 0)),
                  pl.BlockSpec(wuv.shape, lambda b, i: (0, 0, 0))],
        out_specs=pl.BlockSpec((tq, MLA_WIDTH), lambda b, i: (b * nq + i, 0)),
        out_shape=jax.ShapeDtypeStruct((batch * seq, MLA_WIDTH), F32),
        compiler_params=_cparams(("parallel", "parallel")),
        name="mla_prompt",
    )(qmla, mrowb.reshape(batch, seq, MLA_KROW), wuv)


CMP_PAIRS = CMP_STRIDE // 2


def _compress_slab(load_pair, wc1_ref, pe_ref, b1_ref, w2_ref, s, n_chunk):
    proj = jnp.zeros((n_chunk, 2 * LANES), F32)
    pe = jnp.zeros((8, 2 * LANES), F32)
    for j in range(CMP_PAIRS):
        w = wc1_ref[s, j]
        proj = proj + _dot(load_pair(j).astype(BF16), w)
        pe = pe + _dot(pe_ref[s, j].astype(BF16), w)
    bias = b1_ref[s] + pe[0:1, 0:LANES] + pe[1:2, LANES:2 * LANES]
    hid = proj[:, 0:LANES] + pltpu.roll(proj[:, LANES:2 * LANES], n_chunk - 1, 0) + bias
    return _dot(_silu(hid).astype(BF16), w2_ref[s])


def _compress_prompt_kernel(x_ref, wc1_ref, pe_ref, b1_ref, w2_ref, o_ref, *, n_chunk):
    for s in range(2):
        row = lambda p, s=s: x_ref[s, pl.ds(p, n_chunk, stride=CMP_STRIDE), :]
        load = lambda j, row=row: jnp.concatenate([row(2 * j), row(2 * j + 1)], axis=-1)
        o_ref[s] = _compress_slab(load, wc1_ref, pe_ref, b1_ref, w2_ref, s, n_chunk).astype(BF16)


def _compress_prompt(cmps, wts, batch, seq):
    n_chunk = seq // CMP_STRIDE
    full = lambda a: pl.BlockSpec(a.shape, lambda b: (0,) * a.ndim)
    return pl.pallas_call(
        functools.partial(_compress_prompt_kernel, n_chunk=n_chunk),
        grid=(batch,),
        in_specs=[pl.BlockSpec((2, seq, LANES), lambda b: (0, b, 0)),
                  full(wts['wc1']), full(wts['pe']), full(wts['b1']), full(wts['w2'])],
        out_specs=pl.BlockSpec((None, 2, n_chunk, LANES), lambda b: (b, 0, 0, 0)),
        out_shape=jax.ShapeDtypeStruct((batch, 2, n_chunk, LANES), BF16),
        compiler_params=_cparams(("parallel",)),
        name="compress_prompt",
    )(cmps, wts['wc1'], wts['pe'], wts['b1'], wts['w2'])


def _cmp_branch(q, kc, vc, q_pos_rows, n_cmp_blocks, impmap_ref, rows_t, imp_transposed=False):
    s = _dot_nt(q, kc)
    n = lax.broadcasted_iota(jnp.int32, (rows_t, s.shape[1]), 1)
    visible = (n * CMP_STRIDE + (CMP_BLOCK - 1) <= q_pos_rows[0:rows_t]) & (n < n_cmp_blocks)
    sm = s + _tile_rows(jnp.where(visible, 0.0, NEG_INF), NSA_HEADS)
    e = jnp.exp2(sm - jnp.max(sm, axis=-1, keepdims=True))
    any_visible = (q_pos_rows >= CMP_BLOCK - 1) & (n_cmp_blocks > 0)
    p = e * jnp.where(any_visible, 1.0 / jnp.sum(e, axis=-1, keepdims=True), 0.0)
    o_cmp = _dot(p.astype(BF16), vc)
    imps = []
    for g in range(NSA_KV_HEADS):
        base = g * NSA_GROUP * rows_t
        psum = p[base:base + rows_t]
        for hh in range(1, NSA_GROUP):
            psum = psum + p[base + hh * rows_t: base + (hh + 1) * rows_t]
        hi = psum.astype(BF16)
        lo = (psum - hi.astype(F32)).astype(BF16)
        if imp_transposed:
            imps.append(_dot_nt(impmap_ref[...], hi) + _dot_nt(impmap_ref[...], lo))
        else:
            imps.append(_dot(hi, impmap_ref[...]) + _dot(lo, impmap_ref[...]))
    return o_cmp, imps


def _mix_heads(gates, o_cmp, o_sel, o_win, rows_t):
    outs = []
    for hd in range(NSA_HEADS):
        g = hd // NSA_GROUP
        rs = slice(hd * rows_t, (hd + 1) * rows_t)
        ls = slice(g * NSA_HEAD_DIM, (g + 1) * NSA_HEAD_DIM)
        gcol = lambda br: gates[:, MISC_GATE0 + br * NSA_HEADS + hd: MISC_GATE0 + br * NSA_HEADS + hd + 1]
        outs.append(gcol(0) * o_cmp[rs, ls] + gcol(1) * o_sel[rs, ls] + gcol(2) * o_win[rs, ls])
    return jnp.concatenate(outs, axis=-1)


def _group_rows(per_group, reps):
    return jnp.concatenate([per_group[g] for g in range(NSA_KV_HEADS) for _ in range(reps)], axis=0)


def _nsa_prompt_kernel(qn_ref, qrot_ref, kvc_ref, sel_ref, win_ref, gates_ref, impmap_ref, o_ref,
                       *, tq, tk, n_cmp_blocks, n_sel_blocks, wlen):
    qi = pl.program_id(1)
    rows = NSA_HEADS * tq
    q_pos_t = qi * tq + lax.broadcasted_iota(jnp.int32, (tq, 1), 0)
    q_pos_r = qi * tq + (lax.broadcasted_iota(jnp.int32, (rows, 1), 0) & (tq - 1))
    qn = qn_ref[...].reshape(rows, LANES)
    qrot = qrot_ref[...].reshape(rows, LANES)

    o_cmp, imps_t = _cmp_branch(qn, kvc_ref[0], kvc_ref[1], q_pos_r, n_cmp_blocks, impmap_ref, tq,
                                imp_transposed=True)
    q_pos_lane = qi * tq + lax.broadcasted_iota(jnp.int32, (1, tq), 1)
    selms = [_select_blocks_t(imp_t, q_pos_lane, n_sel_blocks).T.astype(BF16) for imp_t in imps_t]

    blk_per_chunk = tk // SEL_BLOCK
    n_chunks = (qi * tq + tq + tk - 1) // tk
    e_row = lax.broadcasted_iota(jnp.int32, (LANES, tk), 0)
    e_col = lax.shift_right_logical(lax.broadcasted_iota(jnp.int32, (LANES, tk), 1), 6)
    col_t = lax.broadcasted_iota(jnp.int32, (tq, tk), 1)

    def sel_body(c, carry):
        start = pl.multiple_of(c * tk, tk)
        k = sel_ref[pl.ds(start, tk), 0:LANES]
        v_ext = sel_ref[pl.ds(start, tk), LANES:3 * LANES]
        expand = jnp.where(e_row == c * blk_per_chunk + e_col, 1.0, 0.0).astype(BF16)
        causal = (c * tk + col_t) <= q_pos_t
        bias = _group_rows([jnp.where((_dot(sm, expand) > 0.5) & causal, 0.0, NEG_INF) for sm in selms], NSA_GROUP)
        return _flash_step(_dot_nt(qrot, k), bias, v_ext, *carry)

    init = (jnp.full((rows, 1), NEG_INF, F32), jnp.zeros((rows, 2 * LANES), F32))
    _, acc = lax.fori_loop(0, n_chunks, sel_body, init)
    o_sel = _flash_result(acc, LANES)

    w_start = pl.multiple_of(jnp.maximum(qi * tq + tq - wlen, 0), tq)
    k = win_ref[pl.ds(w_start, wlen), 0:LANES]
    v_ext = win_ref[pl.ds(w_start, wlen), LANES:3 * LANES]
    d = q_pos_t - (w_start + lax.broadcasted_iota(jnp.int32, (tq, wlen), 1))
    bias = _tile_rows(jnp.where((d >= 0) & (d < WINDOW), 0.0, NEG_INF), NSA_HEADS)
    _, acc = _flash_step(_dot_nt(qrot, k), bias, v_ext, *init)
    o_win = _flash_result(acc, LANES)

    o_ref[...] = _mix_heads(gates_ref[...], o_cmp, o_sel, o_win, tq)


def _importance_map(nc_rows, nc, ns, width):
    i = np.arange(nc_rows)[:, None]
    j = np.arange(width)[None, :]
    lo = np.maximum(i * CMP_STRIDE, j * SEL_BLOCK)
    hi = np.minimum(i * CMP_STRIDE + CMP_BLOCK, (j + 1) * SEL_BLOCK)
    m = np.maximum(hi - lo, 0).astype(np.float32) / CMP_BLOCK
    m = m * (i < nc) * (j < ns)
    return jnp.asarray(m, dtype=BF16)


def _nsa_prompt(P, kvc, batch, seq):
    tq = min(128, seq)
    tk = min(512, seq)
    nq = seq // tq
    n_chunk = seq // CMP_STRIDE
    nc = n_chunk - CMP_BLOCK // CMP_STRIDE + 1
    ns = seq // SEL_BLOCK
    assert ns <= LANES and seq % tk == 0 and seq % CMP_STRIDE == 0 and seq % SEL_BLOCK == 0
    wlen = min(WINDOW + tq, seq)
    impmap = _importance_map(n_chunk, nc, ns, LANES).T
    blk = lambda w: pl.BlockSpec((tq, w), lambda b, i: (b * nq + i, 0))
    qblk = pl.BlockSpec((NSA_HEADS, tq, LANES), lambda b, i: (0, b * nq + i, 0))
    per_b = lambda a: pl.BlockSpec((None,) + a.shape[1:], lambda b, i: (b,) + (0,) * (a.ndim - 1))
    selb = P['selb'].reshape(batch, seq, NSA_KROW)
    winb = P['winb'].reshape(batch, seq, NSA_KROW)
    return pl.pallas_call(
        functools.partial(_nsa_prompt_kernel, tq=tq, tk=tk, n_cmp_blocks=nc, n_sel_blocks=ns, wlen=wlen),
        grid=(batch, nq),
        in_specs=[qblk, qblk, per_b(kvc), per_b(selb), per_b(winb), blk(LANES),
                  pl.BlockSpec(impmap.shape, lambda b, i: (0, 0))],
        out_specs=blk(NSA_WIDTH),
        out_shape=jax.ShapeDtypeStruct((batch * seq, NSA_WIDTH), F32),
        compiler_params=_cparams(("parallel", "parallel")),
        name="nsa_prompt",
    )(P['qn'], P['qrot'], kvc, selb, winb, P['gates'], impmap)


def _finish_kernel(x_ref, omla_ref, onsa_ref, szm_ref, szn_ref, wo_ref, g_ref, b_ref, y_ref):
    mixed = jnp.concatenate([omla_ref[...] * szm_ref[...], onsa_ref[...] * szn_ref[...]], axis=-1)
    h = ALPHA * x_ref[...] + _dot(mixed.astype(BF16), wo_ref[...])
    mu = jnp.mean(h, axis=-1, keepdims=True)
    d = h - mu
    var = jnp.mean(d * d, axis=-1, keepdims=True)
    y_ref[...] = d * lax.rsqrt(var + LN_EPS) * g_ref[...] + b_ref[...]


def _finish(x2d, omla, onsa, szm, szn, wts, tm):
    n = x2d.shape[0]
    row = lambda w: pl.BlockSpec((tm, w), lambda i: (i, 0))
    full = lambda a: pl.BlockSpec(a.shape, lambda i: (0,) * a.ndim)
    return pl.pallas_call(
        _finish_kernel,
        grid=(n // tm,),
        in_specs=[row(D_MODEL), row(MLA_WIDTH), row(NSA_WIDTH), row(MLA_WIDTH), row(NSA_WIDTH),
                  full(wts['w_o']), full(wts['ln_g']), full(wts['ln_b'])],
        out_specs=row(D_MODEL),
        out_shape=jax.ShapeDtypeStruct((n, D_MODEL), F32),
        compiler_params=_cparams(("parallel",)),
        name="finish",
    )(x2d, omla, onsa, szm, szn, wts['w_o'], wts['ln_g'], wts['ln_b'])


def _page_copies(pt_ref, b, n_pages, make_copy):
    def start(p, _):
        make_copy(pt_ref[b, p], p).start()
        return 0
    lax.fori_loop(0, n_pages, start, 0)


def _page_waits(n_pages, make_copy):
    def wait(p, _):
        make_copy(0, p).wait()
        return 0
    lax.fori_loop(0, n_pages, wait, 0)


def _mla_decode_kernel(pt_ref, q_ref, knew_ref, pool_ref, o_ref, kbuf, sem, *, n_pages, t_new, tk):
    b = pl.program_id(0)
    nb = pl.num_programs(0)
    slot = b % 2
    past = n_pages * PAGE_SIZE

    def copy(sl):
        return lambda page, p: pltpu.make_async_copy(
            pool_ref.at[page], kbuf.at[sl, :, pl.ds(pl.multiple_of(p * PAGE_SIZE, PAGE_SIZE), PAGE_SIZE)], sem.at[sl])

    @pl.when(b == 0)
    def _():
        _page_copies(pt_ref, 0, n_pages, copy(0))

    @pl.when(b + 1 < nb)
    def _():
        _page_copies(pt_ref, b + 1, n_pages, copy(1 - slot))

    _page_waits(n_pages, copy(slot))

    q = q_ref[...]
    rows = q.shape[0]
    m = jnp.full((rows, 1), NEG_INF, F32)
    l = jnp.zeros((rows, 1), F32)
    acc = jnp.zeros((rows, MLA_KV_LORA), F32)
    for c in range(past // tk):
        kt = kbuf[slot, :, pl.ds(c * tk, tk)].astype(BF16)
        m, l, acc = _online_update(_dot(q, kt), None, kt[0:MLA_KV_LORA], m, l, acc, v_transposed=True)
    kn = knew_ref[...]
    t_q = lax.broadcasted_iota(jnp.int32, (rows, kn.shape[0]), 0) & (t_new - 1)
    t_k = lax.broadcasted_iota(jnp.int32, (rows, kn.shape[0]), 1)
    m, l, acc = _online_update(_dot_nt(q, kn), t_k <= t_q, kn[:, 0:MLA_KV_LORA], m, l, acc)
    o_ref[...] = _finalize(l, acc)


def _mla_decode(page_table, q_b, knew_b, pool):
    nb, n_pages = page_table.shape
    rows = q_b.shape[1]
    t_new = rows // MLA_HEADS
    tk = 1024
    past = n_pages * PAGE_SIZE
    assert past % tk == 0 and (t_new & (t_new - 1)) == 0
    grid_spec = pltpu.PrefetchScalarGridSpec(
        num_scalar_prefetch=1,
        grid=(nb,),
        in_specs=[pl.BlockSpec((None, rows, MLA_ROW), lambda b, pt: (b, 0, 0)),
                  pl.BlockSpec((None,) + knew_b.shape[1:], lambda b, pt: (b, 0, 0)),
                  pl.BlockSpec(memory_space=pl.ANY)],
        out_specs=pl.BlockSpec((None, rows, MLA_KV_LORA), lambda b, pt: (b, 0, 0)),
        scratch_shapes=[pltpu.VMEM((2, MLA_ROW, past), F32), pltpu.SemaphoreType.DMA((2,))],
    )
    return pl.pallas_call(
        functools.partial(_mla_decode_kernel, n_pages=n_pages, t_new=t_new, tk=tk),
        grid_spec=grid_spec,
        out_shape=jax.ShapeDtypeStruct((nb, rows, MLA_KV_LORA), F32),
        compiler_params=_cparams(("arbitrary",)),
        name="mla_decode",
    )(page_table, q_b, knew_b, pool)


def _uv_kernel(o_ref, wuv_ref, y_ref):
    rows = o_ref.shape[1]
    y_ref[...] = _uv_project(o_ref[...].reshape(MLA_HEADS * rows, MLA_KV_LORA), wuv_ref, rows)


def _uv(o_heads, wuv):
    rows = o_heads.shape[1]
    return pl.pallas_call(
        _uv_kernel,
        out_shape=jax.ShapeDtypeStruct((rows, MLA_WIDTH), F32),
        name="mla_value_up",
    )(o_heads, wuv)


def _nsa_dec_cmp_kernel(pt_ref, qn_ref, cnew_ref, pool_ref, perm_ref, wc1_ref, pe_ref, b1_ref, w2_ref, impmap_ref,
                        ocmp_ref, selm_ref, pbuf, xrows, sem, *, n_pages, t_new, t_pad, n_chunk, n_cmp_blocks,
                        n_sel_blocks):
    b = pl.program_id(0)
    nb = pl.num_programs(0)
    slot = b % 2
    past = n_pages * PAGE_SIZE
    chunks_per_page = PAGE_SIZE // CMP_STRIDE

    def copy(sl):
        return lambda page, p: pltpu.make_async_copy(pool_ref.at[page], pbuf.at[sl, p], sem.at[sl])

    @pl.when(b == 0)
    def _():
        _page_copies(pt_ref, 0, n_pages, copy(0))

    @pl.when(b + 1 < nb)
    def _():
        _page_copies(pt_ref, b + 1, n_pages, copy(1 - slot))

    _page_waits(n_pages, copy(slot))

    def relayout(p, _):
        t = _dot_nt(perm_ref[...], pbuf[slot, p].astype(BF16))
        r0 = pl.multiple_of(p * chunks_per_page, chunks_per_page)
        for j in range(CMP_PAIRS):
            for s in range(2):
                for h in range(2):
                    r = (2 * j + h) * chunks_per_page
                    xrows[s, j, pl.ds(r0, chunks_per_page), h * LANES:(h + 1) * LANES] = (
                        t[r:r + chunks_per_page, s * LANES:(s + 1) * LANES])
        return 0

    lax.fori_loop(0, n_pages, relayout, 0)

    base = n_pages * chunks_per_page
    cnew = cnew_ref[...]
    for s in range(2):
        for j in range(CMP_PAIRS):
            xrows[s, j, pl.ds(base, n_chunk - base), :] = jnp.zeros((n_chunk - base, 2 * LANES), F32)
        for p in range(t_new):
            xrows[s, p // 2, pl.ds(base, 1), (p % 2) * LANES:(p % 2 + 1) * LANES] = cnew[p:p + 1, s * LANES:(s + 1) * LANES]

    kvc = []
    for s in range(2):
        load = lambda j, s=s: xrows[s, j]
        kvc.append(_compress_slab(load, wc1_ref, pe_ref, b1_ref, w2_ref, s, n_chunk).astype(BF16))

    rows = NSA_HEADS * t_pad
    q_pos_r = past + (lax.broadcasted_iota(jnp.int32, (rows, 1), 0) & (t_pad - 1))
    q_pos_t = past + lax.broadcasted_iota(jnp.int32, (t_pad, 1), 0)
    o_cmp, imps = _cmp_branch(qn_ref[...], kvc[0][0:n_cmp_blocks], kvc[1][0:n_cmp_blocks], q_pos_r,
                              n_cmp_blocks, impmap_ref, t_pad)
    ocmp_ref[...] = o_cmp
    selm_ref[...] = _group_rows([_select_blocks(imp, q_pos_t, n_sel_blocks) for imp in imps], NSA_GROUP)


def _nsa_decode_cmp(page_table, qn_b, cnew_b, pool, wts, t_new, t_pad):
    nb, n_pages = page_table.shape
    past = n_pages * PAGE_SIZE
    total = past + t_new
    n_chunk_true = -(-total // CMP_STRIDE)
    nc = n_chunk_true - CMP_BLOCK // CMP_STRIDE + 1
    n_chunk = -(-n_chunk_true // 8) * 8
    assert nc % LANES == 0, "compressed-block count must be lane aligned"
    ns = past // SEL_BLOCK + (-(-t_new // SEL_BLOCK))
    sel_w = -(-ns // LANES) * LANES
    impmap = _importance_map(nc, nc, ns, sel_w)
    assert t_new <= CMP_STRIDE and past % PAGE_SIZE == 0
    tok = np.arange(PAGE_SIZE)
    perm = np.zeros((PAGE_SIZE, PAGE_SIZE), np.float32)
    perm[(tok % CMP_STRIDE) * (PAGE_SIZE // CMP_STRIDE) + tok // CMP_STRIDE, tok] = 1.0
    perm = jnp.asarray(perm, dtype=BF16)
    rows = NSA_HEADS * t_pad
    full = lambda a: pl.BlockSpec(a.shape, lambda b, pt: (0,) * a.ndim)
    per_b = lambda a: pl.BlockSpec((None,) + a.shape[1:], lambda b, pt: (b,) + (0,) * (a.ndim - 1))
    grid_spec = pltpu.PrefetchScalarGridSpec(
        num_scalar_prefetch=1,
        grid=(nb,),
        in_specs=[per_b(qn_b), per_b(cnew_b), pl.BlockSpec(memory_space=pl.ANY), full(perm),
                  full(wts['wc1']), full(wts['pe']), full(wts['b1']), full(wts['w2']), full(impmap)],
        out_specs=[pl.BlockSpec((None, rows, LANES), lambda b, pt: (b, 0, 0)),
                   pl.BlockSpec((None, rows, sel_w), lambda b, pt: (b, 0, 0))],
        scratch_shapes=[pltpu.VMEM((2, n_pages, NSA_KV_WIDTH, PAGE_SIZE), F32),
                        pltpu.VMEM((2, CMP_PAIRS, n_chunk, 2 * LANES), F32),
                        pltpu.SemaphoreType.DMA((2,))],
    )
    return pl.pallas_call(
        functools.partial(_nsa_dec_cmp_kernel, n_pages=n_pages, t_new=t_new, t_pad=t_pad, n_chunk=n_chunk,
                          n_cmp_blocks=nc, n_sel_blocks=ns),
        grid_spec=grid_spec,
        out_shape=[jax.ShapeDtypeStruct((nb, rows, LANES), F32),
                   jax.ShapeDtypeStruct((nb, rows, sel_w), F32)],
        compiler_params=_cparams(("arbitrary",)),
        name="nsa_decode_compress",
    )(page_table, qn_b, cnew_b, pool, perm, wts['wc1'], wts['pe'], wts['b1'], wts['w2'], impmap)


def _nsa_dec_sel_kernel(pt_ref, qr_ref, snew_ref, wnew_ref, selm_ref, ocmp_ref, gates_ref, wst_ref, pool_ref,
                        o_ref, kbuf, sem, *, n_pages, t_new, t_pad, tk):
    b = pl.program_id(0)
    nb = pl.num_programs(0)
    slot = b % 2
    past = n_pages * PAGE_SIZE
    rows = NSA_HEADS * t_pad

    def copy(sl):
        return lambda page, p: pltpu.make_async_copy(
            pool_ref.at[page], kbuf.at[sl, :, pl.ds(pl.multiple_of(p * PAGE_SIZE, PAGE_SIZE), PAGE_SIZE)], sem.at[sl])

    @pl.when(b == 0)
    def _():
        _page_copies(pt_ref, 0, n_pages, copy(0))

    @pl.when(b + 1 < nb)
    def _():
        _page_copies(pt_ref, b + 1, n_pages, copy(1 - slot))

    _page_waits(n_pages, copy(slot))

    blk_per_chunk = tk // SEL_BLOCK
    lane_lo = lax.broadcasted_iota(jnp.int32, (rows, LANES), 1) < SEL_BLOCK
    n_new = snew_ref.shape[0]
    t_q = lax.broadcasted_iota(jnp.int32, (rows, n_new), 0) & (t_pad - 1)
    t_k = lax.broadcasted_iota(jnp.int32, (rows, n_new), 1)
    new_causal = (t_k <= t_q) & (t_k < t_new)
    w_buf = wst_ref.shape[1]
    t_qw = lax.broadcasted_iota(jnp.int32, (rows, w_buf), 0) & (t_pad - 1)
    i_w = lax.broadcasted_iota(jnp.int32, (rows, w_buf), 1)
    win_mask = (t_qw + w_buf - i_w) < WINDOW
    snew = snew_ref[...]
    wnew = wnew_ref[...]
    qr = qr_ref[...]
    selm = selm_ref[...]

    def fresh():
        return jnp.full((rows, 1), NEG_INF, F32), jnp.zeros((rows, 1), F32), jnp.zeros((rows, LANES), F32)

    m, l, acc = fresh()
    for c in range(past // tk):
        kt = kbuf[slot, 0:LANES, pl.ds(c * tk, tk)].astype(BF16)
        vt = kbuf[slot, LANES:2 * LANES, pl.ds(c * tk, tk)].astype(BF16)
        pieces = []
        for i in range(tk // LANES):
            jb = c * blk_per_chunk + 2 * i
            pieces.append(jnp.where(lane_lo, selm[:, jb:jb + 1], selm[:, jb + 1:jb + 2]))
        mask = jnp.concatenate(pieces, axis=-1) > 0.5
        m, l, acc = _online_update(_dot(qr, kt), mask, vt, m, l, acc, v_transposed=True)
    nb_past = past // SEL_BLOCK
    mask = (selm[:, nb_past:nb_past + 1] > 0.5) & new_causal
    m, l, acc = _online_update(_dot_nt(qr, snew[:, 0:LANES].astype(BF16)), mask,
                               snew[:, LANES:2 * LANES].astype(BF16), m, l, acc)
    o_sel = _finalize(l, acc)

    m, l, acc = fresh()
    m, l, acc = _online_update(_dot(qr, wst_ref[0:LANES, :].astype(BF16)), win_mask,
                               wst_ref[LANES:2 * LANES, :].astype(BF16), m, l, acc, v_transposed=True)
    m, l, acc = _online_update(_dot_nt(qr, wnew[:, 0:LANES].astype(BF16)), new_causal,
                               wnew[:, LANES:2 * LANES].astype(BF16), m, l, acc)
    o_win = _finalize(l, acc)

    o_ref[...] = _mix_heads(gates_ref[...], ocmp_ref[...], o_sel, o_win, t_pad)


def _nsa_decode_sel(page_table, qr_b, snew_b, wnew_b, selm, ocmp, gates_b, win_state, pool, t_new, t_pad):
    nb, n_pages = page_table.shape
    past = n_pages * PAGE_SIZE
    tk = 1024
    assert past % tk == 0
    per_b = lambda a: pl.BlockSpec((None,) + a.shape[1:], lambda b, pt: (b,) + (0,) * (a.ndim - 1))
    grid_spec = pltpu.PrefetchScalarGridSpec(
        num_scalar_prefetch=1,
        grid=(nb,),
        in_specs=[per_b(qr_b), per_b(snew_b), per_b(wnew_b), per_b(selm), per_b(ocmp), per_b(gates_b),
                  per_b(win_state), pl.BlockSpec(memory_space=pl.ANY)],
        out_specs=pl.BlockSpec((None, t_pad, NSA_WIDTH), lambda b, pt: (b, 0, 0)),
        scratch_shapes=[pltpu.VMEM((2, NSA_KV_WIDTH, past), F32), pltpu.SemaphoreType.DMA((2,))],
    )
    return pl.pallas_call(
        functools.partial(_nsa_dec_sel_kernel, n_pages=n_pages, t_new=t_new, t_pad=t_pad, tk=tk),
        grid_spec=grid_spec,
        out_shape=jax.ShapeDtypeStruct((nb, t_pad, NSA_WIDTH), F32),
        compiler_params=_cparams(("arbitrary",)),
        name="nsa_decode_select_window",
    )(page_table, qr_b, snew_b, wnew_b, selm, ocmp, gates_b, win_state, pool)


def _prep_weights(w_in, q_norm_g, w_uq, kv_norm_g, w_uk, w_uv, cmp_pos_emb, cmp_w1, cmp_b1, cmp_w2, w_o, ln_g, ln_b):
    cuts = np.cumsum([MLA_Q_LORA, MLA_KV_LORA, MLA_ROPE_DIM, MLA_WIDTH, NSA_WIDTH, NSA_KV_WIDTH, NSA_KV_WIDTH,
                      NSA_KV_WIDTH, 3 * NSA_HEADS])[:].tolist()
    c_q, c_kv, k_r, z_mla, q_n, cmp_kv, sel_kv, win_kv, g_br, z_nsa = jnp.split(w_in, cuts, axis=1)
    pad = jnp.zeros((D_MODEL, LANES - MLA_ROPE_DIM - 3 * NSA_HEADS), w_in.dtype)
    w_in_p = jnp.concatenate([c_q, c_kv, z_mla, q_n, cmp_kv, sel_kv, win_kv, z_nsa, k_r, g_br, pad], axis=1)
    assert w_in_p.shape[1] == IN_WIDTH_P
    uq = w_uq.reshape(MLA_Q_LORA, MLA_HEADS, MLA_NOPE_DIM + MLA_ROPE_DIM)
    w_uq_p = jnp.concatenate([uq[:, :, :MLA_NOPE_DIM].reshape(MLA_Q_LORA, -1),
                              uq[:, :, MLA_NOPE_DIM:].reshape(MLA_Q_LORA, -1)], axis=1)
    w_uk_t = jnp.transpose(w_uk.reshape(MLA_KV_LORA, MLA_HEADS, MLA_NOPE_DIM), (1, 2, 0))
    w_uv_h = jnp.transpose(w_uv.reshape(MLA_KV_LORA, MLA_HEADS, MLA_V_DIM), (1, 0, 2))
    ratio = CMP_BLOCK // CMP_STRIDE
    eye = jnp.eye(NSA_KV_HEADS, dtype=w_in.dtype)
    w1r = cmp_w1.reshape(2, ratio, CMP_STRIDE, NSA_HEAD_DIM, NSA_HEAD_DIM)
    wc1 = jnp.einsum('gh,srpde->spgdrhe', eye, w1r).reshape(2, CMP_PAIRS, 2 * LANES, ratio * LANES)
    w2 = jnp.einsum('gh,sde->sgdhe', eye, cmp_w2).reshape(2, LANES, LANES)
    pe = jnp.transpose(cmp_pos_emb, (1, 0, 2))
    pe = jnp.concatenate([pe, pe], axis=-1).reshape(2, ratio, CMP_PAIRS, 2 * LANES)
    pe = jnp.transpose(pe, (0, 2, 1, 3))
    pe = jnp.pad(pe, ((0, 0), (0, 0), (0, 8 - ratio), (0, 0)))
    b1 = jnp.concatenate([cmp_b1, cmp_b1], axis=-1).reshape(2, 1, LANES)
    return dict(w_in=w_in_p.astype(BF16), q_norm_g=q_norm_g.reshape(1, -1), w_uq=w_uq_p.astype(BF16),
                w_uk=w_uk_t.astype(BF16), kv_norm_g=kv_norm_g.reshape(1, -1), w_uv=w_uv_h.astype(BF16),
                wc1=wc1.astype(BF16), w2=w2.astype(BF16), pe=pe, b1=b1,
                w_o=w_o.astype(BF16), ln_g=ln_g.reshape(1, -1), ln_b=ln_b.reshape(1, -1))


def _prompt_layer(x, wts):
    batch, seq, _ = x.shape
    x2d = x.reshape(batch * seq, D_MODEL)
    tm = min(256, seq)
    P = _project(x2d, _rope_tables(jnp.arange(seq, dtype=jnp.int32)), wts, tm)
    o_mla = _mla_prompt(P['qmla'], P['mrowb'], wts['w_uv'], batch, seq)
    kvc = _compress_prompt(P['cmps'], wts, batch, seq)
    o_nsa = _nsa_prompt(P, kvc, batch, seq)
    y = _finish(x2d, o_mla, o_nsa, P['szm'], P['szn'], wts, tm)
    kvd = (2, NSA_KV_HEADS, NSA_HEAD_DIM)
    w_keep = min(WINDOW, seq)
    return (y.reshape(batch, seq, D_MODEL),
            P['mrow'].reshape(batch, seq, MLA_ROW),
            P['cmp'].reshape((batch, seq) + kvd),
            P['sel'].reshape((batch, seq) + kvd),
            P['win'].reshape((batch, seq) + kvd)[:, seq - w_keep:])


def _sample_layer(x, cache_mla, cache_cmp, cache_sel, win_state, page_table, wts):
    nb, t_new, _ = x.shape
    n_pages = page_table.shape[1]
    past = n_pages * PAGE_SIZE
    n = nb * t_new
    t_pad = 8
    x2d = x.reshape(n, D_MODEL)
    pos = past + (jnp.arange(n, dtype=jnp.int32) % t_new)
    P = _project(x2d, _rope_tables(pos), wts, n)
    kvd = (2, NSA_KV_HEADS, NSA_HEAD_DIM)
    n_pool = cache_mla.shape[0]

    q_b = jnp.transpose(P['qmla'][:, :, :MLA_ROW].reshape(MLA_HEADS, nb, t_new, MLA_ROW), (1, 0, 2, 3))
    q_b = q_b.reshape(nb, -1, MLA_ROW)
    knew_b = jnp.pad(P['mrowb'][:, :MLA_ROW].reshape(nb, t_new, MLA_ROW), ((0, 0), (0, 16 - t_new), (0, 0)))
    feat_major = lambda a: jnp.swapaxes(a.reshape(a.shape[0], a.shape[1], -1), 1, 2)
    o_lat = _mla_decode(page_table, q_b, knew_b, feat_major(cache_mla))
    o_heads = jnp.transpose(o_lat.reshape(nb, MLA_HEADS, t_new, MLA_KV_LORA), (1, 0, 2, 3)).reshape(MLA_HEADS, n, -1)
    o_mla = _uv(o_heads, wts['w_uv'])

    def q_rows(q):
        q = jnp.transpose(q.reshape(NSA_HEADS, nb, t_new, LANES), (1, 0, 2, 3))
        q = jnp.pad(q, ((0, 0), (0, 0), (0, t_pad - t_new), (0, 0)))
        return q.reshape(nb, NSA_HEADS * t_pad, LANES)

    cnew_b = P['cmp'].reshape(nb, t_new, NSA_KV_WIDTH)
    ocmp, selm = _nsa_decode_cmp(page_table, q_rows(P['qn']), cnew_b, feat_major(cache_cmp), wts, t_new, t_pad)
    pad_rows = lambda a, r: jnp.pad(a.reshape(nb, t_new, -1), ((0, 0), (0, r - t_new), (0, 0)))
    o_nsa = _nsa_decode_sel(page_table, q_rows(P['qrot']), pad_rows(P['sel'], 16), pad_rows(P['win'], 16), selm,
                            ocmp, pad_rows(P['gates'], t_pad), feat_major(win_state), feat_major(cache_sel),
                            t_new, t_pad)
    o_nsa = o_nsa[:, :t_new].reshape(n, NSA_WIDTH)

    y = _finish(x2d, o_mla, o_nsa, P['szm'], P['szn'], wts, n)
    wst = win_state.reshape(nb, -1, NSA_KV_WIDTH)
    win_all = jnp.concatenate([wst, P['win'].reshape(nb, t_new, NSA_KV_WIDTH)], axis=1)[:, t_new:]
    return (y.reshape(nb, t_new, D_MODEL),
            P['mrow'].reshape(nb, t_new, MLA_ROW),
            P['cmp'].reshape((nb, t_new) + kvd),
            P['sel'].reshape((nb, t_new) + kvd),
            win_all.reshape((nb, win_all.shape[1]) + kvd))


def kernel(x_prompt, x_sample, cache_mla, cache_cmp_kv, cache_sel_kv, state_win_kv, page_table, w_in, q_norm_g, w_uq,
           kv_norm_g, w_uk, w_uv, cmp_pos_emb, cmp_w1, cmp_b1, cmp_w2, w_o, ln_g, ln_b):
    assert w_in.shape[0] == DEPTH
    wts = _prep_weights(w_in[0], q_norm_g[0], w_uq[0], kv_norm_g[0], w_uk[0], w_uv[0], cmp_pos_emb[0], cmp_w1[0],
                        cmp_b1[0], cmp_w2[0], w_o[0], ln_g[0], ln_b[0])
    yp, p_mla, p_cmp, p_sel, p_win = _prompt_layer(x_prompt, wts)
    drop_depth = lambda a: a.reshape(a.shape[1:])
    ys, s_mla, s_cmp, s_sel, s_win = _sample_layer(x_sample, drop_depth(cache_mla), drop_depth(cache_cmp_kv),
                                                   drop_depth(cache_sel_kv), drop_depth(state_win_kv), page_table, wts)
    add_depth = lambda a: a[None]
    return (yp, ys) + tuple(add_depth(a) for a in (p_mla, p_cmp, p_sel, p_win, s_mla, s_cmp, s_sel, s_win))
```

```python
import functools

import numpy as np
import jax
import jax.numpy as jnp
from jax import lax
from jax.experimental import pallas as pl
from jax.experimental.pallas import tpu as pltpu

F32 = jnp.float32
BF16 = jnp.bfloat16

D_MODEL = 1024
PAGE_SIZE = 128
MLA_HEADS = 8
MLA_V_DIM = 64
MLA_NOPE_DIM = 64
MLA_ROPE_DIM = 32
MLA_Q_LORA = 256
MLA_KV_LORA = 128
MLA_WIDTH = MLA_HEADS * MLA_V_DIM
MLA_ROW = MLA_KV_LORA + MLA_ROPE_DIM
NSA_HEADS = 8
NSA_KV_HEADS = 2
NSA_HEAD_DIM = 64
NSA_GROUP = NSA_HEADS // NSA_KV_HEADS
NSA_WIDTH = NSA_HEADS * NSA_HEAD_DIM
NSA_KV_WIDTH = 2 * NSA_KV_HEADS * NSA_HEAD_DIM
CMP_BLOCK = 32
CMP_STRIDE = 16
SEL_BLOCK = 64
SEL_TOP_N = 16
WINDOW = 512
Q_BLOCK = 128
ROPE_THETA = 10000.0
RMS_EPS = 1e-6
LN_EPS = 1e-5
NEG_INF = -1e30
MLA_SCALE = (MLA_NOPE_DIM + MLA_ROPE_DIM) ** -0.5
NSA_SCALE = NSA_HEAD_DIM ** -0.5
DEPTH = 1
ALPHA = (2 * DEPTH) ** 0.25

LANES = 128
LOG2E = 1.4426950408889634
MLA_KROW = 2 * LANES
MLA_ONES_LANE = MLA_ROW
NSA_KROW = 3 * LANES

SEG_CQ = (0, 256)
SEG_CKV = (256, 128)
SEG_ZMLA = (384, 512)
SEG_QN = (896, 512)
SEG_CMP = (1408, 256)
SEG_SEL = (1664, 256)
SEG_WIN = (1920, 256)
SEG_ZNSA = (2176, 512)
SEG_MISC = (2688, 128)
IN_WIDTH_P = 2816
MISC_GATE0 = MLA_ROPE_DIM

VMEM_LIMIT = 48 * 1024 * 1024

def _cparams(sem, flags=None):
    return pltpu.CompilerParams(dimension_semantics=sem, vmem_limit_bytes=VMEM_LIMIT, flags=flags)


def _sigmoid(x):
    return 1.0 / (1.0 + jnp.exp(-x))


def _silu(x):
    return x * _sigmoid(x)


def _rope_slab(x, c, sa, sb, half):
    return x * c + pltpu.roll(x, LANES - half, 1) * sa + pltpu.roll(x, half, 1) * sb


def _dot_nt(a, b):
    return lax.dot_general(a, b, (((1,), (1,)), ((), ())), preferred_element_type=F32)


def _dot(a, b):
    return jnp.dot(a, b, preferred_element_type=F32)


def _masked_softmax(s, mask):
    sm = jnp.where(mask, s, NEG_INF)
    m = jnp.max(sm, axis=-1, keepdims=True)
    e = jnp.exp2(sm - m)
    p = e / jnp.sum(e, axis=-1, keepdims=True)
    return jnp.where(mask, p, 0.0)


def _online_update(s, mask, v, m, l, acc, v_transposed=False):
    sm = s if mask is None else jnp.where(mask, s, NEG_INF)
    m_new = jnp.maximum(m, jnp.max(sm, axis=-1, keepdims=True))
    alpha = jnp.exp2(m - m_new)
    p = jnp.exp2(sm - m_new)
    if mask is not None:
        p = jnp.where(mask, p, 0.0)
    l_new = alpha * l + jnp.sum(p, axis=-1, keepdims=True)
    pv = _dot_nt(p.astype(BF16), v) if v_transposed else _dot(p.astype(BF16), v)
    return m_new, l_new, alpha * acc + pv


def _finalize(l, acc):
    ok = l > 0.0
    return jnp.where(ok, acc / jnp.where(ok, l, 1.0), 0.0)


def _flash_step(s, bias, v_ext, m, acc):
    sm = s if bias is None else s + bias
    m_new = jnp.maximum(m, jnp.max(sm, axis=-1, keepdims=True))
    p = jnp.exp2(sm - m_new)
    return m_new, jnp.exp2(m - m_new) * acc + _dot(p.astype(BF16), v_ext)


def _flash_result(acc, ones_lane):
    return acc[:, 0:LANES] * (1.0 / acc[:, ones_lane:ones_lane + 1])


def _tile_rows(x, reps):
    return jnp.concatenate([x] * reps, axis=0)


def _select_blocks(imp, q_pos, n_sel_blocks):
    rows, width = imp.shape
    j = lax.broadcasted_iota(jnp.int32, (rows, width), 1)
    cur = lax.shift_right_logical(q_pos, 6)
    avail = (j * SEL_BLOCK <= q_pos) & (j < n_sel_blocks)
    forced = (j == 0) | (j == cur) | (j == cur - 1)
    val = jnp.where(avail, jnp.where(forced, jnp.inf, imp), -jnp.inf)
    rank = jnp.zeros((rows, width), F32)
    for jp in range(n_sel_blocks):
        col = val[:, jp:jp + 1]
        ahead = (col > val) | ((col == val) & (j > jp))
        rank = rank + jnp.where(ahead, 1.0, 0.0)
    keep = (rank < float(min(SEL_TOP_N, n_sel_blocks))) & avail
    return jnp.where(keep, 1.0, 0.0)


def _select_blocks_t(imp_t, q_pos, n_sel_blocks):
    n_rows, width = imp_t.shape
    j = lax.broadcasted_iota(jnp.int32, (n_rows, width), 0)
    cur = lax.shift_right_logical(q_pos, 6)
    avail = (j * SEL_BLOCK <= q_pos) & (j < n_sel_blocks)
    forced = (j == 0) | (j == cur) | (j == cur - 1)
    val = jnp.where(avail, jnp.where(forced, jnp.inf, imp_t), -jnp.inf)
    n_pieces = -(-n_sel_blocks // 8)
    pieces = [val[8 * v:8 * v + 8] for v in range(n_pieces)]
    jrow = lax.broadcasted_iota(jnp.int32, (8, width), 0)
    ranks = [jnp.zeros((8, width), F32) for _ in range(n_pieces)]
    for jp in range(n_sel_blocks):
        cand = jnp.broadcast_to(val[jp:jp + 1], (8, width))
        for v in range(n_pieces):
            if 8 * v > jp:
                ahead = cand >= pieces[v]
            elif 8 * v + 7 < jp:
                ahead = cand > pieces[v]
            else:
                ahead = (cand > pieces[v]) | ((cand == pieces[v]) & (jrow > jp - 8 * v))
            ranks[v] = ranks[v] + jnp.where(ahead, 1.0, 0.0)
    if n_rows > 8 * n_pieces:
        ranks.append(jnp.zeros((n_rows - 8 * n_pieces, width), F32))
    rank = jnp.concatenate(ranks, axis=0)
    keep = (rank < float(min(SEL_TOP_N, n_sel_blocks))) & avail
    return jnp.where(keep, 1.0, 0.0)


def _proj_kernel(x_ref, w_ref, qg_ref, wuq_ref, wuk_ref, kvg_ref, tab_ref,
                 qmla_ref, mrow_ref, mrowb_ref, szm_ref, qn_ref, qrot_ref, cmp_ref, cmps_ref,
                 sel_ref, selb_ref, win_ref, winb_ref, gates_ref, szn_ref):
    xb = x_ref[...].astype(BF16)

    def seg(s):
        return _dot(xb, w_ref[:, s[0]:s[0] + s[1]])

    cn, san, sbn = tab_ref[0], tab_ref[1], tab_ref[2]
    cm, sam, sbm = tab_ref[3], tab_ref[4], tab_ref[5]

    cq = seg(SEG_CQ)
    r = cq * lax.rsqrt(jnp.mean(cq * cq, axis=-1, keepdims=True) + RMS_EPS) * qg_ref[...]
    q = _dot(r.astype(BF16), wuq_ref[...])
    nope_w = MLA_HEADS * MLA_NOPE_DIM
    lane = lax.broadcasted_iota(jnp.int32, (q.shape[0], LANES), 1)
    for h in range(MLA_HEADS):
        ql = _dot(q[:, h * MLA_NOPE_DIM:(h + 1) * MLA_NOPE_DIM].astype(BF16), wuk_ref[h])
        qmla_ref[h, :, 0:LANES] = (ql * (MLA_SCALE * LOG2E)).astype(BF16)
    heads_per_slab = LANES // MLA_ROPE_DIM
    for jj in range(MLA_HEADS // heads_per_slab):
        qr = _rope_slab(q[:, nope_w + jj * LANES: nope_w + (jj + 1) * LANES], cm, sam, sbm, MLA_ROPE_DIM // 2)
        qr = qr * (MLA_SCALE * LOG2E)
        for hh in range(heads_per_slab):
            front = qr if hh == 0 else pltpu.roll(qr, LANES - hh * MLA_ROPE_DIM, 1)
            qmla_ref[jj * heads_per_slab + hh, :, LANES:2 * LANES] = (
                jnp.where(lane < MLA_ROPE_DIM, front, 0.0).astype(BF16))

    ckv = seg(SEG_CKV)
    lat = ckv * lax.rsqrt(jnp.mean(ckv * ckv, axis=-1, keepdims=True) + RMS_EPS) * kvg_ref[...]
    misc = seg(SEG_MISC)
    kr = _rope_slab(misc, cm, sam, sbm, MLA_ROPE_DIM // 2)
    mrow_ref[:, 0:MLA_KV_LORA] = lat
    mrow_ref[:, MLA_KV_LORA:MLA_ROW] = kr[:, 0:MLA_ROPE_DIM]
    mrowb_ref[:, 0:LANES] = lat.astype(BF16)
    ones_col = jnp.where(lane == MLA_ONES_LANE - LANES, 1.0, 0.0)
    mrowb_ref[:, LANES:2 * LANES] = jnp.where(lane < MLA_ROPE_DIM, kr, ones_col).astype(BF16)
    gates_ref[...] = _sigmoid(misc)

    szm_ref[...] = _silu(seg(SEG_ZMLA))
    szn_ref[...] = _silu(seg(SEG_ZNSA))

    qn = seg(SEG_QN)
    for jj in range(NSA_WIDTH // LANES):
        raw = qn[:, jj * LANES:(jj + 1) * LANES] * (NSA_SCALE * LOG2E)
        rot = _rope_slab(qn[:, jj * LANES:(jj + 1) * LANES], cn, san, sbn, NSA_HEAD_DIM // 2) * (NSA_SCALE * LOG2E)
        for src, dst in ((raw, qn_ref), (rot, qrot_ref)):
            swapped = pltpu.roll(src, NSA_HEAD_DIM, 1)
            for half in range(2):
                hd = 2 * jj + half
                g = hd // NSA_GROUP
                keep = (lane >= g * NSA_HEAD_DIM) & (lane < (g + 1) * NSA_HEAD_DIM)
                dst[hd] = jnp.where(keep, src if half == g else swapped, 0.0).astype(BF16)

    cmpv = seg(SEG_CMP)
    cmp_ref[...] = cmpv
    cmps_ref[0] = cmpv[:, 0:LANES]
    cmps_ref[1] = cmpv[:, LANES:2 * LANES]

    for src, dst, dstb in ((SEG_SEL, sel_ref, selb_ref), (SEG_WIN, win_ref, winb_ref)):
        kv = seg(src)
        k = _rope_slab(kv[:, 0:LANES], cn, san, sbn, NSA_HEAD_DIM // 2)
        v = kv[:, LANES:2 * LANES]
        dst[:, 0:LANES] = k
        dst[:, LANES:2 * LANES] = v
        dstb[:, 0:LANES] = k.astype(BF16)
        dstb[:, LANES:2 * LANES] = v.astype(BF16)
        dstb[:, 2 * LANES:3 * LANES] = jnp.where(lane == 0, 1.0, 0.0).astype(BF16)


def _rope_tables(pos):
    def tab(d):
        inv = 1.0 / (ROPE_THETA ** (jnp.arange(0, d, 2, dtype=F32) / d))
        ang = pos.astype(F32)[:, None] * inv[None, :]
        cos, sin = jnp.cos(ang), jnp.sin(ang)
        zero = jnp.zeros_like(sin)
        reps = LANES // d
        return (jnp.tile(jnp.concatenate([cos, cos], -1), (1, reps)),
                jnp.tile(jnp.concatenate([-sin, zero], -1), (1, reps)),
                jnp.tile(jnp.concatenate([zero, sin], -1), (1, reps)))
    return jnp.stack(tab(NSA_HEAD_DIM) + tab(MLA_ROPE_DIM))


def _project(x2d, tabs, wts, tm):
    n = x2d.shape[0]
    p_rows = tabs.shape[1]
    assert n % tm == 0 and p_rows % tm == 0
    nt = p_rows // tm
    row = lambda w: pl.BlockSpec((tm, w), lambda i: (i, 0))
    full = lambda a: pl.BlockSpec(a.shape, lambda i: (0,) * a.ndim)
    out_shapes = dict(
        qmla=((MLA_HEADS, n, MLA_KROW), BF16, pl.BlockSpec((MLA_HEADS, tm, MLA_KROW), lambda i: (0, i, 0))),
        mrow=((n, MLA_ROW), F32, row(MLA_ROW)),
        mrowb=((n, MLA_KROW), BF16, row(MLA_KROW)),
        szm=((n, MLA_WIDTH), F32, row(MLA_WIDTH)),
        qn=((NSA_HEADS, n, LANES), BF16, pl.BlockSpec((NSA_HEADS, tm, LANES), lambda i: (0, i, 0))),
        qrot=((NSA_HEADS, n, LANES), BF16, pl.BlockSpec((NSA_HEADS, tm, LANES), lambda i: (0, i, 0))),
        cmp=((n, NSA_KV_WIDTH), F32, row(NSA_KV_WIDTH)),
        cmps=((2, n, LANES), F32, pl.BlockSpec((2, tm, LANES), lambda i: (0, i, 0))),
        sel=((n, NSA_KV_WIDTH), F32, row(NSA_KV_WIDTH)),
        selb=((n, NSA_KROW), BF16, row(NSA_KROW)),
        win=((n, NSA_KV_WIDTH), F32, row(NSA_KV_WIDTH)),
        winb=((n, NSA_KROW), BF16, row(NSA_KROW)),
        gates=((n, LANES), F32, row(LANES)),
        szn=((n, NSA_WIDTH), F32, row(NSA_WIDTH)),
    )
    names = list(out_shapes)
    outs = pl.pallas_call(
        _proj_kernel,
        grid=(n // tm,),
        in_specs=[row(D_MODEL), full(wts['w_in']), full(wts['q_norm_g']), full(wts['w_uq']), full(wts['w_uk']),
                  full(wts['kv_norm_g']), pl.BlockSpec((6, tm, LANES), lambda i: (0, i % nt, 0))],
        out_specs=[out_shapes[k][2] for k in names],
        out_shape=[jax.ShapeDtypeStruct(out_shapes[k][0], out_shapes[k][1]) for k in names],
        compiler_params=_cparams(("parallel",)),
        name="projection",
    )(x2d, wts['w_in'], wts['q_norm_g'], wts['w_uq'], wts['w_uk'], wts['kv_norm_g'], tabs)
    return dict(zip(names, outs))


def _uv_project(o, wuv_ref, rows):
    return jnp.concatenate(
        [_dot(o[h * rows:(h + 1) * rows].astype(BF16), wuv_ref[h]) for h in range(MLA_HEADS)], axis=-1)


def _mla_prompt_kernel(q_ref, k_ref, wuv_ref, o_ref, *, tq, tk):
    qi = pl.program_id(1)
    rows = MLA_HEADS * tq
    q = q_ref[...].reshape(rows, MLA_KROW)
    n_full = (qi * tq) // tk

    def chunk(c):
        return k_ref[pl.ds(pl.multiple_of(c * tk, tk), tk), :]

    def full_body(c, carry):
        kc = chunk(c)
        return _flash_step(_dot_nt(q, kc), None, kc, *carry)

    init = (jnp.full((rows, 1), NEG_INF, F32), jnp.zeros((rows, MLA_KROW), F32))
    carry = lax.fori_loop(0, n_full, full_body, init)
    kc = chunk(n_full)
    k_pos = n_full * tk + lax.broadcasted_iota(jnp.int32, (tq, tk), 1)
    q_pos = qi * tq + lax.broadcasted_iota(jnp.int32, (tq, tk), 0)
    bias = _tile_rows(jnp.where(k_pos <= q_pos, 0.0, NEG_INF), MLA_HEADS)
    _, acc = _flash_step(_dot_nt(q, kc), bias, kc, *carry)
    o_ref[...] = _uv_project(_flash_result(acc, MLA_ONES_LANE), wuv_ref, tq)


def _mla_prompt(qmla, mrowb, wuv, batch, seq):
    tq = min(256, seq)
    tk = min(512, seq)
    nq = seq // tq
    assert tk % tq == 0 and seq % tk == 0
    return pl.pallas_call(
        functools.partial(_mla_prompt_kernel, tq=tq, tk=tk),
        grid=(batch, nq),
        in_specs=[pl.BlockSpec((MLA_HEADS, tq, MLA_KROW), lambda b, i: (0, b * nq + i, 0)),
                  pl.BlockSpec((None, seq, MLA_KROW), lambda b, i: (b, 0, 0)),
                  pl.BlockSpec(wuv.shape, lambda b, i: (0, 0, 0))],
        out_specs=pl.BlockSpec((tq, MLA_WIDTH), lambda b, i: (b * nq + i, 0)),
        out_shape=jax.ShapeDtypeStruct((batch * seq, MLA_WIDTH), F32),
        compiler_params=_cparams(("parallel", "parallel")),
        name="mla_prompt",
    )(qmla, mrowb.reshape(batch, seq, MLA_KROW), wuv)


CMP_PAIRS = CMP_STRIDE // 2
CHUNK_PITCH = 24


def _compress_slab(load_pair, wc1_ref, pe_ref, b1_ref, w2_ref, s, n_chunk):
    proj = jnp.zeros((n_chunk, 2 * LANES), F32)
    pe = jnp.zeros((8, 2 * LANES), F32)
    for j in range(CMP_PAIRS):
        w = wc1_ref[s, j]
        proj = proj + _dot(load_pair(j).astype(BF16), w)
        pe = pe + _dot(pe_ref[s, j].astype(BF16), w)
    bias = b1_ref[s] + pe[0:1, 0:LANES] + pe[1:2, LANES:2 * LANES]
    hid = proj[:, 0:LANES] + pltpu.roll(proj[:, LANES:2 * LANES], n_chunk - 1, 0) + bias
    return _dot(_silu(hid).astype(BF16), w2_ref[s])


def _compress_prompt_kernel(x_ref, wc1_ref, pe_ref, b1_ref, w2_ref, o_ref, *, n_chunk):
    for s in range(2):
        row = lambda p, s=s: x_ref[s, pl.ds(p, n_chunk, stride=CMP_STRIDE), :]
        load = lambda j, row=row: jnp.concatenate([row(2 * j), row(2 * j + 1)], axis=-1)
        o_ref[s] = _compress_slab(load, wc1_ref, pe_ref, b1_ref, w2_ref, s, n_chunk).astype(BF16)


def _compress_prompt(cmps, wts, batch, seq):
    n_chunk = seq // CMP_STRIDE
    full = lambda a: pl.BlockSpec(a.shape, lambda b: (0,) * a.ndim)
    return pl.pallas_call(
        functools.partial(_compress_prompt_kernel, n_chunk=n_chunk),
        grid=(batch,),
        in_specs=[pl.BlockSpec((2, seq, LANES), lambda b: (0, b, 0)),
                  full(wts['wc1']), full(wts['pe']), full(wts['b1']), full(wts['w2'])],
        out_specs=pl.BlockSpec((None, 2, n_chunk, LANES), lambda b: (b, 0, 0, 0)),
        out_shape=jax.ShapeDtypeStruct((batch, 2, n_chunk, LANES), BF16),
        compiler_params=_cparams(("parallel",)),
        name="compress_prompt",
    )(cmps, wts['wc1'], wts['pe'], wts['b1'], wts['w2'])


def _cmp_branch(q, kc, vc, q_pos_rows, n_cmp_blocks, impmap_ref, rows_t, imp_transposed=False):
    s = _dot_nt(q, kc)
    n = lax.broadcasted_iota(jnp.int32, (rows_t, s.shape[1]), 1)
    visible = (n * CMP_STRIDE + (CMP_BLOCK - 1) <= q_pos_rows[0:rows_t]) & (n < n_cmp_blocks)
    sm = s + _tile_rows(jnp.where(visible, 0.0, NEG_INF), NSA_HEADS)
    e = jnp.exp2(sm - jnp.max(sm, axis=-1, keepdims=True))
    any_visible = (q_pos_rows >= CMP_BLOCK - 1) & (n_cmp_blocks > 0)
    p = e * jnp.where(any_visible, 1.0 / jnp.sum(e, axis=-1, keepdims=True), 0.0)
    o_cmp = _dot(p.astype(BF16), vc)
    imps = []
    for g in range(NSA_KV_HEADS):
        base = g * NSA_GROUP * rows_t
        psum = p[base:base + rows_t]
        for hh in range(1, NSA_GROUP):
            psum = psum + p[base + hh * rows_t: base + (hh + 1) * rows_t]
        hi = psum.astype(BF16)
        lo = (psum - hi.astype(F32)).astype(BF16)
        if imp_transposed:
            imps.append(_dot_nt(impmap_ref[...], hi) + _dot_nt(impmap_ref[...], lo))
        else:
            imps.append(_dot(hi, impmap_ref[...]) + _dot(lo, impmap_ref[...]))
    return o_cmp, imps


def _mix_heads(gates, o_cmp, o_sel, o_win, rows_t):
    outs = []
    for hd in range(NSA_HEADS):
        g = hd // NSA_GROUP
        rs = slice(hd * rows_t, (hd + 1) * rows_t)
        ls = slice(g * NSA_HEAD_DIM, (g + 1) * NSA_HEAD_DIM)
        gcol = lambda br: gates[:, MISC_GATE0 + br * NSA_HEADS + hd: MISC_GATE0 + br * NSA_HEADS + hd + 1]
        outs.append(gcol(0) * o_cmp[rs, ls] + gcol(1) * o_sel[rs, ls] + gcol(2) * o_win[rs, ls])
    return jnp.concatenate(outs, axis=-1)


def _group_rows(per_group, reps):
    return jnp.concatenate([per_group[g] for g in range(NSA_KV_HEADS) for _ in range(reps)], axis=0)


def _nsa_prompt_kernel(qn_ref, qrot_ref, kvc_ref, sel_ref, win_ref, gates_ref, impmap_ref, o_ref,
                       *, tq, tk, n_cmp_blocks, n_sel_blocks, wlen):
    qi = pl.program_id(1)
    rows = NSA_HEADS * tq
    q_pos_t = qi * tq + lax.broadcasted_iota(jnp.int32, (tq, 1), 0)
    q_pos_r = qi * tq + (lax.broadcasted_iota(jnp.int32, (rows, 1), 0) & (tq - 1))
    qn = qn_ref[...].reshape(rows, LANES)
    qrot = qrot_ref[...].reshape(rows, LANES)

    o_cmp, imps_t = _cmp_branch(qn, kvc_ref[0], kvc_ref[1], q_pos_r, n_cmp_blocks, impmap_ref, tq,
                                imp_transposed=True)
    q_pos_lane = qi * tq + lax.broadcasted_iota(jnp.int32, (1, tq), 1)
    selms = [_select_blocks_t(imp_t, q_pos_lane, n_sel_blocks).T.astype(BF16) for imp_t in imps_t]

    blk_per_chunk = tk // SEL_BLOCK
    n_chunks = (qi * tq + tq + tk - 1) // tk
    e_row = lax.broadcasted_iota(jnp.int32, (LANES, tk), 0)
    e_col = lax.shift_right_logical(lax.broadcasted_iota(jnp.int32, (LANES, tk), 1), 6)
    col_t = lax.broadcasted_iota(jnp.int32, (tq, tk), 1)

    def sel_body(c, carry):
        start = pl.multiple_of(c * tk, tk)
        k = sel_ref[pl.ds(start, tk), 0:LANES]
        v_ext = sel_ref[pl.ds(start, tk), LANES:3 * LANES]
        expand = jnp.where(e_row == c * blk_per_chunk + e_col, 1.0, 0.0).astype(BF16)
        causal = (c * tk + col_t) <= q_pos_t
        bias = _group_rows([jnp.where((_dot(sm, expand) > 0.5) & causal, 0.0, NEG_INF) for sm in selms], NSA_GROUP)
        return _flash_step(_dot_nt(qrot, k), bias, v_ext, *carry)

    init = (jnp.full((rows, 1), NEG_INF, F32), jnp.zeros((rows, 2 * LANES), F32))
    _, acc = lax.fori_loop(0, n_chunks, sel_body, init)
    o_sel = _flash_result(acc, LANES)

    w_start = pl.multiple_of(jnp.maximum(qi * tq + tq - wlen, 0), tq)
    k = win_ref[pl.ds(w_start, wlen), 0:LANES]
    v_ext = win_ref[pl.ds(w_start, wlen), LANES:3 * LANES]
    d = q_pos_t - (w_start + lax.broadcasted_iota(jnp.int32, (tq, wlen), 1))
    bias = _tile_rows(jnp.where((d >= 0) & (d < WINDOW), 0.0, NEG_INF), NSA_HEADS)
    _, acc = _flash_step(_dot_nt(qrot, k), bias, v_ext, *init)
    o_win = _flash_result(acc, LANES)

    o_ref[...] = _mix_heads(gates_ref[...], o_cmp, o_sel, o_win, tq)


def _importance_map(nc_rows, nc, ns, width):
    i = np.arange(nc_rows)[:, None]
    j = np.arange(width)[None, :]
    lo = np.maximum(i * CMP_STRIDE, j * SEL_BLOCK)
    hi = np.minimum(i * CMP_STRIDE + CMP_BLOCK, (j + 1) * SEL_BLOCK)
    m = np.maximum(hi - lo, 0).astype(np.float32) / CMP_BLOCK
    m = m * (i < nc) * (j < ns)
    return jnp.asarray(m, dtype=BF16)


def _nsa_prompt(P, kvc, batch, seq):
    tq = min(256, seq)
    tk = min(512, seq)
    nq = seq // tq
    n_chunk = seq // CMP_STRIDE
    nc = n_chunk - CMP_BLOCK // CMP_STRIDE + 1
    ns = seq // SEL_BLOCK
    assert ns <= LANES and seq % tk == 0 and seq % CMP_STRIDE == 0 and seq % SEL_BLOCK == 0
    wlen = min(WINDOW + tq, seq)
    impmap = _importance_map(n_chunk, nc, ns, LANES).T
    blk = lambda w: pl.BlockSpec((tq, w), lambda b, i: (b * nq + i, 0))
    qblk = pl.BlockSpec((NSA_HEADS, tq, LANES), lambda b, i: (0, b * nq + i, 0))
    per_b = lambda a: pl.BlockSpec((None,) + a.shape[1:], lambda b, i: (b,) + (0,) * (a.ndim - 1))
    selb = P['selb'].reshape(batch, seq, NSA_KROW)
    winb = P['winb'].reshape(batch, seq, NSA_KROW)
    return pl.pallas_call(
        functools.partial(_nsa_prompt_kernel, tq=tq, tk=tk, n_cmp_blocks=nc, n_sel_blocks=ns, wlen=wlen),
        grid=(batch, nq),
        in_specs=[qblk, qblk, per_b(kvc), per_b(selb), per_b(winb), blk(LANES),
                  pl.BlockSpec(impmap.shape, lambda b, i: (0, 0))],
        out_specs=blk(NSA_WIDTH),
        out_shape=jax.ShapeDtypeStruct((batch * seq, NSA_WIDTH), F32),
        compiler_params=_cparams(("parallel", "parallel")),
        name="nsa_prompt",
    )(P['qn'], P['qrot'], kvc, selb, winb, P['gates'], impmap)


def _finish_kernel(x_ref, omla_ref, onsa_ref, szm_ref, szn_ref, wo_ref, g_ref, b_ref, y_ref):
    mixed = jnp.concatenate([omla_ref[...] * szm_ref[...], onsa_ref[...] * szn_ref[...]], axis=-1)
    h = ALPHA * x_ref[...] + _dot(mixed.astype(BF16), wo_ref[...])
    mu = jnp.mean(h, axis=-1, keepdims=True)
    d = h - mu
    var = jnp.mean(d * d, axis=-1, keepdims=True)
    y_ref[...] = d * lax.rsqrt(var + LN_EPS) * g_ref[...] + b_ref[...]


def _finish(x2d, omla, onsa, szm, szn, wts, tm):
    n = x2d.shape[0]
    row = lambda w: pl.BlockSpec((tm, w), lambda i: (i, 0))
    full = lambda a: pl.BlockSpec(a.shape, lambda i: (0,) * a.ndim)
    return pl.pallas_call(
        _finish_kernel,
        grid=(n // tm,),
        in_specs=[row(D_MODEL), row(MLA_WIDTH), row(NSA_WIDTH), row(MLA_WIDTH), row(NSA_WIDTH),
                  full(wts['w_o']), full(wts['ln_g']), full(wts['ln_b'])],
        out_specs=row(D_MODEL),
        out_shape=jax.ShapeDtypeStruct((n, D_MODEL), F32),
        compiler_params=_cparams(("parallel",)),
        name="finish",
    )(x2d, omla, onsa, szm, szn, wts['w_o'], wts['ln_g'], wts['ln_b'])


def _page_copies(pt_ref, b, n_pages, make_copy):
    def start(p, _):
        make_copy(pt_ref[b, p], p).start()
        return 0
    lax.fori_loop(0, n_pages, start, 0)


def _page_waits(n_pages, make_copy):
    def wait(p, _):
        make_copy(0, p).wait()
        return 0
    lax.fori_loop(0, n_pages, wait, 0)


def _mla_decode_kernel(pt_ref, q_ref, knew_ref, pool_ref, o_ref, kbuf, sem, *, n_pages, t_new, tk):
    b = pl.program_id(0)
    nb = pl.num_programs(0)
    slot = b % 2
    past = n_pages * PAGE_SIZE

    def copy(sl):
        return lambda page, p: pltpu.make_async_copy(pool_ref.at[page], kbuf.at[sl, p], sem.at[sl])

    @pl.when(b == 0)
    def _():
        _page_copies(pt_ref, 0, n_pages, copy(0))

    @pl.when(b + 1 < nb)
    def _():
        _page_copies(pt_ref, b + 1, n_pages, copy(1 - slot))

    _page_waits(n_pages, copy(slot))

    q = q_ref[...]
    rows = q.shape[0]
    m = jnp.full((rows, 1), NEG_INF, F32)
    l = jnp.zeros((rows, 1), F32)
    acc = jnp.zeros((rows, MLA_KV_LORA), F32)
    pages_per_chunk = tk // PAGE_SIZE
    for c in range(n_pages // pages_per_chunk):
        kt = jnp.concatenate([kbuf[slot, c * pages_per_chunk + i].astype(BF16) for i in range(pages_per_chunk)],
                             axis=1)
        m, l, acc = _online_update(_dot(q, kt), None, kt[0:MLA_KV_LORA], m, l, acc, v_transposed=True)
    kn = knew_ref[...]
    t_q = lax.broadcasted_iota(jnp.int32, (rows, kn.shape[0]), 0) & (t_new - 1)
    t_k = lax.broadcasted_iota(jnp.int32, (rows, kn.shape[0]), 1)
    m, l, acc = _online_update(_dot_nt(q, kn), t_k <= t_q, kn[:, 0:MLA_KV_LORA], m, l, acc)
    o_ref[...] = _finalize(l, acc)


def _mla_decode(page_table, q_b, knew_b, pool):
    nb, n_pages = page_table.shape
    rows = q_b.shape[1]
    t_new = rows // MLA_HEADS
    tk = 1024
    past = n_pages * PAGE_SIZE
    assert past % tk == 0 and (t_new & (t_new - 1)) == 0
    grid_spec = pltpu.PrefetchScalarGridSpec(
        num_scalar_prefetch=1,
        grid=(nb,),
        in_specs=[pl.BlockSpec((None, rows, MLA_ROW), lambda b, pt: (b, 0, 0)),
                  pl.BlockSpec((None,) + knew_b.shape[1:], lambda b, pt: (b, 0, 0)),
                  pl.BlockSpec(memory_space=pl.ANY)],
        out_specs=pl.BlockSpec((None, rows, MLA_KV_LORA), lambda b, pt: (b, 0, 0)),
        scratch_shapes=[pltpu.VMEM((2, n_pages, MLA_ROW, PAGE_SIZE), F32), pltpu.SemaphoreType.DMA((2,))],
    )
    return pl.pallas_call(
        functools.partial(_mla_decode_kernel, n_pages=n_pages, t_new=t_new, tk=tk),
        grid_spec=grid_spec,
        out_shape=jax.ShapeDtypeStruct((nb, rows, MLA_KV_LORA), F32),
        compiler_params=_cparams(("arbitrary",)),
        name="mla_decode",
    )(page_table, q_b, knew_b, pool)


def _uv_kernel(o_ref, wuv_ref, y_ref):
    rows = o_ref.shape[1]
    y_ref[...] = _uv_project(o_ref[...].reshape(MLA_HEADS * rows, MLA_KV_LORA), wuv_ref, rows)


def _uv(o_heads, wuv):
    rows = o_heads.shape[1]
    return pl.pallas_call(
        _uv_kernel,
        out_shape=jax.ShapeDtypeStruct((rows, MLA_WIDTH), F32),
        name="mla_value_up",
    )(o_heads, wuv)


def _nsa_dec_cmp_kernel(pt_ref, qn_ref, cnew_ref, pool_ref, wc1_ref, pe_ref, b1_ref, w2_ref, impmap_ref,
                        ocmp_ref, selm_ref, pbuf, xrows, sem, *, n_pages, t_new, t_pad, n_chunk, n_cmp_blocks,
                        n_sel_blocks):
    b = pl.program_id(0)
    nb = pl.num_programs(0)
    slot = b % 2
    past = n_pages * PAGE_SIZE

    def copy(sl):
        return lambda page, p: pltpu.make_async_copy(pool_ref.at[page], pbuf.at[sl, p], sem.at[sl])

    @pl.when(b == 0)
    def _():
        _page_copies(pt_ref, 0, n_pages, copy(0))

    @pl.when(b + 1 < nb)
    def _():
        _page_copies(pt_ref, b + 1, n_pages, copy(1 - slot))

    _page_waits(n_pages, copy(slot))

    chunks_per_page = PAGE_SIZE // CMP_STRIDE

    def relayout(p, _):
        r0 = pl.multiple_of(p * (chunks_per_page * CHUNK_PITCH), 8)
        for s in range(2):
            t = pbuf[slot, p, s * LANES:(s + 1) * LANES, :].T
            for c in range(chunks_per_page):
                xrows[s, pl.ds(r0 + c * CHUNK_PITCH, CMP_STRIDE), :] = t[c * CMP_STRIDE:(c + 1) * CMP_STRIDE]
        return 0

    lax.fori_loop(0, n_pages, relayout, 0, unroll=4)

    cnew = cnew_ref[...]
    base = n_pages * chunks_per_page * CHUNK_PITCH
    tail = xrows.shape[1] - base
    for s in range(2):
        xrows[s, pl.ds(base, tail), :] = jnp.zeros((tail, LANES), F32)
        xrows[s, pl.ds(base, t_new), :] = cnew[:, s * LANES:(s + 1) * LANES]

    kvc = []
    for s in range(2):
        row = lambda p, s=s: xrows[s, pl.ds(p, n_chunk, stride=CHUNK_PITCH), :]
        load = lambda j, row=row: jnp.concatenate([row(2 * j), row(2 * j + 1)], axis=-1)
        kvc.append(_compress_slab(load, wc1_ref, pe_ref, b1_ref, w2_ref, s, n_chunk).astype(BF16))

    rows = NSA_HEADS * t_pad
    q_pos_r = past + (lax.broadcasted_iota(jnp.int32, (rows, 1), 0) & (t_pad - 1))
    q_pos_t = past + lax.broadcasted_iota(jnp.int32, (t_pad, 1), 0)
    o_cmp, imps = _cmp_branch(qn_ref[...], kvc[0][0:n_cmp_blocks], kvc[1][0:n_cmp_blocks], q_pos_r,
                              n_cmp_blocks, impmap_ref, t_pad)
    ocmp_ref[...] = o_cmp
    selm_ref[...] = _group_rows([_select_blocks(imp, q_pos_t, n_sel_blocks) for imp in imps], NSA_GROUP)


def _nsa_decode_cmp(page_table, qn_b, cnew_b, pool, wts, t_new, t_pad):
    nb, n_pages = page_table.shape
    past = n_pages * PAGE_SIZE
    total = past + t_new
    n_chunk_true = -(-total // CMP_STRIDE)
    nc = n_chunk_true - CMP_BLOCK // CMP_STRIDE + 1
    n_chunk = -(-n_chunk_true // 8) * 8
    assert nc % LANES == 0, "compressed-block count must be lane aligned"
    ns = past // SEL_BLOCK + (-(-t_new // SEL_BLOCK))
    sel_w = -(-ns // LANES) * LANES
    impmap = _importance_map(nc, nc, ns, sel_w)
    rows = NSA_HEADS * t_pad
    full = lambda a: pl.BlockSpec(a.shape, lambda b, pt: (0,) * a.ndim)
    per_b = lambda a: pl.BlockSpec((None,) + a.shape[1:], lambda b, pt: (b,) + (0,) * (a.ndim - 1))
    grid_spec = pltpu.PrefetchScalarGridSpec(
        num_scalar_prefetch=1,
        grid=(nb,),
        in_specs=[per_b(qn_b), per_b(cnew_b), pl.BlockSpec(memory_space=pl.ANY),
                  full(wts['wc1']), full(wts['pe']), full(wts['b1']), full(wts['w2']), full(impmap)],
        out_specs=[pl.BlockSpec((None, rows, LANES), lambda b, pt: (b, 0, 0)),
                   pl.BlockSpec((None, rows, sel_w), lambda b, pt: (b, 0, 0))],
        scratch_shapes=[pltpu.VMEM((2, n_pages, NSA_KV_WIDTH, PAGE_SIZE), F32),
                        pltpu.VMEM((2, n_chunk * CHUNK_PITCH, LANES), F32),
                        pltpu.SemaphoreType.DMA((2,))],
    )
    return pl.pallas_call(
        functools.partial(_nsa_dec_cmp_kernel, n_pages=n_pages, t_new=t_new, t_pad=t_pad, n_chunk=n_chunk,
                          n_cmp_blocks=nc, n_sel_blocks=ns),
        grid_spec=grid_spec,
        out_shape=[jax.ShapeDtypeStruct((nb, rows, LANES), F32),
                   jax.ShapeDtypeStruct((nb, rows, sel_w), F32)],
        compiler_params=_cparams(("arbitrary",)),
        name="nsa_decode_compress",
    )(page_table, qn_b, cnew_b, pool, wts['wc1'], wts['pe'], wts['b1'], wts['w2'], impmap)


def _nsa_dec_sel_kernel(pt_ref, qr_ref, snew_ref, wnew_ref, selm_ref, ocmp_ref, gates_ref, wst_ref, pool_ref,
                        o_ref, kbuf, sem, *, n_pages, t_new, t_pad, tk):
    b = pl.program_id(0)
    nb = pl.num_programs(0)
    slot = b % 2
    past = n_pages * PAGE_SIZE
    rows = NSA_HEADS * t_pad

    def copy(sl):
        return lambda page, p: pltpu.make_async_copy(pool_ref.at[page], kbuf.at[sl, p], sem.at[sl])

    @pl.when(b == 0)
    def _():
        _page_copies(pt_ref, 0, n_pages, copy(0))

    @pl.when(b + 1 < nb)
    def _():
        _page_copies(pt_ref, b + 1, n_pages, copy(1 - slot))

    _page_waits(n_pages, copy(slot))

    blk_per_chunk = tk // SEL_BLOCK
    lane_lo = lax.broadcasted_iota(jnp.int32, (rows, LANES), 1) < SEL_BLOCK
    n_new = snew_ref.shape[0]
    t_q = lax.broadcasted_iota(jnp.int32, (rows, n_new), 0) & (t_pad - 1)
    t_k = lax.broadcasted_iota(jnp.int32, (rows, n_new), 1)
    new_causal = (t_k <= t_q) & (t_k < t_new)
    w_buf = wst_ref.shape[1]
    t_qw = lax.broadcasted_iota(jnp.int32, (rows, w_buf), 0) & (t_pad - 1)
    i_w = lax.broadcasted_iota(jnp.int32, (rows, w_buf), 1)
    win_mask = (t_qw + w_buf - i_w) < WINDOW
    snew = snew_ref[...]
    wnew = wnew_ref[...]
    qr = qr_ref[...]
    selm = selm_ref[...]

    def fresh():
        return jnp.full((rows, 1), NEG_INF, F32), jnp.zeros((rows, 1), F32), jnp.zeros((rows, LANES), F32)

    m, l, acc = fresh()
    pages_per_chunk = tk // PAGE_SIZE
    for c in range(past // tk):
        chunk_pages = range(c * pages_per_chunk, (c + 1) * pages_per_chunk)
        kt = jnp.concatenate([kbuf[slot, p, 0:LANES, :].astype(BF16) for p in chunk_pages], axis=1)
        vt = jnp.concatenate([kbuf[slot, p, LANES:2 * LANES, :].astype(BF16) for p in chunk_pages], axis=1)
        pieces = []
        for i in range(tk // LANES):
            jb = c * blk_per_chunk + 2 * i
            pieces.append(jnp.where(lane_lo, selm[:, jb:jb + 1], selm[:, jb + 1:jb + 2]))
        mask = jnp.concatenate(pieces, axis=-1) > 0.5
        m, l, acc = _online_update(_dot(qr, kt), mask, vt, m, l, acc, v_transposed=True)
    nb_past = past // SEL_BLOCK
    mask = (selm[:, nb_past:nb_past + 1] > 0.5) & new_causal
    m, l, acc = _online_update(_dot_nt(qr, snew[:, 0:LANES].astype(BF16)), mask,
                               snew[:, LANES:2 * LANES].astype(BF16), m, l, acc)
    o_sel = _finalize(l, acc)

    m, l, acc = fresh()
    m, l, acc = _online_update(_dot(qr, wst_ref[0:LANES, :].astype(BF16)), win_mask,
                               wst_ref[LANES:2 * LANES, :].astype(BF16), m, l, acc, v_transposed=True)
    m, l, acc = _online_update(_dot_nt(qr, wnew[:, 0:LANES].astype(BF16)), new_causal,
                               wnew[:, LANES:2 * LANES].astype(BF16), m, l, acc)
    o_win = _finalize(l, acc)

    o_ref[...] = _mix_heads(gates_ref[...], ocmp_ref[...], o_sel, o_win, t_pad)


def _nsa_decode_sel(page_table, qr_b, snew_b, wnew_b, selm, ocmp, gates_b, win_state, pool, t_new, t_pad):
    nb, n_pages = page_table.shape
    past = n_pages * PAGE_SIZE
    tk = 1024
    assert past % tk == 0
    per_b = lambda a: pl.BlockSpec((None,) + a.shape[1:], lambda b, pt: (b,) + (0,) * (a.ndim - 1))
    grid_spec = pltpu.PrefetchScalarGridSpec(
        num_scalar_prefetch=1,
        grid=(nb,),
        in_specs=[per_b(qr_b), per_b(snew_b), per_b(wnew_b), per_b(selm), per_b(ocmp), per_b(gates_b),
                  per_b(win_state), pl.BlockSpec(memory_space=pl.ANY)],
        out_specs=pl.BlockSpec((None, t_pad, NSA_WIDTH), lambda b, pt: (b, 0, 0)),
        scratch_shapes=[pltpu.VMEM((2, n_pages, NSA_KV_WIDTH, PAGE_SIZE), F32), pltpu.SemaphoreType.DMA((2,))],
    )
    return pl.pallas_call(
        functools.partial(_nsa_dec_sel_kernel, n_pages=n_pages, t_new=t_new, t_pad=t_pad, tk=tk),
        grid_spec=grid_spec,
        out_shape=jax.ShapeDtypeStruct((nb, t_pad, NSA_WIDTH), F32),
        compiler_params=_cparams(("arbitrary",)),
        name="nsa_decode_select_window",
    )(page_table, qr_b, snew_b, wnew_b, selm, ocmp, gates_b, win_state, pool)


def _prep_weights(w_in, q_norm_g, w_uq, kv_norm_g, w_uk, w_uv, cmp_pos_emb, cmp_w1, cmp_b1, cmp_w2, w_o, ln_g, ln_b):
    cuts = np.cumsum([MLA_Q_LORA, MLA_KV_LORA, MLA_ROPE_DIM, MLA_WIDTH, NSA_WIDTH, NSA_KV_WIDTH, NSA_KV_WIDTH,
                      NSA_KV_WIDTH, 3 * NSA_HEADS])[:].tolist()
    c_q, c_kv, k_r, z_mla, q_n, cmp_kv, sel_kv, win_kv, g_br, z_nsa = jnp.split(w_in, cuts, axis=1)
    pad = jnp.zeros((D_MODEL, LANES - MLA_ROPE_DIM - 3 * NSA_HEADS), w_in.dtype)
    w_in_p = jnp.concatenate([c_q, c_kv, z_mla, q_n, cmp_kv, sel_kv, win_kv, z_nsa, k_r, g_br, pad], axis=1)
    assert w_in_p.shape[1] == IN_WIDTH_P
    uq = w_uq.reshape(MLA_Q_LORA, MLA_HEADS, MLA_NOPE_DIM + MLA_ROPE_DIM)
    w_uq_p = jnp.concatenate([uq[:, :, :MLA_NOPE_DIM].reshape(MLA_Q_LORA, -1),
                              uq[:, :, MLA_NOPE_DIM:].reshape(MLA_Q_LORA, -1)], axis=1)
    w_uk_t = jnp.transpose(w_uk.reshape(MLA_KV_LORA, MLA_HEADS, MLA_NOPE_DIM), (1, 2, 0))
    w_uv_h = jnp.transpose(w_uv.reshape(MLA_KV_LORA, MLA_HEADS, MLA_V_DIM), (1, 0, 2))
    ratio = CMP_BLOCK // CMP_STRIDE
    eye = jnp.eye(NSA_KV_HEADS, dtype=w_in.dtype)
    w1r = cmp_w1.reshape(2, ratio, CMP_STRIDE, NSA_HEAD_DIM, NSA_HEAD_DIM)
    wc1 = jnp.einsum('gh,srpde->spgdrhe', eye, w1r).reshape(2, CMP_PAIRS, 2 * LANES, ratio * LANES)
    w2 = jnp.einsum('gh,sde->sgdhe', eye, cmp_w2).reshape(2, LANES, LANES)
    pe = jnp.transpose(cmp_pos_emb, (1, 0, 2))
    pe = jnp.concatenate([pe, pe], axis=-1).reshape(2, ratio, CMP_PAIRS, 2 * LANES)
    pe = jnp.transpose(pe, (0, 2, 1, 3))
    pe = jnp.pad(pe, ((0, 0), (0, 0), (0, 8 - ratio), (0, 0)))
    b1 = jnp.concatenate([cmp_b1, cmp_b1], axis=-1).reshape(2, 1, LANES)
    return dict(w_in=w_in_p.astype(BF16), q_norm_g=q_norm_g.reshape(1, -1), w_uq=w_uq_p.astype(BF16),
                w_uk=w_uk_t.astype(BF16), kv_norm_g=kv_norm_g.reshape(1, -1), w_uv=w_uv_h.astype(BF16),
                wc1=wc1.astype(BF16), w2=w2.astype(BF16), pe=pe, b1=b1,
                w_o=w_o.astype(BF16), ln_g=ln_g.reshape(1, -1), ln_b=ln_b.reshape(1, -1))


def _prompt_layer(x, wts):
    batch, seq, _ = x.shape
    x2d = x.reshape(batch * seq, D_MODEL)
    tm = min(256, seq)
    P = _project(x2d, _rope_tables(jnp.arange(seq, dtype=jnp.int32)), wts, tm)
    o_mla = _mla_prompt(P['qmla'], P['mrowb'], wts['w_uv'], batch, seq)
    kvc = _compress_prompt(P['cmps'], wts, batch, seq)
    o_nsa = _nsa_prompt(P, kvc, batch, seq)
    y = _finish(x2d, o_mla, o_nsa, P['szm'], P['szn'], wts, tm)
    kvd = (2, NSA_KV_HEADS, NSA_HEAD_DIM)
    w_keep = min(WINDOW, seq)
    return (y.reshape(batch, seq, D_MODEL),
            P['mrow'].reshape(batch, seq, MLA_ROW),
            P['cmp'].reshape((batch, seq) + kvd),
            P['sel'].reshape((batch, seq) + kvd),
            P['win'].reshape((batch, seq) + kvd)[:, seq - w_keep:])


def _sample_layer(x, cache_mla, cache_cmp, cache_sel, win_state, page_table, wts):
    nb, t_new, _ = x.shape
    n_pages = page_table.shape[1]
    past = n_pages * PAGE_SIZE
    n = nb * t_new
    t_pad = 8
    x2d = x.reshape(n, D_MODEL)
    pos = past + (jnp.arange(n, dtype=jnp.int32) % t_new)
    P = _project(x2d, _rope_tables(pos), wts, n)
    kvd = (2, NSA_KV_HEADS, NSA_HEAD_DIM)
    n_pool = cache_mla.shape[0]

    q_b = jnp.transpose(P['qmla'][:, :, :MLA_ROW].reshape(MLA_HEADS, nb, t_new, MLA_ROW), (1, 0, 2, 3))
    q_b = q_b.reshape(nb, -1, MLA_ROW)
    knew_b = jnp.pad(P['mrowb'][:, :MLA_ROW].reshape(nb, t_new, MLA_ROW), ((0, 0), (0, 16 - t_new), (0, 0)))
    feat_major = lambda a: jnp.swapaxes(a.reshape(a.shape[0], a.shape[1], -1), 1, 2)
    o_lat = _mla_decode(page_table, q_b, knew_b, feat_major(cache_mla))
    o_heads = jnp.transpose(o_lat.reshape(nb, MLA_HEADS, t_new, MLA_KV_LORA), (1, 0, 2, 3)).reshape(MLA_HEADS, n, -1)
    o_mla = _uv(o_heads, wts['w_uv'])

    def q_rows(q):
        q = jnp.transpose(q.reshape(NSA_HEADS, nb, t_new, LANES), (1, 0, 2, 3))
        q = jnp.pad(q, ((0, 0), (0, 0), (0, t_pad - t_new), (0, 0)))
        return q.reshape(nb, NSA_HEADS * t_pad, LANES)

    cnew_b = P['cmp'].reshape(nb, t_new, NSA_KV_WIDTH)
    ocmp, selm = _nsa_decode_cmp(page_table, q_rows(P['qn']), cnew_b, feat_major(cache_cmp), wts, t_new, t_pad)
    pad_rows = lambda a, r: jnp.pad(a.reshape(nb, t_new, -1), ((0, 0), (0, r - t_new), (0, 0)))
    o_nsa = _nsa_decode_sel(page_table, q_rows(P['qrot']), pad_rows(P['sel'], 16), pad_rows(P['win'], 16), selm,
                            ocmp, pad_rows(P['gates'], t_pad), feat_major(win_state), feat_major(cache_sel),
                            t_new, t_pad)
    o_nsa = o_nsa[:, :t_new].reshape(n, NSA_WIDTH)

    y = _finish(x2d, o_mla, o_nsa, P['szm'], P['szn'], wts, n)
    wst = win_state.reshape(nb, -1, NSA_KV_WIDTH)
    win_all = jnp.concatenate([wst, P['win'].reshape(nb, t_new, NSA_KV_WIDTH)], axis=1)[:, t_new:]
    return (y.reshape(nb, t_new, D_MODEL),
            P['mrow'].reshape(nb, t_new, MLA_ROW),
            P['cmp'].reshape((nb, t_new) + kvd),
            P['sel'].reshape((nb, t_new) + kvd),
            win_all.reshape((nb, win_all.shape[1]) + kvd))


def kernel(x_prompt, x_sample, cache_mla, cache_cmp_kv, cache_sel_kv, state_win_kv, page_table, w_in, q_norm_g, w_uq,
           kv_norm_g, w_uk, w_uv, cmp_pos_emb, cmp_w1, cmp_b1, cmp_w2, w_o, ln_g, ln_b):
    assert w_in.shape[0] == DEPTH
    wts = _prep_weights(w_in[0], q_norm_g[0], w_uq[0], kv_norm_g[0], w_uk[0], w_uv[0], cmp_pos_emb[0], cmp_w1[0],
                        cmp_b1[0], cmp_w2[0], w_o[0], ln_g[0], ln_b[0])
    yp, p_mla, p_cmp, p_sel, p_win = _prompt_layer(x_prompt, wts)
    drop_depth = lambda a: a.reshape(a.shape[1:])
    ys, s_mla, s_cmp, s_sel, s_win = _sample_layer(x_sample, drop_depth(cache_mla), drop_depth(cache_cmp_kv),
                                                   drop_depth(cache_sel_kv), drop_depth(state_win_kv), page_table, wts)
    add_depth = lambda a: a[None]
    return (yp, ys) + tuple(add_depth(a) for a in (p_mla, p_cmp, p_sel, p_win, s_mla, s_cmp, s_sel, s_win))
```

```python
import functools

import numpy as np
import jax
import jax.numpy as jnp
from jax import lax
from jax.experimental import pallas as pl
from jax.experimental.pallas import tpu as pltpu

F32 = jnp.float32
BF16 = jnp.bfloat16

D_MODEL = 1024
PAGE_SIZE = 128
MLA_HEADS = 8
MLA_V_DIM = 64
MLA_NOPE_DIM = 64
MLA_ROPE_DIM = 32
MLA_Q_LORA = 256
MLA_KV_LORA = 128
MLA_WIDTH = MLA_HEADS * MLA_V_DIM
MLA_ROW = MLA_KV_LORA + MLA_ROPE_DIM
NSA_HEADS = 8
NSA_KV_HEADS = 2
NSA_HEAD_DIM = 64
NSA_GROUP = NSA_HEADS // NSA_KV_HEADS
NSA_WIDTH = NSA_HEADS * NSA_HEAD_DIM
NSA_KV_WIDTH = 2 * NSA_KV_HEADS * NSA_HEAD_DIM
CMP_BLOCK = 32
CMP_STRIDE = 16
SEL_BLOCK = 64
SEL_TOP_N = 16
WINDOW = 512
Q_BLOCK = 128
ROPE_THETA = 10000.0
RMS_EPS = 1e-6
LN_EPS = 1e-5
NEG_INF = -1e30
MLA_SCALE = (MLA_NOPE_DIM + MLA_ROPE_DIM) ** -0.5
NSA_SCALE = NSA_HEAD_DIM ** -0.5
DEPTH = 1
ALPHA = (2 * DEPTH) ** 0.25

LANES = 128
LOG2E = 1.4426950408889634
MLA_KROW = 2 * LANES
MLA_ONES_LANE = MLA_ROW
NSA_KROW = 3 * LANES

SEG_CQ = (0, 256)
SEG_CKV = (256, 128)
SEG_ZMLA = (384, 512)
SEG_QN = (896, 512)
SEG_CMP = (1408, 256)
SEG_SEL = (1664, 256)
SEG_WIN = (1920, 256)
SEG_ZNSA = (2176, 512)
SEG_MISC = (2688, 128)
IN_WIDTH_P = 2816
MISC_GATE0 = MLA_ROPE_DIM

VMEM_LIMIT = 48 * 1024 * 1024

def _cparams(sem, flags=None):
    return pltpu.CompilerParams(dimension_semantics=sem, vmem_limit_bytes=VMEM_LIMIT, flags=flags)


def _sigmoid(x):
    return 1.0 / (1.0 + jnp.exp(-x))


def _silu(x):
    return x * _sigmoid(x)


def _rope_slab(x, c, sa, sb, half):
    return x * c + pltpu.roll(x, LANES - half, 1) * sa + pltpu.roll(x, half, 1) * sb


def _dot_nt(a, b):
    return lax.dot_general(a, b, (((1,), (1,)), ((), ())), preferred_element_type=F32)


def _dot(a, b):
    return jnp.dot(a, b, preferred_element_type=F32)


def _masked_softmax(s, mask):
    sm = jnp.where(mask, s, NEG_INF)
    m = jnp.max(sm, axis=-1, keepdims=True)
    e = jnp.exp2(sm - m)
    p = e / jnp.sum(e, axis=-1, keepdims=True)
    return jnp.where(mask, p, 0.0)


def _online_update(s, mask, v, m, l, acc, v_transposed=False):
    sm = s if mask is None else jnp.where(mask, s, NEG_INF)
    m_new = jnp.maximum(m, jnp.max(sm, axis=-1, keepdims=True))
    alpha = jnp.exp2(m - m_new)
    p = jnp.exp2(sm - m_new)
    if mask is not None:
        p = jnp.where(mask, p, 0.0)
    l_new = alpha * l + jnp.sum(p, axis=-1, keepdims=True)
    pv = _dot_nt(p.astype(BF16), v) if v_transposed else _dot(p.astype(BF16), v)
    return m_new, l_new, alpha * acc + pv


def _finalize(l, acc):
    ok = l > 0.0
    return jnp.where(ok, acc / jnp.where(ok, l, 1.0), 0.0)


def _flash_step(s, bias, v_ext, m, acc):
    sm = s if bias is None else s + bias
    m_new = jnp.maximum(m, jnp.max(sm, axis=-1, keepdims=True))
    p = jnp.exp2(sm - m_new)
    return m_new, jnp.exp2(m - m_new) * acc + _dot(p.astype(BF16), v_ext)


def _flash_result(acc, ones_lane):
    return acc[:, 0:LANES] * (1.0 / acc[:, ones_lane:ones_lane + 1])


def _tile_rows(x, reps):
    return jnp.concatenate([x] * reps, axis=0)


def _select_blocks(imp, q_pos, n_sel_blocks):
    rows, width = imp.shape
    j = lax.broadcasted_iota(jnp.int32, (rows, width), 1)
    cur = lax.shift_right_logical(q_pos, 6)
    avail = (j * SEL_BLOCK <= q_pos) & (j < n_sel_blocks)
    forced = (j == 0) | (j == cur) | (j == cur - 1)
    val = jnp.where(avail, jnp.where(forced, jnp.inf, imp), -jnp.inf)
    rank = jnp.zeros((rows, width), F32)
    for jp in range(n_sel_blocks):
        col = val[:, jp:jp + 1]
        ahead = (col > val) | ((col == val) & (j > jp))
        rank = rank + jnp.where(ahead, 1.0, 0.0)
    keep = (rank < float(min(SEL_TOP_N, n_sel_blocks))) & avail
    return jnp.where(keep, 1.0, 0.0)


def _select_blocks_t(imp_t, q_pos, n_sel_blocks):
    n_rows, width = imp_t.shape
    j = lax.broadcasted_iota(jnp.int32, (n_rows, width), 0)
    cur = lax.shift_right_logical(q_pos, 6)
    avail = (j * SEL_BLOCK <= q_pos) & (j < n_sel_blocks)
    forced = (j == 0) | (j == cur) | (j == cur - 1)
    val = jnp.where(avail, jnp.where(forced, jnp.inf, imp_t), -jnp.inf)
    n_pieces = -(-n_sel_blocks // 8)
    pieces = [val[8 * v:8 * v + 8] for v in range(n_pieces)]
    jrow = lax.broadcasted_iota(jnp.int32, (8, width), 0)
    ranks = [jnp.zeros((8, width), F32) for _ in range(n_pieces)]
    for jp in range(n_sel_blocks):
        cand = jnp.broadcast_to(val[jp:jp + 1], (8, width))
        for v in range(n_pieces):
            if 8 * v > jp:
                ahead = cand >= pieces[v]
            elif 8 * v + 7 < jp:
                ahead = cand > pieces[v]
            else:
                ahead = (cand > pieces[v]) | ((cand == pieces[v]) & (jrow > jp - 8 * v))
            ranks[v] = ranks[v] + jnp.where(ahead, 1.0, 0.0)
    if n_rows > 8 * n_pieces:
        ranks.append(jnp.zeros((n_rows - 8 * n_pieces, width), F32))
    rank = jnp.concatenate(ranks, axis=0)
    keep = (rank < float(min(SEL_TOP_N, n_sel_blocks))) & avail
    return jnp.where(keep, 1.0, 0.0)


def _proj_kernel(x_ref, w_ref, qg_ref, wuq_ref, wuk_ref, kvg_ref, tab_ref,
                 qmla_ref, mrow_ref, mrowb_ref, szm_ref, qn_ref, qrot_ref, cmp_ref, cmps_ref,
                 sel_ref, selb_ref, win_ref, winb_ref, gates_ref, szn_ref, *, feature_major):
    xb = x_ref[...].astype(BF16)

    def seg(s):
        return _dot(xb, w_ref[:, s[0]:s[0] + s[1]])

    cn, san, sbn = tab_ref[0], tab_ref[1], tab_ref[2]
    cm, sam, sbm = tab_ref[3], tab_ref[4], tab_ref[5]

    cq = seg(SEG_CQ)
    r = cq * lax.rsqrt(jnp.mean(cq * cq, axis=-1, keepdims=True) + RMS_EPS) * qg_ref[...]
    q = _dot(r.astype(BF16), wuq_ref[...])
    nope_w = MLA_HEADS * MLA_NOPE_DIM
    lane = lax.broadcasted_iota(jnp.int32, (q.shape[0], LANES), 1)
    for h in range(MLA_HEADS):
        ql = _dot(q[:, h * MLA_NOPE_DIM:(h + 1) * MLA_NOPE_DIM].astype(BF16), wuk_ref[h])
        qmla_ref[h, :, 0:LANES] = (ql * (MLA_SCALE * LOG2E)).astype(BF16)
    heads_per_slab = LANES // MLA_ROPE_DIM
    for jj in range(MLA_HEADS // heads_per_slab):
        qr = _rope_slab(q[:, nope_w + jj * LANES: nope_w + (jj + 1) * LANES], cm, sam, sbm, MLA_ROPE_DIM // 2)
        qr = qr * (MLA_SCALE * LOG2E)
        for hh in range(heads_per_slab):
            front = qr if hh == 0 else pltpu.roll(qr, LANES - hh * MLA_ROPE_DIM, 1)
            qmla_ref[jj * heads_per_slab + hh, :, LANES:2 * LANES] = (
                jnp.where(lane < MLA_ROPE_DIM, front, 0.0).astype(BF16))

    ckv = seg(SEG_CKV)
    lat = ckv * lax.rsqrt(jnp.mean(ckv * ckv, axis=-1, keepdims=True) + RMS_EPS) * kvg_ref[...]
    misc = seg(SEG_MISC)
    kr = _rope_slab(misc, cm, sam, sbm, MLA_ROPE_DIM // 2)
    if feature_major:
        mrow_ref[0:MLA_KV_LORA, :] = lat.T
        mrow_ref[MLA_KV_LORA:MLA_ROW, :] = kr.T[0:MLA_ROPE_DIM]
    else:
        mrow_ref[:, 0:MLA_KV_LORA] = lat
        mrow_ref[:, MLA_KV_LORA:MLA_ROW] = kr[:, 0:MLA_ROPE_DIM]
    mrowb_ref[:, 0:LANES] = lat.astype(BF16)
    ones_col = jnp.where(lane == MLA_ONES_LANE - LANES, 1.0, 0.0)
    mrowb_ref[:, LANES:2 * LANES] = jnp.where(lane < MLA_ROPE_DIM, kr, ones_col).astype(BF16)
    gates_ref[...] = _sigmoid(misc)

    szm_ref[...] = _silu(seg(SEG_ZMLA))
    szn_ref[...] = _silu(seg(SEG_ZNSA))

    qn = seg(SEG_QN)
    for jj in range(NSA_WIDTH // LANES):
        raw = qn[:, jj * LANES:(jj + 1) * LANES] * (NSA_SCALE * LOG2E)
        rot = _rope_slab(qn[:, jj * LANES:(jj + 1) * LANES], cn, san, sbn, NSA_HEAD_DIM // 2) * (NSA_SCALE * LOG2E)
        for src, dst in ((raw, qn_ref), (rot, qrot_ref)):
            swapped = pltpu.roll(src, NSA_HEAD_DIM, 1)
            for half in range(2):
                hd = 2 * jj + half
                g = hd // NSA_GROUP
                keep = (lane >= g * NSA_HEAD_DIM) & (lane < (g + 1) * NSA_HEAD_DIM)
                dst[hd] = jnp.where(keep, src if half == g else swapped, 0.0).astype(BF16)

    def put_rows(dst, first, second):
        if feature_major:
            dst[0:LANES, :] = first.T
            dst[LANES:2 * LANES, :] = second.T
        else:
            dst[:, 0:LANES] = first
            dst[:, LANES:2 * LANES] = second

    cmpv = seg(SEG_CMP)
    put_rows(cmp_ref, cmpv[:, 0:LANES], cmpv[:, LANES:2 * LANES])
    cmps_ref[0] = cmpv[:, 0:LANES]
    cmps_ref[1] = cmpv[:, LANES:2 * LANES]

    for src, dst, dstb in ((SEG_SEL, sel_ref, selb_ref), (SEG_WIN, win_ref, winb_ref)):
        kv = seg(src)
        k = _rope_slab(kv[:, 0:LANES], cn, san, sbn, NSA_HEAD_DIM // 2)
        v = kv[:, LANES:2 * LANES]
        put_rows(dst, k, v)
        dstb[:, 0:LANES] = k.astype(BF16)
        dstb[:, LANES:2 * LANES] = v.astype(BF16)
        dstb[:, 2 * LANES:3 * LANES] = jnp.where(lane == 0, 1.0, 0.0).astype(BF16)


def _rope_tables(pos):
    def tab(d):
        inv = 1.0 / (ROPE_THETA ** (jnp.arange(0, d, 2, dtype=F32) / d))
        ang = pos.astype(F32)[:, None] * inv[None, :]
        cos, sin = jnp.cos(ang), jnp.sin(ang)
        zero = jnp.zeros_like(sin)
        reps = LANES // d
        return (jnp.tile(jnp.concatenate([cos, cos], -1), (1, reps)),
                jnp.tile(jnp.concatenate([-sin, zero], -1), (1, reps)),
                jnp.tile(jnp.concatenate([zero, sin], -1), (1, reps)))
    return jnp.stack(tab(NSA_HEAD_DIM) + tab(MLA_ROPE_DIM))


def _project(x2d, tabs, wts, tm, feature_major=False):
    n = x2d.shape[0]
    p_rows = tabs.shape[1]
    assert n % tm == 0 and p_rows % tm == 0
    nt = p_rows // tm
    row = lambda w: pl.BlockSpec((tm, w), lambda i: (i, 0))
    full = lambda a: pl.BlockSpec(a.shape, lambda i: (0,) * a.ndim)
    if feature_major:
        out_rows = lambda w: ((n // p_rows, w, p_rows), F32, pl.BlockSpec((None, w, tm), lambda i: (i // nt, 0, i % nt)))
    else:
        out_rows = lambda w: ((n, w), F32, row(w))
    out_shapes = dict(
        qmla=((MLA_HEADS, n, MLA_KROW), BF16, pl.BlockSpec((MLA_HEADS, tm, MLA_KROW), lambda i: (0, i, 0))),
        mrow=out_rows(MLA_ROW),
        mrowb=((n, MLA_KROW), BF16, row(MLA_KROW)),
        szm=((n, MLA_WIDTH), F32, row(MLA_WIDTH)),
        qn=((NSA_HEADS, n, LANES), BF16, pl.BlockSpec((NSA_HEADS, tm, LANES), lambda i: (0, i, 0))),
        qrot=((NSA_HEADS, n, LANES), BF16, pl.BlockSpec((NSA_HEADS, tm, LANES), lambda i: (0, i, 0))),
        cmp=out_rows(NSA_KV_WIDTH),
        cmps=((2, n, LANES), F32, pl.BlockSpec((2, tm, LANES), lambda i: (0, i, 0))),
        sel=out_rows(NSA_KV_WIDTH),
        selb=((n, NSA_KROW), BF16, row(NSA_KROW)),
        win=out_rows(NSA_KV_WIDTH),
        winb=((n, NSA_KROW), BF16, row(NSA_KROW)),
        gates=((n, LANES), F32, row(LANES)),
        szn=((n, NSA_WIDTH), F32, row(NSA_WIDTH)),
    )
    names = list(out_shapes)
    outs = pl.pallas_call(
        functools.partial(_proj_kernel, feature_major=feature_major),
        grid=(n // tm,),
        in_specs=[row(D_MODEL), full(wts['w_in']), full(wts['q_norm_g']), full(wts['w_uq']), full(wts['w_uk']),
                  full(wts['kv_norm_g']), pl.BlockSpec((6, tm, LANES), lambda i: (0, i % nt, 0))],
        out_specs=[out_shapes[k][2] for k in names],
        out_shape=[jax.ShapeDtypeStruct(out_shapes[k][0], out_shapes[k][1]) for k in names],
        compiler_params=_cparams(("parallel",)),
        name="projection",
    )(x2d, wts['w_in'], wts['q_norm_g'], wts['w_uq'], wts['w_uk'], wts['kv_norm_g'], tabs)
    return dict(zip(names, outs))


def _uv_project(o, wuv_ref, rows):
    return jnp.concatenate(
        [_dot(o[h * rows:(h + 1) * rows].astype(BF16), wuv_ref[h]) for h in range(MLA_HEADS)], axis=-1)


def _mla_prompt_kernel(q_ref, k_ref, wuv_ref, o_ref, *, tq, tk):
    qi = pl.program_id(1)
    rows = MLA_HEADS * tq
    q = q_ref[...].reshape(rows, MLA_KROW)
    n_full = (qi * tq) // tk

    def chunk(c):
        return k_ref[pl.ds(pl.multiple_of(c * tk, tk), tk), :]

    def full_body(c, carry):
        kc = chunk(c)
        return _flash_step(_dot_nt(q, kc), None, kc, *carry)

    init = (jnp.full((rows, 1), NEG_INF, F32), jnp.zeros((rows, MLA_KROW), F32))
    carry = lax.fori_loop(0, n_full, full_body, init)
    kc = chunk(n_full)
    k_pos = n_full * tk + lax.broadcasted_iota(jnp.int32, (tq, tk), 1)
    q_pos = qi * tq + lax.broadcasted_iota(jnp.int32, (tq, tk), 0)
    bias = _tile_rows(jnp.where(k_pos <= q_pos, 0.0, NEG_INF), MLA_HEADS)
    _, acc = _flash_step(_dot_nt(q, kc), bias, kc, *carry)
    o_ref[...] = _uv_project(_flash_result(acc, MLA_ONES_LANE), wuv_ref, tq)


def _mla_prompt(qmla, mrowb, wuv, batch, seq):
    tq = min(256, seq)
    tk = min(512, seq)
    nq = seq // tq
    assert tk % tq == 0 and seq % tk == 0
    return pl.pallas_call(
        functools.partial(_mla_prompt_kernel, tq=tq, tk=tk),
        grid=(batch, nq),
        in_specs=[pl.BlockSpec((MLA_HEADS, tq, MLA_KROW), lambda b, i: (0, b * nq + i, 0)),
                  pl.BlockSpec((None, seq, MLA_KROW), lambda b, i: (b, 0, 0)),
                  pl.BlockSpec(wuv.shape, lambda b, i: (0, 0, 0))],
        out_specs=pl.BlockSpec((tq, MLA_WIDTH), lambda b, i: (b * nq + i, 0)),
        out_shape=jax.ShapeDtypeStruct((batch * seq, MLA_WIDTH), F32),
        compiler_params=_cparams(("parallel", "parallel")),
        name="mla_prompt",
    )(qmla, mrowb.reshape(batch, seq, MLA_KROW), wuv)


CMP_PAIRS = CMP_STRIDE // 2
CHUNK_PITCH = 24


def _compress_slab(load_pair, wc1_ref, pe_ref, b1_ref, w2_ref, s, n_chunk):
    proj = jnp.zeros((n_chunk, 2 * LANES), F32)
    pe = jnp.zeros((8, 2 * LANES), F32)
    for j in range(CMP_PAIRS):
        w = wc1_ref[s, j]
        proj = proj + _dot(load_pair(j).astype(BF16), w)
        pe = pe + _dot(pe_ref[s, j].astype(BF16), w)
    bias = b1_ref[s] + pe[0:1, 0:LANES] + pe[1:2, LANES:2 * LANES]
    hid = proj[:, 0:LANES] + pltpu.roll(proj[:, LANES:2 * LANES], n_chunk - 1, 0) + bias
    return _dot(_silu(hid).astype(BF16), w2_ref[s])


def _compress_prompt_kernel(x_ref, wc1_ref, pe_ref, b1_ref, w2_ref, o_ref, *, n_chunk):
    for s in range(2):
        row = lambda p, s=s: x_ref[s, pl.ds(p, n_chunk, stride=CMP_STRIDE), :]
        load = lambda j, row=row: jnp.concatenate([row(2 * j), row(2 * j + 1)], axis=-1)
        o_ref[s] = _compress_slab(load, wc1_ref, pe_ref, b1_ref, w2_ref, s, n_chunk).astype(BF16)


def _compress_prompt(cmps, wts, batch, seq):
    n_chunk = seq // CMP_STRIDE
    full = lambda a: pl.BlockSpec(a.shape, lambda b: (0,) * a.ndim)
    return pl.pallas_call(
        functools.partial(_compress_prompt_kernel, n_chunk=n_chunk),
        grid=(batch,),
        in_specs=[pl.BlockSpec((2, seq, LANES), lambda b: (0, b, 0)),
                  full(wts['wc1']), full(wts['pe']), full(wts['b1']), full(wts['w2'])],
        out_specs=pl.BlockSpec((None, 2, n_chunk, LANES), lambda b: (b, 0, 0, 0)),
        out_shape=jax.ShapeDtypeStruct((batch, 2, n_chunk, LANES), BF16),
        compiler_params=_cparams(("parallel",)),
        name="compress_prompt",
    )(cmps, wts['wc1'], wts['pe'], wts['b1'], wts['w2'])


def _cmp_branch(q, kc, vc, q_pos_rows, n_cmp_blocks, impmap_ref, rows_t, imp_transposed=False):
    s = _dot_nt(q, kc)
    n = lax.broadcasted_iota(jnp.int32, (rows_t, s.shape[1]), 1)
    visible = (n * CMP_STRIDE + (CMP_BLOCK - 1) <= q_pos_rows[0:rows_t]) & (n < n_cmp_blocks)
    sm = s + _tile_rows(jnp.where(visible, 0.0, NEG_INF), NSA_HEADS)
    e = jnp.exp2(sm - jnp.max(sm, axis=-1, keepdims=True))
    any_visible = (q_pos_rows >= CMP_BLOCK - 1) & (n_cmp_blocks > 0)
    p = e * jnp.where(any_visible, 1.0 / jnp.sum(e, axis=-1, keepdims=True), 0.0)
    o_cmp = _dot(p.astype(BF16), vc)
    imps = []
    for g in range(NSA_KV_HEADS):
        base = g * NSA_GROUP * rows_t
        psum = p[base:base + rows_t]
        for hh in range(1, NSA_GROUP):
            psum = psum + p[base + hh * rows_t: base + (hh + 1) * rows_t]
        hi = psum.astype(BF16)
        lo = (psum - hi.astype(F32)).astype(BF16)
        if imp_transposed:
            imps.append(_dot_nt(impmap_ref[...], hi) + _dot_nt(impmap_ref[...], lo))
        else:
            imps.append(_dot(hi, impmap_ref[...]) + _dot(lo, impmap_ref[...]))
    return o_cmp, imps


def _mix_heads(gates, o_cmp, o_sel, o_win, rows_t):
    outs = []
    for hd in range(NSA_HEADS):
        g = hd // NSA_GROUP
        rs = slice(hd * rows_t, (hd + 1) * rows_t)
        ls = slice(g * NSA_HEAD_DIM, (g + 1) * NSA_HEAD_DIM)
        gcol = lambda br: gates[:, MISC_GATE0 + br * NSA_HEADS + hd: MISC_GATE0 + br * NSA_HEADS + hd + 1]
        outs.append(gcol(0) * o_cmp[rs, ls] + gcol(1) * o_sel[rs, ls] + gcol(2) * o_win[rs, ls])
    return jnp.concatenate(outs, axis=-1)


def _group_rows(per_group, reps):
    return jnp.concatenate([per_group[g] for g in range(NSA_KV_HEADS) for _ in range(reps)], axis=0)


def _nsa_prompt_kernel(qn_ref, qrot_ref, kvc_ref, sel_ref, win_ref, gates_ref, impmap_ref, o_ref,
                       *, tq, tk, n_cmp_blocks, n_sel_blocks, wlen):
    qi = pl.program_id(1)
    rows = NSA_HEADS * tq
    q_pos_t = qi * tq + lax.broadcasted_iota(jnp.int32, (tq, 1), 0)
    q_pos_r = qi * tq + (lax.broadcasted_iota(jnp.int32, (rows, 1), 0) & (tq - 1))
    qn = qn_ref[...].reshape(rows, LANES)
    qrot = qrot_ref[...].reshape(rows, LANES)

    o_cmp, imps_t = _cmp_branch(qn, kvc_ref[0], kvc_ref[1], q_pos_r, n_cmp_blocks, impmap_ref, tq,
                                imp_transposed=True)
    q_pos_lane = qi * tq + lax.broadcasted_iota(jnp.int32, (1, tq), 1)
    selms = [_select_blocks_t(imp_t, q_pos_lane, n_sel_blocks).T.astype(BF16) for imp_t in imps_t]

    blk_per_chunk = tk // SEL_BLOCK
    n_chunks = (qi * tq + tq + tk - 1) // tk
    e_row = lax.broadcasted_iota(jnp.int32, (LANES, tk), 0)
    e_col = lax.shift_right_logical(lax.broadcasted_iota(jnp.int32, (LANES, tk), 1), 6)
    col_t = lax.broadcasted_iota(jnp.int32, (tq, tk), 1)

    def sel_body(c, carry):
        start = pl.multiple_of(c * tk, tk)
        k = sel_ref[pl.ds(start, tk), 0:LANES]
        v_ext = sel_ref[pl.ds(start, tk), LANES:3 * LANES]
        expand = jnp.where(e_row == c * blk_per_chunk + e_col, 1.0, 0.0).astype(BF16)
        causal = (c * tk + col_t) <= q_pos_t
        bias = _group_rows([jnp.where((_dot(sm, expand) > 0.5) & causal, 0.0, NEG_INF) for sm in selms], NSA_GROUP)
        return _flash_step(_dot_nt(qrot, k), bias, v_ext, *carry)

    init = (jnp.full((rows, 1), NEG_INF, F32), jnp.zeros((rows, 2 * LANES), F32))
    _, acc = lax.fori_loop(0, n_chunks, sel_body, init)
    o_sel = _flash_result(acc, LANES)

    w_start = pl.multiple_of(jnp.maximum(qi * tq + tq - wlen, 0), tq)
    k = win_ref[pl.ds(w_start, wlen), 0:LANES]
    v_ext = win_ref[pl.ds(w_start, wlen), LANES:3 * LANES]
    d = q_pos_t - (w_start + lax.broadcasted_iota(jnp.int32, (tq, wlen), 1))
    bias = _tile_rows(jnp.where((d >= 0) & (d < WINDOW), 0.0, NEG_INF), NSA_HEADS)
    _, acc = _flash_step(_dot_nt(qrot, k), bias, v_ext, *init)
    o_win = _flash_result(acc, LANES)

    o_ref[...] = _mix_heads(gates_ref[...], o_cmp, o_sel, o_win, tq)


def _importance_map(nc_rows, nc, ns, width):
    i = np.arange(nc_rows)[:, None]
    j = np.arange(width)[None, :]
    lo = np.maximum(i * CMP_STRIDE, j * SEL_BLOCK)
    hi = np.minimum(i * CMP_STRIDE + CMP_BLOCK, (j + 1) * SEL_BLOCK)
    m = np.maximum(hi - lo, 0).astype(np.float32) / CMP_BLOCK
    m = m * (i < nc) * (j < ns)
    return jnp.asarray(m, dtype=BF16)


def _nsa_prompt(P, kvc, batch, seq):
    tq = min(256, seq)
    tk = min(512, seq)
    nq = seq // tq
    n_chunk = seq // CMP_STRIDE
    nc = n_chunk - CMP_BLOCK // CMP_STRIDE + 1
    ns = seq // SEL_BLOCK
    assert ns <= LANES and seq % tk == 0 and seq % CMP_STRIDE == 0 and seq % SEL_BLOCK == 0
    wlen = min(WINDOW + tq, seq)
    impmap = _importance_map(n_chunk, nc, ns, LANES).T
    blk = lambda w: pl.BlockSpec((tq, w), lambda b, i: (b * nq + i, 0))
    qblk = pl.BlockSpec((NSA_HEADS, tq, LANES), lambda b, i: (0, b * nq + i, 0))
    per_b = lambda a: pl.BlockSpec((None,) + a.shape[1:], lambda b, i: (b,) + (0,) * (a.ndim - 1))
    selb = P['selb'].reshape(batch, seq, NSA_KROW)
    winb = P['winb'].reshape(batch, seq, NSA_KROW)
    return pl.pallas_call(
        functools.partial(_nsa_prompt_kernel, tq=tq, tk=tk, n_cmp_blocks=nc, n_sel_blocks=ns, wlen=wlen),
        grid=(batch, nq),
        in_specs=[qblk, qblk, per_b(kvc), per_b(selb), per_b(winb), blk(LANES),
                  pl.BlockSpec(impmap.shape, lambda b, i: (0, 0))],
        out_specs=blk(NSA_WIDTH),
        out_shape=jax.ShapeDtypeStruct((batch * seq, NSA_WIDTH), F32),
        compiler_params=_cparams(("parallel", "parallel")),
        name="nsa_prompt",
    )(P['qn'], P['qrot'], kvc, selb, winb, P['gates'], impmap)


def _finish_kernel(x_ref, omla_ref, onsa_ref, szm_ref, szn_ref, wo_ref, g_ref, b_ref, y_ref):
    mixed = jnp.concatenate([omla_ref[...] * szm_ref[...], onsa_ref[...] * szn_ref[...]], axis=-1)
    h = ALPHA * x_ref[...] + _dot(mixed.astype(BF16), wo_ref[...])
    mu = jnp.mean(h, axis=-1, keepdims=True)
    d = h - mu
    var = jnp.mean(d * d, axis=-1, keepdims=True)
    y_ref[...] = d * lax.rsqrt(var + LN_EPS) * g_ref[...] + b_ref[...]


def _finish(x2d, omla, onsa, szm, szn, wts, tm):
    n = x2d.shape[0]
    row = lambda w: pl.BlockSpec((tm, w), lambda i: (i, 0))
    full = lambda a: pl.BlockSpec(a.shape, lambda i: (0,) * a.ndim)
    return pl.pallas_call(
        _finish_kernel,
        grid=(n // tm,),
        in_specs=[row(D_MODEL), row(MLA_WIDTH), row(NSA_WIDTH), row(MLA_WIDTH), row(NSA_WIDTH),
                  full(wts['w_o']), full(wts['ln_g']), full(wts['ln_b'])],
        out_specs=row(D_MODEL),
        out_shape=jax.ShapeDtypeStruct((n, D_MODEL), F32),
        compiler_params=_cparams(("parallel",)),
        name="finish",
    )(x2d, omla, onsa, szm, szn, wts['w_o'], wts['ln_g'], wts['ln_b'])


def _page_copies(pt_ref, b, n_pages, make_copy):
    assert n_pages % 2 == 0

    def start(i, _):
        make_copy(pt_ref[b, 2 * i], 2 * i).start(priority=0)
        make_copy(pt_ref[b, 2 * i + 1], 2 * i + 1).start(priority=1)
        return 0
    lax.fori_loop(0, n_pages // 2, start, 0)


def _page_waits(n_pages, make_copy):
    def wait(p, _):
        make_copy(0, p).wait()
        return 0
    lax.fori_loop(0, n_pages, wait, 0)


def _mla_decode_kernel(pt_ref, q_ref, knew_ref, pool_ref, o_ref, kbuf, sem, *, n_pages, t_new, tk):
    b = pl.program_id(0)
    nb = pl.num_programs(0)
    slot = b % 2
    past = n_pages * PAGE_SIZE

    def copy(sl):
        return lambda page, p: pltpu.make_async_copy(pool_ref.at[page], kbuf.at[sl, p], sem.at[sl])

    @pl.when(b == 0)
    def _():
        _page_copies(pt_ref, 0, n_pages, copy(0))

    @pl.when(b + 1 < nb)
    def _():
        _page_copies(pt_ref, b + 1, n_pages, copy(1 - slot))

    _page_waits(n_pages, copy(slot))

    q = q_ref[...]
    rows = q.shape[0]
    m = jnp.full((rows, 1), NEG_INF, F32)
    l = jnp.zeros((rows, 1), F32)
    acc = jnp.zeros((rows, MLA_KV_LORA), F32)
    pages_per_chunk = tk // PAGE_SIZE
    for c in range(n_pages // pages_per_chunk):
        kt = jnp.concatenate([kbuf[slot, c * pages_per_chunk + i].astype(BF16) for i in range(pages_per_chunk)],
                             axis=1)
        m, l, acc = _online_update(_dot(q, kt), None, kt[0:MLA_KV_LORA], m, l, acc, v_transposed=True)
    kn = knew_ref[...]
    t_q = lax.broadcasted_iota(jnp.int32, (rows, kn.shape[0]), 0) & (t_new - 1)
    t_k = lax.broadcasted_iota(jnp.int32, (rows, kn.shape[0]), 1)
    m, l, acc = _online_update(_dot_nt(q, kn), t_k <= t_q, kn[:, 0:MLA_KV_LORA], m, l, acc)
    o_ref[...] = _finalize(l, acc)


def _mla_decode(page_table, q_b, knew_b, pool):
    nb, n_pages = page_table.shape
    rows = q_b.shape[1]
    t_new = rows // MLA_HEADS
    tk = 1024
    past = n_pages * PAGE_SIZE
    assert past % tk == 0 and (t_new & (t_new - 1)) == 0
    grid_spec = pltpu.PrefetchScalarGridSpec(
        num_scalar_prefetch=1,
        grid=(nb,),
        in_specs=[pl.BlockSpec((None, rows, MLA_ROW), lambda b, pt: (b, 0, 0)),
                  pl.BlockSpec((None,) + knew_b.shape[1:], lambda b, pt: (b, 0, 0)),
                  pl.BlockSpec(memory_space=pl.ANY)],
        out_specs=pl.BlockSpec((None, rows, MLA_KV_LORA), lambda b, pt: (b, 0, 0)),
        scratch_shapes=[pltpu.VMEM((2, n_pages, MLA_ROW, PAGE_SIZE), F32), pltpu.SemaphoreType.DMA((2,))],
    )
    return pl.pallas_call(
        functools.partial(_mla_decode_kernel, n_pages=n_pages, t_new=t_new, tk=tk),
        grid_spec=grid_spec,
        out_shape=jax.ShapeDtypeStruct((nb, rows, MLA_KV_LORA), F32),
        compiler_params=_cparams(("arbitrary",)),
        name="mla_decode",
    )(page_table, q_b, knew_b, pool)


def _uv_kernel(o_ref, wuv_ref, y_ref):
    rows = o_ref.shape[1]
    y_ref[...] = _uv_project(o_ref[...].reshape(MLA_HEADS * rows, MLA_KV_LORA), wuv_ref, rows)


def _uv(o_heads, wuv):
    rows = o_heads.shape[1]
    return pl.pallas_call(
        _uv_kernel,
        out_shape=jax.ShapeDtypeStruct((rows, MLA_WIDTH), F32),
        name="mla_value_up",
    )(o_heads, wuv)


def _nsa_dec_cmp_kernel(pt_ref, qn_ref, cnew_ref, pool_ref, wc1_ref, pe_ref, b1_ref, w2_ref, impmap_ref,
                        ocmp_ref, selm_ref, pbuf, xrows, sem, *, n_pages, t_new, t_pad, n_chunk, n_cmp_blocks,
                        n_sel_blocks):
    b = pl.program_id(0)
    nb = pl.num_programs(0)
    slot = b % 2
    past = n_pages * PAGE_SIZE

    def copy(sl):
        return lambda page, p: pltpu.make_async_copy(pool_ref.at[page], pbuf.at[sl, p], sem.at[sl])

    @pl.when(b == 0)
    def _():
        _page_copies(pt_ref, 0, n_pages, copy(0))

    @pl.when(b + 1 < nb)
    def _():
        _page_copies(pt_ref, b + 1, n_pages, copy(1 - slot))

    _page_waits(n_pages, copy(slot))

    chunks_per_page = PAGE_SIZE // CMP_STRIDE

    def relayout(p, _):
        r0 = pl.multiple_of(p * (chunks_per_page * CHUNK_PITCH), 8)
        for s in range(2):
            t = pbuf[slot, p, s * LANES:(s + 1) * LANES, :].T
            for c in range(chunks_per_page):
                xrows[s, pl.ds(r0 + c * CHUNK_PITCH, CMP_STRIDE), :] = t[c * CMP_STRIDE:(c + 1) * CMP_STRIDE]
        return 0

    lax.fori_loop(0, n_pages, relayout, 0, unroll=4)

    cnew = cnew_ref[...]
    base = n_pages * chunks_per_page * CHUNK_PITCH
    tail = xrows.shape[1] - base
    for s in range(2):
        xrows[s, pl.ds(base, tail), :] = jnp.zeros((tail, LANES), F32)
        xrows[s, pl.ds(base, t_new), :] = cnew[:, s * LANES:(s + 1) * LANES]

    kvc = []
    for s in range(2):
        row = lambda p, s=s: xrows[s, pl.ds(p, n_chunk, stride=CHUNK_PITCH), :]
        load = lambda j, row=row: jnp.concatenate([row(2 * j), row(2 * j + 1)], axis=-1)
        kvc.append(_compress_slab(load, wc1_ref, pe_ref, b1_ref, w2_ref, s, n_chunk).astype(BF16))

    rows = NSA_HEADS * t_pad
    q_pos_r = past + (lax.broadcasted_iota(jnp.int32, (rows, 1), 0) & (t_pad - 1))
    q_pos_t = past + lax.broadcasted_iota(jnp.int32, (t_pad, 1), 0)
    o_cmp, imps = _cmp_branch(qn_ref[...], kvc[0][0:n_cmp_blocks], kvc[1][0:n_cmp_blocks], q_pos_r,
                              n_cmp_blocks, impmap_ref, t_pad)
    ocmp_ref[...] = o_cmp
    selm_ref[...] = _group_rows([_select_blocks(imp, q_pos_t, n_sel_blocks) for imp in imps], NSA_GROUP)


def _nsa_decode_cmp(page_table, qn_b, cnew_b, pool, wts, t_new, t_pad):
    nb, n_pages = page_table.shape
    past = n_pages * PAGE_SIZE
    total = past + t_new
    n_chunk_true = -(-total // CMP_STRIDE)
    nc = n_chunk_true - CMP_BLOCK // CMP_STRIDE + 1
    n_chunk = -(-n_chunk_true // 8) * 8
    assert nc % LANES == 0, "compressed-block count must be lane aligned"
    ns = past // SEL_BLOCK + (-(-t_new // SEL_BLOCK))
    sel_w = -(-ns // LANES) * LANES
    impmap = _importance_map(nc, nc, ns, sel_w)
    rows = NSA_HEADS * t_pad
    full = lambda a: pl.BlockSpec(a.shape, lambda b, pt: (0,) * a.ndim)
    per_b = lambda a: pl.BlockSpec((None,) + a.shape[1:], lambda b, pt: (b,) + (0,) * (a.ndim - 1))
    grid_spec = pltpu.PrefetchScalarGridSpec(
        num_scalar_prefetch=1,
        grid=(nb,),
        in_specs=[per_b(qn_b), per_b(cnew_b), pl.BlockSpec(memory_space=pl.ANY),
                  full(wts['wc1']), full(wts['pe']), full(wts['b1']), full(wts['w2']), full(impmap)],
        out_specs=[pl.BlockSpec((None, rows, LANES), lambda b, pt: (b, 0, 0)),
                   pl.BlockSpec((None, rows, sel_w), lambda b, pt: (b, 0, 0))],
        scratch_shapes=[pltpu.VMEM((2, n_pages, NSA_KV_WIDTH, PAGE_SIZE), F32),
                        pltpu.VMEM((2, n_chunk * CHUNK_PITCH, LANES), F32),
                        pltpu.SemaphoreType.DMA((2,))],
    )
    return pl.pallas_call(
        functools.partial(_nsa_dec_cmp_kernel, n_pages=n_pages, t_new=t_new, t_pad=t_pad, n_chunk=n_chunk,
                          n_cmp_blocks=nc, n_sel_blocks=ns),
        grid_spec=grid_spec,
        out_shape=[jax.ShapeDtypeStruct((nb, rows, LANES), F32),
                   jax.ShapeDtypeStruct((nb, rows, sel_w), F32)],
        compiler_params=_cparams(("arbitrary",)),
        name="nsa_decode_compress",
    )(page_table, qn_b, cnew_b, pool, wts['wc1'], wts['pe'], wts['b1'], wts['w2'], impmap)


def _nsa_dec_sel_kernel(pt_ref, qr_ref, snew_ref, wnew_ref, selm_ref, ocmp_ref, gates_ref, wst_ref, pool_ref,
                        o_ref, kbuf, sem, *, n_pages, t_new, t_pad, tk):
    b = pl.program_id(0)
    nb = pl.num_programs(0)
    slot = b % 2
    past = n_pages * PAGE_SIZE
    rows = NSA_HEADS * t_pad

    def copy(sl):
        return lambda page, p: pltpu.make_async_copy(pool_ref.at[page], kbuf.at[sl, p], sem.at[sl])

    @pl.when(b == 0)
    def _():
        _page_copies(pt_ref, 0, n_pages, copy(0))

    @pl.when(b + 1 < nb)
    def _():
        _page_copies(pt_ref, b + 1, n_pages, copy(1 - slot))

    _page_waits(n_pages, copy(slot))

    blk_per_chunk = tk // SEL_BLOCK
    lane_lo = lax.broadcasted_iota(jnp.int32, (rows, LANES), 1) < SEL_BLOCK
    n_new = snew_ref.shape[0]
    t_q = lax.broadcasted_iota(jnp.int32, (rows, n_new), 0) & (t_pad - 1)
    t_k = lax.broadcasted_iota(jnp.int32, (rows, n_new), 1)
    new_causal = (t_k <= t_q) & (t_k < t_new)
    w_buf = wst_ref.shape[1]
    t_qw = lax.broadcasted_iota(jnp.int32, (rows, w_buf), 0) & (t_pad - 1)
    i_w = lax.broadcasted_iota(jnp.int32, (rows, w_buf), 1)
    win_mask = (t_qw + w_buf - i_w) < WINDOW
    snew = snew_ref[...]
    wnew = wnew_ref[...]
    qr = qr_ref[...]
    selm = selm_ref[...]

    def fresh():
        return jnp.full((rows, 1), NEG_INF, F32), jnp.zeros((rows, 1), F32), jnp.zeros((rows, LANES), F32)

    m, l, acc = fresh()
    pages_per_chunk = tk // PAGE_SIZE
    for c in range(past // tk):
        chunk_pages = range(c * pages_per_chunk, (c + 1) * pages_per_chunk)
        kt = jnp.concatenate([kbuf[slot, p, 0:LANES, :].astype(BF16) for p in chunk_pages], axis=1)
        vt = jnp.concatenate([kbuf[slot, p, LANES:2 * LANES, :].astype(BF16) for p in chunk_pages], axis=1)
        pieces = []
        for i in range(tk // LANES):
            jb = c * blk_per_chunk + 2 * i
            pieces.append(jnp.where(lane_lo, selm[:, jb:jb + 1], selm[:, jb + 1:jb + 2]))
        mask = jnp.concatenate(pieces, axis=-1) > 0.5
        m, l, acc = _online_update(_dot(qr, kt), mask, vt, m, l, acc, v_transposed=True)
    nb_past = past // SEL_BLOCK
    mask = (selm[:, nb_past:nb_past + 1] > 0.5) & new_causal
    m, l, acc = _online_update(_dot_nt(qr, snew[:, 0:LANES].astype(BF16)), mask,
                               snew[:, LANES:2 * LANES].astype(BF16), m, l, acc)
    o_sel = _finalize(l, acc)

    m, l, acc = fresh()
    m, l, acc = _online_update(_dot(qr, wst_ref[0:LANES, :].astype(BF16)), win_mask,
                               wst_ref[LANES:2 * LANES, :].astype(BF16), m, l, acc, v_transposed=True)
    m, l, acc = _online_update(_dot_nt(qr, wnew[:, 0:LANES].astype(BF16)), new_causal,
                               wnew[:, LANES:2 * LANES].astype(BF16), m, l, acc)
    o_win = _finalize(l, acc)

    o_ref[...] = _mix_heads(gates_ref[...], ocmp_ref[...], o_sel, o_win, t_pad)


def _nsa_decode_sel(page_table, qr_b, snew_b, wnew_b, selm, ocmp, gates_b, win_state, pool, t_new, t_pad):
    nb, n_pages = page_table.shape
    past = n_pages * PAGE_SIZE
    tk = 1024
    assert past % tk == 0
    per_b = lambda a: pl.BlockSpec((None,) + a.shape[1:], lambda b, pt: (b,) + (0,) * (a.ndim - 1))
    grid_spec = pltpu.PrefetchScalarGridSpec(
        num_scalar_prefetch=1,
        grid=(nb,),
        in_specs=[per_b(qr_b), per_b(snew_b), per_b(wnew_b), per_b(selm), per_b(ocmp), per_b(gates_b),
                  per_b(win_state), pl.BlockSpec(memory_space=pl.ANY)],
        out_specs=pl.BlockSpec((None, t_pad, NSA_WIDTH), lambda b, pt: (b, 0, 0)),
        scratch_shapes=[pltpu.VMEM((2, n_pages, NSA_KV_WIDTH, PAGE_SIZE), F32), pltpu.SemaphoreType.DMA((2,))],
    )
    return pl.pallas_call(
        functools.partial(_nsa_dec_sel_kernel, n_pages=n_pages, t_new=t_new, t_pad=t_pad, tk=tk),
        grid_spec=grid_spec,
        out_shape=jax.ShapeDtypeStruct((nb, t_pad, NSA_WIDTH), F32),
        compiler_params=_cparams(("arbitrary",)),
        name="nsa_decode_select_window",
    )(page_table, qr_b, snew_b, wnew_b, selm, ocmp, gates_b, win_state, pool)


def _prep_weights(w_in, q_norm_g, w_uq, kv_norm_g, w_uk, w_uv, cmp_pos_emb, cmp_w1, cmp_b1, cmp_w2, w_o, ln_g, ln_b):
    cuts = np.cumsum([MLA_Q_LORA, MLA_KV_LORA, MLA_ROPE_DIM, MLA_WIDTH, NSA_WIDTH, NSA_KV_WIDTH, NSA_KV_WIDTH,
                      NSA_KV_WIDTH, 3 * NSA_HEADS])[:].tolist()
    c_q, c_kv, k_r, z_mla, q_n, cmp_kv, sel_kv, win_kv, g_br, z_nsa = jnp.split(w_in, cuts, axis=1)
    pad = jnp.zeros((D_MODEL, LANES - MLA_ROPE_DIM - 3 * NSA_HEADS), w_in.dtype)
    w_in_p = jnp.concatenate([c_q, c_kv, z_mla, q_n, cmp_kv, sel_kv, win_kv, z_nsa, k_r, g_br, pad], axis=1)
    assert w_in_p.shape[1] == IN_WIDTH_P
    uq = w_uq.reshape(MLA_Q_LORA, MLA_HEADS, MLA_NOPE_DIM + MLA_ROPE_DIM)
    w_uq_p = jnp.concatenate([uq[:, :, :MLA_NOPE_DIM].reshape(MLA_Q_LORA, -1),
                              uq[:, :, MLA_NOPE_DIM:].reshape(MLA_Q_LORA, -1)], axis=1)
    w_uk_t = jnp.transpose(w_uk.reshape(MLA_KV_LORA, MLA_HEADS, MLA_NOPE_DIM), (1, 2, 0))
    w_uv_h = jnp.transpose(w_uv.reshape(MLA_KV_LORA, MLA_HEADS, MLA_V_DIM), (1, 0, 2))
    ratio = CMP_BLOCK // CMP_STRIDE
    eye = jnp.eye(NSA_KV_HEADS, dtype=w_in.dtype)
    w1r = cmp_w1.reshape(2, ratio, CMP_STRIDE, NSA_HEAD_DIM, NSA_HEAD_DIM)
    wc1 = jnp.einsum('gh,srpde->spgdrhe', eye, w1r).reshape(2, CMP_PAIRS, 2 * LANES, ratio * LANES)
    w2 = jnp.einsum('gh,sde->sgdhe', eye, cmp_w2).reshape(2, LANES, LANES)
    pe = jnp.transpose(cmp_pos_emb, (1, 0, 2))
    pe = jnp.concatenate([pe, pe], axis=-1).reshape(2, ratio, CMP_PAIRS, 2 * LANES)
    pe = jnp.transpose(pe, (0, 2, 1, 3))
    pe = jnp.pad(pe, ((0, 0), (0, 0), (0, 8 - ratio), (0, 0)))
    b1 = jnp.concatenate([cmp_b1, cmp_b1], axis=-1).reshape(2, 1, LANES)
    return dict(w_in=w_in_p.astype(BF16), q_norm_g=q_norm_g.reshape(1, -1), w_uq=w_uq_p.astype(BF16),
                w_uk=w_uk_t.astype(BF16), kv_norm_g=kv_norm_g.reshape(1, -1), w_uv=w_uv_h.astype(BF16),
                wc1=wc1.astype(BF16), w2=w2.astype(BF16), pe=pe, b1=b1,
                w_o=w_o.astype(BF16), ln_g=ln_g.reshape(1, -1), ln_b=ln_b.reshape(1, -1))


def _prompt_layer(x, wts):
    batch, seq, _ = x.shape
    x2d = x.reshape(batch * seq, D_MODEL)
    tm = min(256, seq)
    P = _project(x2d, _rope_tables(jnp.arange(seq, dtype=jnp.int32)), wts, tm, feature_major=True)
    o_mla = _mla_prompt(P['qmla'], P['mrowb'], wts['w_uv'], batch, seq)
    kvc = _compress_prompt(P['cmps'], wts, batch, seq)
    o_nsa = _nsa_prompt(P, kvc, batch, seq)
    y = _finish(x2d, o_mla, o_nsa, P['szm'], P['szn'], wts, tm)
    kvd = (2, NSA_KV_HEADS, NSA_HEAD_DIM)
    w_keep = min(WINDOW, seq)
    token_major = lambda a: jnp.swapaxes(a, 1, 2)
    return (y.reshape(batch, seq, D_MODEL),
            token_major(P['mrow']),
            token_major(P['cmp']).reshape((batch, seq) + kvd),
            token_major(P['sel']).reshape((batch, seq) + kvd),
            token_major(P['win'][:, :, seq - w_keep:]).reshape((batch, w_keep) + kvd))


def _sample_layer(x, cache_mla, cache_cmp, cache_sel, win_state, page_table, wts):
    nb, t_new, _ = x.shape
    n_pages = page_table.shape[1]
    past = n_pages * PAGE_SIZE
    n = nb * t_new
    t_pad = 8
    x2d = x.reshape(n, D_MODEL)
    pos = past + (jnp.arange(n, dtype=jnp.int32) % t_new)
    P = _project(x2d, _rope_tables(pos), wts, n)
    kvd = (2, NSA_KV_HEADS, NSA_HEAD_DIM)
    n_pool = cache_mla.shape[0]

    q_b = jnp.transpose(P['qmla'][:, :, :MLA_ROW].reshape(MLA_HEADS, nb, t_new, MLA_ROW), (1, 0, 2, 3))
    q_b = q_b.reshape(nb, -1, MLA_ROW)
    knew_b = jnp.pad(P['mrowb'][:, :MLA_ROW].reshape(nb, t_new, MLA_ROW), ((0, 0), (0, 16 - t_new), (0, 0)))
    feat_major = lambda a: jnp.swapaxes(a.reshape(a.shape[0], a.shape[1], -1), 1, 2)
    o_lat = _mla_decode(page_table, q_b, knew_b, feat_major(cache_mla))
    o_heads = jnp.transpose(o_lat.reshape(nb, MLA_HEADS, t_new, MLA_KV_LORA), (1, 0, 2, 3)).reshape(MLA_HEADS, n, -1)
    o_mla = _uv(o_heads, wts['w_uv'])

    def q_rows(q):
        q = jnp.transpose(q.reshape(NSA_HEADS, nb, t_new, LANES), (1, 0, 2, 3))
        q = jnp.pad(q, ((0, 0), (0, 0), (0, t_pad - t_new), (0, 0)))
        return q.reshape(nb, NSA_HEADS * t_pad, LANES)

    cnew_b = P['cmp'].reshape(nb, t_new, NSA_KV_WIDTH)
    ocmp, selm = _nsa_decode_cmp(page_table, q_rows(P['qn']), cnew_b, feat_major(cache_cmp), wts, t_new, t_pad)
    pad_rows = lambda a, r: jnp.pad(a.reshape(nb, t_new, -1), ((0, 0), (0, r - t_new), (0, 0)))
    o_nsa = _nsa_decode_sel(page_table, q_rows(P['qrot']), pad_rows(P['sel'], 16), pad_rows(P['win'], 16), selm,
                            ocmp, pad_rows(P['gates'], t_pad), feat_major(win_state), feat_major(cache_sel),
                            t_new, t_pad)
    o_nsa = o_nsa[:, :t_new].reshape(n, NSA_WIDTH)

    y = _finish(x2d, o_mla, o_nsa, P['szm'], P['szn'], wts, n)
    wst = win_state.reshape(nb, -1, NSA_KV_WIDTH)
    win_all = jnp.concatenate([wst, P['win'].reshape(nb, t_new, NSA_KV_WIDTH)], axis=1)[:, t_new:]
    return (y.reshape(nb, t_new, D_MODEL),
            P['mrow'].reshape(nb, t_new, MLA_ROW),
            P['cmp'].reshape((nb, t_new) + kvd),
            P['sel'].reshape((nb, t_new) + kvd),
            win_all.reshape((nb, win_all.shape[1]) + kvd))


def kernel(x_prompt, x_sample, cache_mla, cache_cmp_kv, cache_sel_kv, state_win_kv, page_table, w_in, q_norm_g, w_uq,
           kv_norm_g, w_uk, w_uv, cmp_pos_emb, cmp_w1, cmp_b1, cmp_w2, w_o, ln_g, ln_b):
    assert w_in.shape[0] == DEPTH
    wts = _prep_weights(w_in[0], q_norm_g[0], w_uq[0], kv_norm_g[0], w_uk[0], w_uv[0], cmp_pos_emb[0], cmp_w1[0],
                        cmp_b1[0], cmp_w2[0], w_o[0], ln_g[0], ln_b[0])
    yp, p_mla, p_cmp, p_sel, p_win = _prompt_layer(x_prompt, wts)
    drop_depth = lambda a: a.reshape(a.shape[1:])
    ys, s_mla, s_cmp, s_sel, s_win = _sample_layer(x_sample, drop_depth(cache_mla), drop_depth(cache_cmp_kv),
                                                   drop_depth(cache_sel_kv), drop_depth(state_win_kv), page_table, wts)
    add_depth = lambda a: a[None]
    return (yp, ys) + tuple(add_depth(a) for a in (p_mla, p_cmp, p_sel, p_win, s_mla, s_cmp, s_sel, s_win))
```

```python
import functools

import numpy as np
import jax
import jax.numpy as jnp
from jax import lax
from jax.experimental import pallas as pl
from jax.experimental.pallas import tpu as pltpu

F32 = jnp.float32
BF16 = jnp.bfloat16

D_MODEL = 1024
PAGE_SIZE = 128
MLA_HEADS = 8
MLA_V_DIM = 64
MLA_NOPE_DIM = 64
MLA_ROPE_DIM = 32
MLA_Q_LORA = 256
MLA_KV_LORA = 128
MLA_WIDTH = MLA_HEADS * MLA_V_DIM
MLA_ROW = MLA_KV_LORA + MLA_ROPE_DIM
NSA_HEADS = 8
NSA_KV_HEADS = 2
NSA_HEAD_DIM = 64
NSA_GROUP = NSA_HEADS // NSA_KV_HEADS
NSA_WIDTH = NSA_HEADS * NSA_HEAD_DIM
NSA_KV_WIDTH = 2 * NSA_KV_HEADS * NSA_HEAD_DIM
CMP_BLOCK = 32
CMP_STRIDE = 16
SEL_BLOCK = 64
SEL_TOP_N = 16
WINDOW = 512
Q_BLOCK = 128
ROPE_THETA = 10000.0
RMS_EPS = 1e-6
LN_EPS = 1e-5
NEG_INF = -1e30
MLA_SCALE = (MLA_NOPE_DIM + MLA_ROPE_DIM) ** -0.5
NSA_SCALE = NSA_HEAD_DIM ** -0.5
DEPTH = 1
ALPHA = (2 * DEPTH) ** 0.25

LANES = 128
LOG2E = 1.4426950408889634
MLA_KROW = 2 * LANES
MLA_ONES_LANE = MLA_ROW
NSA_KROW = 3 * LANES

SEG_CQ = (0, 256)
SEG_CKV = (256, 128)
SEG_ZMLA = (384, 512)
SEG_QN = (896, 512)
SEG_CMP = (1408, 256)
SEG_SEL = (1664, 256)
SEG_WIN = (1920, 256)
SEG_ZNSA = (2176, 512)
SEG_MISC = (2688, 128)
IN_WIDTH_P = 2816
MISC_GATE0 = MLA_ROPE_DIM

VMEM_LIMIT = 48 * 1024 * 1024

def _cparams(sem, flags=None):
    return pltpu.CompilerParams(dimension_semantics=sem, vmem_limit_bytes=VMEM_LIMIT, flags=flags)


def _sigmoid(x):
    return 1.0 / (1.0 + jnp.exp(-x))


def _silu(x):
    return x * _sigmoid(x)


def _rope_slab(x, c, sa, sb, half):
    return x * c + pltpu.roll(x, LANES - half, 1) * sa + pltpu.roll(x, half, 1) * sb


def _dot_nt(a, b):
    return lax.dot_general(a, b, (((1,), (1,)), ((), ())), preferred_element_type=F32)


def _dot(a, b):
    return jnp.dot(a, b, preferred_element_type=F32)


def _masked_softmax(s, mask):
    sm = jnp.where(mask, s, NEG_INF)
    m = jnp.max(sm, axis=-1, keepdims=True)
    e = jnp.exp2(sm - m)
    p = e / jnp.sum(e, axis=-1, keepdims=True)
    return jnp.where(mask, p, 0.0)


def _online_update(s, mask, v, m, l, acc, v_transposed=False):
    sm = s if mask is None else jnp.where(mask, s, NEG_INF)
    m_new = jnp.maximum(m, jnp.max(sm, axis=-1, keepdims=True))
    alpha = jnp.exp2(m - m_new)
    p = jnp.exp2(sm - m_new)
    if mask is not None:
        p = jnp.where(mask, p, 0.0)
    l_new = alpha * l + jnp.sum(p, axis=-1, keepdims=True)
    pv = _dot_nt(p.astype(BF16), v) if v_transposed else _dot(p.astype(BF16), v)
    return m_new, l_new, alpha * acc + pv


def _finalize(l, acc):
    ok = l > 0.0
    return jnp.where(ok, acc / jnp.where(ok, l, 1.0), 0.0)


def _flash_step(s, bias, v_ext, m, acc):
    sm = s if bias is None else s + bias
    m_new = jnp.maximum(m, jnp.max(sm, axis=-1, keepdims=True))
    p = jnp.exp2(sm - m_new)
    return m_new, jnp.exp2(m - m_new) * acc + _dot(p.astype(BF16), v_ext)


def _flash_result(acc, ones_lane):
    return acc[:, 0:LANES] * (1.0 / acc[:, ones_lane:ones_lane + 1])


def _tile_rows(x, reps):
    return jnp.concatenate([x] * reps, axis=0)


def _select_blocks(imp, q_pos, n_sel_blocks):
    rows, width = imp.shape
    j = lax.broadcasted_iota(jnp.int32, (rows, width), 1)
    cur = lax.shift_right_logical(q_pos, 6)
    avail = (j * SEL_BLOCK <= q_pos) & (j < n_sel_blocks)
    forced = (j == 0) | (j == cur) | (j == cur - 1)
    val = jnp.where(avail, jnp.where(forced, jnp.inf, imp), -jnp.inf)
    rank = jnp.zeros((rows, width), F32)
    for jp in range(n_sel_blocks):
        col = val[:, jp:jp + 1]
        ahead = (col > val) | ((col == val) & (j > jp))
        rank = rank + jnp.where(ahead, 1.0, 0.0)
    keep = (rank < float(min(SEL_TOP_N, n_sel_blocks))) & avail
    return jnp.where(keep, 1.0, 0.0)


def _select_blocks_t(imp_t, q_pos, n_sel_blocks):
    n_rows, width = imp_t.shape
    j = lax.broadcasted_iota(jnp.int32, (n_rows, width), 0)
    cur = lax.shift_right_logical(q_pos, 6)
    avail = (j * SEL_BLOCK <= q_pos) & (j < n_sel_blocks)
    forced = (j == 0) | (j == cur) | (j == cur - 1)
    val = jnp.where(avail, jnp.where(forced, jnp.inf, imp_t), -jnp.inf)
    n_pieces = -(-n_sel_blocks // 8)
    pieces = [val[8 * v:8 * v + 8] for v in range(n_pieces)]
    jrow = lax.broadcasted_iota(jnp.int32, (8, width), 0)
    ranks = [jnp.zeros((8, width), F32) for _ in range(n_pieces)]
    for jp in range(n_sel_blocks):
        cand = jnp.broadcast_to(val[jp:jp + 1], (8, width))
        for v in range(n_pieces):
            if 8 * v > jp:
                ahead = cand >= pieces[v]
            elif 8 * v + 7 < jp:
                ahead = cand > pieces[v]
            else:
                ahead = (cand > pieces[v]) | ((cand == pieces[v]) & (jrow > jp - 8 * v))
            ranks[v] = ranks[v] + jnp.where(ahead, 1.0, 0.0)
    if n_rows > 8 * n_pieces:
        ranks.append(jnp.zeros((n_rows - 8 * n_pieces, width), F32))
    rank = jnp.concatenate(ranks, axis=0)
    keep = (rank < float(min(SEL_TOP_N, n_sel_blocks))) & avail
    return jnp.where(keep, 1.0, 0.0)


def _proj_kernel(x_ref, w_ref, qg_ref, wuq_ref, wuk_ref, kvg_ref, tab_ref,
                 qmla_ref, mrow_ref, mrowb_ref, szm_ref, qn_ref, qrot_ref, cmp_ref, cmps_ref,
                 sel_ref, selb_ref, win_ref, winb_ref, gates_ref, szn_ref, *, feature_major):
    xb = x_ref[...].astype(BF16)

    def seg(s):
        return _dot(xb, w_ref[:, s[0]:s[0] + s[1]])

    cn, san, sbn = tab_ref[0], tab_ref[1], tab_ref[2]
    cm, sam, sbm = tab_ref[3], tab_ref[4], tab_ref[5]

    cq = seg(SEG_CQ)
    r = cq * lax.rsqrt(jnp.mean(cq * cq, axis=-1, keepdims=True) + RMS_EPS) * qg_ref[...]
    q = _dot(r.astype(BF16), wuq_ref[...])
    nope_w = MLA_HEADS * MLA_NOPE_DIM
    lane = lax.broadcasted_iota(jnp.int32, (q.shape[0], LANES), 1)
    for h in range(MLA_HEADS):
        ql = _dot(q[:, h * MLA_NOPE_DIM:(h + 1) * MLA_NOPE_DIM].astype(BF16), wuk_ref[h])
        qmla_ref[h, :, 0:LANES] = (ql * (MLA_SCALE * LOG2E)).astype(BF16)
    heads_per_slab = LANES // MLA_ROPE_DIM
    for jj in range(MLA_HEADS // heads_per_slab):
        qr = _rope_slab(q[:, nope_w + jj * LANES: nope_w + (jj + 1) * LANES], cm, sam, sbm, MLA_ROPE_DIM // 2)
        qr = qr * (MLA_SCALE * LOG2E)
        for hh in range(heads_per_slab):
            front = qr if hh == 0 else pltpu.roll(qr, LANES - hh * MLA_ROPE_DIM, 1)
            qmla_ref[jj * heads_per_slab + hh, :, LANES:2 * LANES] = (
                jnp.where(lane < MLA_ROPE_DIM, front, 0.0).astype(BF16))

    ckv = seg(SEG_CKV)
    lat = ckv * lax.rsqrt(jnp.mean(ckv * ckv, axis=-1, keepdims=True) + RMS_EPS) * kvg_ref[...]
    misc = seg(SEG_MISC)
    kr = _rope_slab(misc, cm, sam, sbm, MLA_ROPE_DIM // 2)
    if feature_major:
        mrow_ref[0:MLA_KV_LORA, :] = lat.T
        mrow_ref[MLA_KV_LORA:MLA_ROW, :] = kr.T[0:MLA_ROPE_DIM]
    else:
        mrow_ref[:, 0:MLA_KV_LORA] = lat
        mrow_ref[:, MLA_KV_LORA:MLA_ROW] = kr[:, 0:MLA_ROPE_DIM]
    mrowb_ref[:, 0:LANES] = lat.astype(BF16)
    ones_col = jnp.where(lane == MLA_ONES_LANE - LANES, 1.0, 0.0)
    mrowb_ref[:, LANES:2 * LANES] = jnp.where(lane < MLA_ROPE_DIM, kr, ones_col).astype(BF16)
    gates_ref[...] = _sigmoid(misc)

    szm_ref[...] = _silu(seg(SEG_ZMLA))
    szn_ref[...] = _silu(seg(SEG_ZNSA))

    qn = seg(SEG_QN)
    for jj in range(NSA_WIDTH // LANES):
        raw = qn[:, jj * LANES:(jj + 1) * LANES] * (NSA_SCALE * LOG2E)
        rot = _rope_slab(qn[:, jj * LANES:(jj + 1) * LANES], cn, san, sbn, NSA_HEAD_DIM // 2) * (NSA_SCALE * LOG2E)
        for src, dst in ((raw, qn_ref), (rot, qrot_ref)):
            swapped = pltpu.roll(src, NSA_HEAD_DIM, 1)
            for half in range(2):
                hd = 2 * jj + half
                g = hd // NSA_GROUP
                keep = (lane >= g * NSA_HEAD_DIM) & (lane < (g + 1) * NSA_HEAD_DIM)
                dst[hd] = jnp.where(keep, src if half == g else swapped, 0.0).astype(BF16)

    def put_rows(dst, first, second):
        if feature_major:
            dst[0:LANES, :] = first.T
            dst[LANES:2 * LANES, :] = second.T
        else:
            dst[:, 0:LANES] = first
            dst[:, LANES:2 * LANES] = second

    cmpv = seg(SEG_CMP)
    put_rows(cmp_ref, cmpv[:, 0:LANES], cmpv[:, LANES:2 * LANES])
    cmps_ref[0] = cmpv[:, 0:LANES]
    cmps_ref[1] = cmpv[:, LANES:2 * LANES]

    for src, dst, dstb in ((SEG_SEL, sel_ref, selb_ref), (SEG_WIN, win_ref, winb_ref)):
        kv = seg(src)
        k = _rope_slab(kv[:, 0:LANES], cn, san, sbn, NSA_HEAD_DIM // 2)
        v = kv[:, LANES:2 * LANES]
        put_rows(dst, k, v)
        dstb[:, 0:LANES] = k.astype(BF16)
        dstb[:, LANES:2 * LANES] = v.astype(BF16)
        dstb[:, 2 * LANES:3 * LANES] = jnp.where(lane == 0, 1.0, 0.0).astype(BF16)


def _rope_tables(pos):
    def tab(d):
        inv = 1.0 / (ROPE_THETA ** (jnp.arange(0, d, 2, dtype=F32) / d))
        ang = pos.astype(F32)[:, None] * inv[None, :]
        cos, sin = jnp.cos(ang), jnp.sin(ang)
        zero = jnp.zeros_like(sin)
        reps = LANES // d
        return (jnp.tile(jnp.concatenate([cos, cos], -1), (1, reps)),
                jnp.tile(jnp.concatenate([-sin, zero], -1), (1, reps)),
                jnp.tile(jnp.concatenate([zero, sin], -1), (1, reps)))
    return jnp.stack(tab(NSA_HEAD_DIM) + tab(MLA_ROPE_DIM))


def _project(x2d, tabs, wts, tm, feature_major=False):
    n = x2d.shape[0]
    p_rows = tabs.shape[1]
    assert n % tm == 0 and p_rows % tm == 0
    nt = p_rows // tm
    row = lambda w: pl.BlockSpec((tm, w), lambda i: (i, 0))
    full = lambda a: pl.BlockSpec(a.shape, lambda i: (0,) * a.ndim)
    if feature_major:
        out_rows = lambda w: ((n // p_rows, w, p_rows), F32, pl.BlockSpec((None, w, tm), lambda i: (i // nt, 0, i % nt)))
    else:
        out_rows = lambda w: ((n, w), F32, row(w))
    out_shapes = dict(
        qmla=((MLA_HEADS, n, MLA_KROW), BF16, pl.BlockSpec((MLA_HEADS, tm, MLA_KROW), lambda i: (0, i, 0))),
        mrow=out_rows(MLA_ROW),
        mrowb=((n, MLA_KROW), BF16, row(MLA_KROW)),
        szm=((n, MLA_WIDTH), F32, row(MLA_WIDTH)),
        qn=((NSA_HEADS, n, LANES), BF16, pl.BlockSpec((NSA_HEADS, tm, LANES), lambda i: (0, i, 0))),
        qrot=((NSA_HEADS, n, LANES), BF16, pl.BlockSpec((NSA_HEADS, tm, LANES), lambda i: (0, i, 0))),
        cmp=out_rows(NSA_KV_WIDTH),
        cmps=((2, n, LANES), F32, pl.BlockSpec((2, tm, LANES), lambda i: (0, i, 0))),
        sel=out_rows(NSA_KV_WIDTH),
        selb=((n, NSA_KROW), BF16, row(NSA_KROW)),
        win=out_rows(NSA_KV_WIDTH),
        winb=((n, NSA_KROW), BF16, row(NSA_KROW)),
        gates=((n, LANES), F32, row(LANES)),
        szn=((n, NSA_WIDTH), F32, row(NSA_WIDTH)),
    )
    names = list(out_shapes)
    outs = pl.pallas_call(
        functools.partial(_proj_kernel, feature_major=feature_major),
        grid=(n // tm,),
        in_specs=[row(D_MODEL), full(wts['w_in']), full(wts['q_norm_g']), full(wts['w_uq']), full(wts['w_uk']),
                  full(wts['kv_norm_g']), pl.BlockSpec((6, tm, LANES), lambda i: (0, i % nt, 0))],
        out_specs=[out_shapes[k][2] for k in names],
        out_shape=[jax.ShapeDtypeStruct(out_shapes[k][0], out_shapes[k][1]) for k in names],
        compiler_params=_cparams(("parallel",)),
        name="projection",
    )(x2d, wts['w_in'], wts['q_norm_g'], wts['w_uq'], wts['w_uk'], wts['kv_norm_g'], tabs)
    return dict(zip(names, outs))


def _uv_project(o, wuv_ref, rows):
    return jnp.concatenate(
        [_dot(o[h * rows:(h + 1) * rows].astype(BF16), wuv_ref[h]) for h in range(MLA_HEADS)], axis=-1)


def _flash_pipeline(n_chunks, prepare, scores, consume, s_a, s_b):
    scores(0, s_a)

    def body(i, _):
        c0 = 2 * i
        aux = prepare(c0)
        scores(c0 + 1, s_b)
        consume(c0, s_a, aux)

        @pl.when(c0 + 1 < n_chunks)
        def _():
            aux = prepare(c0 + 1)
            scores(c0 + 2, s_a)
            consume(c0 + 1, s_b, aux)
        return 0

    lax.fori_loop(0, (n_chunks + 1) // 2, body, 0)


def _mla_prompt_kernel(q_ref, k_ref, wuv_ref, o_ref, s_a, s_b, m_ref, acc_ref, *, tq, tk):
    qi = pl.program_id(1)
    rows = MLA_HEADS * tq
    q = q_ref[...].reshape(rows, MLA_KROW)
    n_chunks = (qi * tq) // tk + 1
    col = lax.broadcasted_iota(jnp.int32, (tq, tk), 1)
    q_pos = qi * tq + lax.broadcasted_iota(jnp.int32, (tq, tk), 0)
    m_ref[...] = jnp.full(m_ref.shape, NEG_INF, F32)
    acc_ref[...] = jnp.zeros(acc_ref.shape, F32)

    def chunk(c):
        return k_ref[pl.ds(pl.multiple_of(c * tk, tk), tk), :]

    def scores(c, dst):
        dst[...] = _dot_nt(q, chunk(jnp.minimum(c, n_chunks - 1)))

    def prepare(c):
        return _tile_rows(jnp.where(c * tk + col <= q_pos, 0.0, NEG_INF), MLA_HEADS)

    def consume(c, src, bias):
        m_ref[...], acc_ref[...] = _flash_step(src[...], bias, chunk(c), m_ref[...], acc_ref[...])

    _flash_pipeline(n_chunks, prepare, scores, consume, s_a, s_b)
    o_ref[...] = _uv_project(_flash_result(acc_ref[...], MLA_ONES_LANE), wuv_ref, tq)


def _mla_prompt(qmla, mrowb, wuv, batch, seq):
    tq = min(256, seq)
    tk = min(512, seq)
    nq = seq // tq
    assert tk % tq == 0 and seq % tk == 0
    return pl.pallas_call(
        functools.partial(_mla_prompt_kernel, tq=tq, tk=tk),
        grid=(batch, nq),
        in_specs=[pl.BlockSpec((MLA_HEADS, tq, MLA_KROW), lambda b, i: (0, b * nq + i, 0)),
                  pl.BlockSpec((None, seq, MLA_KROW), lambda b, i: (b, 0, 0)),
                  pl.BlockSpec(wuv.shape, lambda b, i: (0, 0, 0))],
        out_specs=pl.BlockSpec((tq, MLA_WIDTH), lambda b, i: (b * nq + i, 0)),
        out_shape=jax.ShapeDtypeStruct((batch * seq, MLA_WIDTH), F32),
        scratch_shapes=[pltpu.VMEM((MLA_HEADS * tq, tk), F32), pltpu.VMEM((MLA_HEADS * tq, tk), F32),
                        pltpu.VMEM((MLA_HEADS * tq, 1), F32), pltpu.VMEM((MLA_HEADS * tq, MLA_KROW), F32)],
        compiler_params=_cparams(("parallel", "parallel")),
        name="mla_prompt",
    )(qmla, mrowb.reshape(batch, seq, MLA_KROW), wuv)


CMP_PAIRS = CMP_STRIDE // 2
CHUNK_PITCH = 24


def _compress_slab(load_pair, wc1_ref, pe_ref, b1_ref, w2_ref, s, n_chunk):
    proj = jnp.zeros((n_chunk, 2 * LANES), F32)
    pe = jnp.zeros((8, 2 * LANES), F32)
    for j in range(CMP_PAIRS):
        w = wc1_ref[s, j]
        proj = proj + _dot(load_pair(j).astype(BF16), w)
        pe = pe + _dot(pe_ref[s, j].astype(BF16), w)
    bias = b1_ref[s] + pe[0:1, 0:LANES] + pe[1:2, LANES:2 * LANES]
    hid = proj[:, 0:LANES] + pltpu.roll(proj[:, LANES:2 * LANES], n_chunk - 1, 0) + bias
    return _dot(_silu(hid).astype(BF16), w2_ref[s])


def _compress_prompt_kernel(x_ref, wc1_ref, pe_ref, b1_ref, w2_ref, o_ref, *, n_chunk):
    for s in range(2):
        row = lambda p, s=s: x_ref[s, pl.ds(p, n_chunk, stride=CMP_STRIDE), :]
        load = lambda j, row=row: jnp.concatenate([row(2 * j), row(2 * j + 1)], axis=-1)
        o_ref[s] = _compress_slab(load, wc1_ref, pe_ref, b1_ref, w2_ref, s, n_chunk).astype(BF16)


def _compress_prompt(cmps, wts, batch, seq):
    n_chunk = seq // CMP_STRIDE
    full = lambda a: pl.BlockSpec(a.shape, lambda b: (0,) * a.ndim)
    return pl.pallas_call(
        functools.partial(_compress_prompt_kernel, n_chunk=n_chunk),
        grid=(batch,),
        in_specs=[pl.BlockSpec((2, seq, LANES), lambda b: (0, b, 0)),
                  full(wts['wc1']), full(wts['pe']), full(wts['b1']), full(wts['w2'])],
        out_specs=pl.BlockSpec((None, 2, n_chunk, LANES), lambda b: (b, 0, 0, 0)),
        out_shape=jax.ShapeDtypeStruct((batch, 2, n_chunk, LANES), BF16),
        compiler_params=_cparams(("parallel",)),
        name="compress_prompt",
    )(cmps, wts['wc1'], wts['pe'], wts['b1'], wts['w2'])


def _cmp_branch(q, kc, vc, q_pos_rows, n_cmp_blocks, impmap_ref, rows_t, imp_transposed=False):
    s = _dot_nt(q, kc)
    n = lax.broadcasted_iota(jnp.int32, (rows_t, s.shape[1]), 1)
    visible = (n * CMP_STRIDE + (CMP_BLOCK - 1) <= q_pos_rows[0:rows_t]) & (n < n_cmp_blocks)
    sm = s + _tile_rows(jnp.where(visible, 0.0, NEG_INF), NSA_HEADS)
    e = jnp.exp2(sm - jnp.max(sm, axis=-1, keepdims=True))
    any_visible = (q_pos_rows >= CMP_BLOCK - 1) & (n_cmp_blocks > 0)
    p = e * jnp.where(any_visible, 1.0 / jnp.sum(e, axis=-1, keepdims=True), 0.0)
    o_cmp = _dot(p.astype(BF16), vc)
    imps = []
    for g in range(NSA_KV_HEADS):
        base = g * NSA_GROUP * rows_t
        psum = p[base:base + rows_t]
        for hh in range(1, NSA_GROUP):
            psum = psum + p[base + hh * rows_t: base + (hh + 1) * rows_t]
        hi = psum.astype(BF16)
        lo = (psum - hi.astype(F32)).astype(BF16)
        if imp_transposed:
            imps.append(_dot_nt(impmap_ref[...], hi) + _dot_nt(impmap_ref[...], lo))
        else:
            imps.append(_dot(hi, impmap_ref[...]) + _dot(lo, impmap_ref[...]))
    return o_cmp, imps


def _mix_heads(gates, o_cmp, o_sel, o_win, rows_t):
    outs = []
    for hd in range(NSA_HEADS):
        g = hd // NSA_GROUP
        rs = slice(hd * rows_t, (hd + 1) * rows_t)
        ls = slice(g * NSA_HEAD_DIM, (g + 1) * NSA_HEAD_DIM)
        gcol = lambda br: gates[:, MISC_GATE0 + br * NSA_HEADS + hd: MISC_GATE0 + br * NSA_HEADS + hd + 1]
        outs.append(gcol(0) * o_cmp[rs, ls] + gcol(1) * o_sel[rs, ls] + gcol(2) * o_win[rs, ls])
    return jnp.concatenate(outs, axis=-1)


def _group_rows(per_group, reps):
    return jnp.concatenate([per_group[g] for g in range(NSA_KV_HEADS) for _ in range(reps)], axis=0)


def _nsa_prompt_kernel(qn_ref, qrot_ref, kvc_ref, sel_ref, win_ref, gates_ref, impmap_ref, o_ref,
                       s_a, s_b, m_ref, acc_ref, *, tq, tk, n_cmp_blocks, n_sel_blocks, wlen):
    qi = pl.program_id(1)
    rows = NSA_HEADS * tq
    q_pos_t = qi * tq + lax.broadcasted_iota(jnp.int32, (tq, 1), 0)
    q_pos_r = qi * tq + (lax.broadcasted_iota(jnp.int32, (rows, 1), 0) & (tq - 1))
    qn = qn_ref[...].reshape(rows, LANES)
    qrot = qrot_ref[...].reshape(rows, LANES)
    init = (jnp.full((rows, 1), NEG_INF, F32), jnp.zeros((rows, 2 * LANES), F32))

    w_start = pl.multiple_of(jnp.maximum(qi * tq + tq - wlen, 0), tq)
    d = q_pos_t - (w_start + lax.broadcasted_iota(jnp.int32, (tq, wlen), 1))
    bias = _tile_rows(jnp.where((d >= 0) & (d < WINDOW), 0.0, NEG_INF), NSA_HEADS)
    _, acc = _flash_step(_dot_nt(qrot, win_ref[pl.ds(w_start, wlen), 0:LANES]), bias,
                         win_ref[pl.ds(w_start, wlen), LANES:3 * LANES], *init)
    o_win = _flash_result(acc, LANES)

    o_cmp, imps_t = _cmp_branch(qn, kvc_ref[0], kvc_ref[1], q_pos_r, n_cmp_blocks, impmap_ref, tq,
                                imp_transposed=True)
    q_pos_lane = qi * tq + lax.broadcasted_iota(jnp.int32, (1, tq), 1)
    selms = [_select_blocks_t(imp_t, q_pos_lane, n_sel_blocks).T.astype(BF16) for imp_t in imps_t]

    blk_per_chunk = tk // SEL_BLOCK
    n_chunks = (qi * tq + tq + tk - 1) // tk
    e_row = lax.broadcasted_iota(jnp.int32, (LANES, tk), 0)
    e_col = lax.shift_right_logical(lax.broadcasted_iota(jnp.int32, (LANES, tk), 1), 6)
    col_t = lax.broadcasted_iota(jnp.int32, (tq, tk), 1)

    def sel_scores(c, dst):
        start = pl.multiple_of(jnp.minimum(c, n_chunks - 1) * tk, tk)
        dst[...] = _dot_nt(qrot, sel_ref[pl.ds(start, tk), 0:LANES])

    def sel_prepare(c):
        expand = jnp.where(e_row == c * blk_per_chunk + e_col, 1.0, 0.0).astype(BF16)
        causal = (c * tk + col_t) <= q_pos_t
        return _group_rows([jnp.where((_dot(sm, expand) > 0.5) & causal, 0.0, NEG_INF) for sm in selms], NSA_GROUP)

    def sel_consume(c, src, bias):
        v_ext = sel_ref[pl.ds(pl.multiple_of(c * tk, tk), tk), LANES:3 * LANES]
        m_ref[...], acc_ref[...] = _flash_step(src[...], bias, v_ext, m_ref[...], acc_ref[...])

    m_ref[...], acc_ref[...] = init
    _flash_pipeline(n_chunks, sel_prepare, sel_scores, sel_consume, s_a, s_b)
    o_sel = _flash_result(acc_ref[...], LANES)

    o_ref[...] = _mix_heads(gates_ref[...], o_cmp, o_sel, o_win, tq)


def _importance_map(nc_rows, nc, ns, width):
    i = np.arange(nc_rows)[:, None]
    j = np.arange(width)[None, :]
    lo = np.maximum(i * CMP_STRIDE, j * SEL_BLOCK)
    hi = np.minimum(i * CMP_STRIDE + CMP_BLOCK, (j + 1) * SEL_BLOCK)
    m = np.maximum(hi - lo, 0).astype(np.float32) / CMP_BLOCK
    m = m * (i < nc) * (j < ns)
    return jnp.asarray(m, dtype=BF16)


def _nsa_prompt(P, kvc, batch, seq):
    tq = min(256, seq)
    tk = min(512, seq)
    nq = seq // tq
    n_chunk = seq // CMP_STRIDE
    nc = n_chunk - CMP_BLOCK // CMP_STRIDE + 1
    ns = seq // SEL_BLOCK
    assert ns <= LANES and seq % tk == 0 and seq % CMP_STRIDE == 0 and seq % SEL_BLOCK == 0
    wlen = min(WINDOW + tq, seq)
    impmap = _importance_map(n_chunk, nc, ns, LANES).T
    blk = lambda w: pl.BlockSpec((tq, w), lambda b, i: (b * nq + i, 0))
    qblk = pl.BlockSpec((NSA_HEADS, tq, LANES), lambda b, i: (0, b * nq + i, 0))
    per_b = lambda a: pl.BlockSpec((None,) + a.shape[1:], lambda b, i: (b,) + (0,) * (a.ndim - 1))
    selb = P['selb'].reshape(batch, seq, NSA_KROW)
    winb = P['winb'].reshape(batch, seq, NSA_KROW)
    return pl.pallas_call(
        functools.partial(_nsa_prompt_kernel, tq=tq, tk=tk, n_cmp_blocks=nc, n_sel_blocks=ns, wlen=wlen),
        grid=(batch, nq),
        in_specs=[qblk, qblk, per_b(kvc), per_b(selb), per_b(winb), blk(LANES),
                  pl.BlockSpec(impmap.shape, lambda b, i: (0, 0))],
        out_specs=blk(NSA_WIDTH),
        out_shape=jax.ShapeDtypeStruct((batch * seq, NSA_WIDTH), F32),
        scratch_shapes=[pltpu.VMEM((NSA_HEADS * tq, tk), F32), pltpu.VMEM((NSA_HEADS * tq, tk), F32),
                        pltpu.VMEM((NSA_HEADS * tq, 1), F32), pltpu.VMEM((NSA_HEADS * tq, 2 * LANES), F32)],
        compiler_params=_cparams(("parallel", "parallel")),
        name="nsa_prompt",
    )(P['qn'], P['qrot'], kvc, selb, winb, P['gates'], impmap)


def _finish_kernel(x_ref, omla_ref, onsa_ref, szm_ref, szn_ref, wo_ref, g_ref, b_ref, y_ref):
    mixed = jnp.concatenate([omla_ref[...] * szm_ref[...], onsa_ref[...] * szn_ref[...]], axis=-1)
    h = ALPHA * x_ref[...] + _dot(mixed.astype(BF16), wo_ref[...])
    mu = jnp.mean(h, axis=-1, keepdims=True)
    d = h - mu
    var = jnp.mean(d * d, axis=-1, keepdims=True)
    y_ref[...] = d * lax.rsqrt(var + LN_EPS) * g_ref[...] + b_ref[...]


def _finish(x2d, omla, onsa, szm, szn, wts, tm):
    n = x2d.shape[0]
    row = lambda w: pl.BlockSpec((tm, w), lambda i: (i, 0))
    full = lambda a: pl.BlockSpec(a.shape, lambda i: (0,) * a.ndim)
    return pl.pallas_call(
        _finish_kernel,
        grid=(n // tm,),
        in_specs=[row(D_MODEL), row(MLA_WIDTH), row(NSA_WIDTH), row(MLA_WIDTH), row(NSA_WIDTH),
                  full(wts['w_o']), full(wts['ln_g']), full(wts['ln_b'])],
        out_specs=row(D_MODEL),
        out_shape=jax.ShapeDtypeStruct((n, D_MODEL), F32),
        compiler_params=_cparams(("parallel",)),
        name="finish",
    )(x2d, omla, onsa, szm, szn, wts['w_o'], wts['ln_g'], wts['ln_b'])


def _page_copies(pt_ref, b, n_pages, make_copy):
    assert n_pages % 2 == 0

    def start(i, _):
        make_copy(pt_ref[b, 2 * i], 2 * i).start(priority=0)
        make_copy(pt_ref[b, 2 * i + 1], 2 * i + 1).start(priority=1)
        return 0
    lax.fori_loop(0, n_pages // 2, start, 0, unroll=4)


def _page_wait_all(pool_ref, slot_buf, sem):
    n_pages = slot_buf.shape[0]
    pltpu.make_async_copy(pool_ref.at[pl.ds(0, n_pages)], slot_buf, sem).wait()


def _mla_decode_kernel(pt_ref, q_ref, knew_ref, pool_ref, o_ref, kbuf, sem, *, n_pages, t_new, tk):
    b = pl.program_id(0)
    nb = pl.num_programs(0)
    slot = b % 2
    past = n_pages * PAGE_SIZE

    def copy(sl):
        return lambda page, p: pltpu.make_async_copy(pool_ref.at[page], kbuf.at[sl, p], sem.at[sl])

    @pl.when(b == 0)
    def _():
        _page_copies(pt_ref, 0, n_pages, copy(0))

    @pl.when(b + 1 < nb)
    def _():
        _page_copies(pt_ref, b + 1, n_pages, copy(1 - slot))

    _page_wait_all(pool_ref, kbuf.at[slot], sem.at[slot])

    q = q_ref[...]
    rows = q.shape[0]
    m = jnp.full((rows, 1), NEG_INF, F32)
    l = jnp.zeros((rows, 1), F32)
    acc = jnp.zeros((rows, MLA_KV_LORA), F32)
    pages_per_chunk = tk // PAGE_SIZE
    for c in range(n_pages // pages_per_chunk):
        kt = jnp.concatenate([kbuf[slot, c * pages_per_chunk + i].astype(BF16) for i in range(pages_per_chunk)],
                             axis=1)
        m, l, acc = _online_update(_dot(q, kt), None, kt[0:MLA_KV_LORA], m, l, acc, v_transposed=True)
    kn = knew_ref[...]
    t_q = lax.broadcasted_iota(jnp.int32, (rows, kn.shape[0]), 0) & (t_new - 1)
    t_k = lax.broadcasted_iota(jnp.int32, (rows, kn.shape[0]), 1)
    m, l, acc = _online_update(_dot_nt(q, kn), t_k <= t_q, kn[:, 0:MLA_KV_LORA], m, l, acc)
    o_ref[...] = _finalize(l, acc)


def _mla_decode(page_table, q_b, knew_b, pool):
    nb, n_pages = page_table.shape
    rows = q_b.shape[1]
    t_new = rows // MLA_HEADS
    tk = 1024
    past = n_pages * PAGE_SIZE
    assert past % tk == 0 and (t_new & (t_new - 1)) == 0
    grid_spec = pltpu.PrefetchScalarGridSpec(
        num_scalar_prefetch=1,
        grid=(nb,),
        in_specs=[pl.BlockSpec((None, rows, MLA_ROW), lambda b, pt: (b, 0, 0)),
                  pl.BlockSpec((None,) + knew_b.shape[1:], lambda b, pt: (b, 0, 0)),
                  pl.BlockSpec(memory_space=pl.ANY)],
        out_specs=pl.BlockSpec((None, rows, MLA_KV_LORA), lambda b, pt: (b, 0, 0)),
        scratch_shapes=[pltpu.VMEM((2, n_pages, MLA_ROW, PAGE_SIZE), F32), pltpu.SemaphoreType.DMA((2,))],
    )
    return pl.pallas_call(
        functools.partial(_mla_decode_kernel, n_pages=n_pages, t_new=t_new, tk=tk),
        grid_spec=grid_spec,
        out_shape=jax.ShapeDtypeStruct((nb, rows, MLA_KV_LORA), F32),
        compiler_params=_cparams(("arbitrary",)),
        name="mla_decode",
    )(page_table, q_b, knew_b, pool)


def _uv_kernel(o_ref, wuv_ref, y_ref):
    rows = o_ref.shape[1]
    y_ref[...] = _uv_project(o_ref[...].reshape(MLA_HEADS * rows, MLA_KV_LORA), wuv_ref, rows)


def _uv(o_heads, wuv):
    rows = o_heads.shape[1]
    return pl.pallas_call(
        _uv_kernel,
        out_shape=jax.ShapeDtypeStruct((rows, MLA_WIDTH), F32),
        name="mla_value_up",
    )(o_heads, wuv)


def _nsa_dec_cmp_kernel(pt_ref, qn_ref, cnew_ref, pool_ref, wc1_ref, pe_ref, b1_ref, w2_ref, impmap_ref,
                        ocmp_ref, selm_ref, pbuf, xrows, sem, *, n_pages, t_new, t_pad, n_chunk, n_cmp_blocks,
                        n_sel_blocks):
    b = pl.program_id(0)
    nb = pl.num_programs(0)
    slot = b % 2
    past = n_pages * PAGE_SIZE

    def copy(sl):
        return lambda page, p: pltpu.make_async_copy(pool_ref.at[page], pbuf.at[sl, p], sem.at[sl])

    @pl.when(b == 0)
    def _():
        _page_copies(pt_ref, 0, n_pages, copy(0))

    @pl.when(b + 1 < nb)
    def _():
        _page_copies(pt_ref, b + 1, n_pages, copy(1 - slot))

    _page_wait_all(pool_ref, pbuf.at[slot], sem.at[slot])

    chunks_per_page = PAGE_SIZE // CMP_STRIDE

    def relayout(p, _):
        r0 = pl.multiple_of(p * (chunks_per_page * CHUNK_PITCH), 8)
        for s in range(2):
            t = pbuf[slot, p, s * LANES:(s + 1) * LANES, :].T
            for c in range(chunks_per_page):
                xrows[s, pl.ds(r0 + c * CHUNK_PITCH, CMP_STRIDE), :] = t[c * CMP_STRIDE:(c + 1) * CMP_STRIDE]
        return 0

    lax.fori_loop(0, n_pages, relayout, 0, unroll=4)

    cnew = cnew_ref[...]
    base = n_pages * chunks_per_page * CHUNK_PITCH
    tail = xrows.shape[1] - base
    for s in range(2):
        xrows[s, pl.ds(base, tail), :] = jnp.zeros((tail, LANES), F32)
        xrows[s, pl.ds(base, t_new), :] = cnew[:, s * LANES:(s + 1) * LANES]

    kvc = []
    for s in range(2):
        row = lambda p, s=s: xrows[s, pl.ds(p, n_chunk, stride=CHUNK_PITCH), :]
        load = lambda j, row=row: jnp.concatenate([row(2 * j), row(2 * j + 1)], axis=-1)
        kvc.append(_compress_slab(load, wc1_ref, pe_ref, b1_ref, w2_ref, s, n_chunk).astype(BF16))

    rows = NSA_HEADS * t_pad
    q_pos_r = past + (lax.broadcasted_iota(jnp.int32, (rows, 1), 0) & (t_pad - 1))
    q_pos_t = past + lax.broadcasted_iota(jnp.int32, (t_pad, 1), 0)
    o_cmp, imps = _cmp_branch(qn_ref[...], kvc[0][0:n_cmp_blocks], kvc[1][0:n_cmp_blocks], q_pos_r,
                              n_cmp_blocks, impmap_ref, t_pad)
    ocmp_ref[...] = o_cmp
    selm_ref[...] = _group_rows([_select_blocks(imp, q_pos_t, n_sel_blocks) for imp in imps], NSA_GROUP)


def _nsa_decode_cmp(page_table, qn_b, cnew_b, pool, wts, t_new, t_pad):
    nb, n_pages = page_table.shape
    past = n_pages * PAGE_SIZE
    total = past + t_new
    n_chunk_true = -(-total // CMP_STRIDE)
    nc = n_chunk_true - CMP_BLOCK // CMP_STRIDE + 1
    n_chunk = -(-n_chunk_true // 8) * 8
    assert nc % LANES == 0, "compressed-block count must be lane aligned"
    ns = past // SEL_BLOCK + (-(-t_new // SEL_BLOCK))
    sel_w = -(-ns // LANES) * LANES
    impmap = _importance_map(nc, nc, ns, sel_w)
    rows = NSA_HEADS * t_pad
    full = lambda a: pl.BlockSpec(a.shape, lambda b, pt: (0,) * a.ndim)
    per_b = lambda a: pl.BlockSpec((None,) + a.shape[1:], lambda b, pt: (b,) + (0,) * (a.ndim - 1))
    grid_spec = pltpu.PrefetchScalarGridSpec(
        num_scalar_prefetch=1,
        grid=(nb,),
        in_specs=[per_b(qn_b), per_b(cnew_b), pl.BlockSpec(memory_space=pl.ANY),
                  full(wts['wc1']), full(wts['pe']), full(wts['b1']), full(wts['w2']), full(impmap)],
        out_specs=[pl.BlockSpec((None, rows, LANES), lambda b, pt: (b, 0, 0)),
                   pl.BlockSpec((None, rows, sel_w), lambda b, pt: (b, 0, 0))],
        scratch_shapes=[pltpu.VMEM((2, n_pages, NSA_KV_WIDTH, PAGE_SIZE), F32),
                        pltpu.VMEM((2, n_chunk * CHUNK_PITCH, LANES), F32),
                        pltpu.SemaphoreType.DMA((2,))],
    )
    return pl.pallas_call(
        functools.partial(_nsa_dec_cmp_kernel, n_pages=n_pages, t_new=t_new, t_pad=t_pad, n_chunk=n_chunk,
                          n_cmp_blocks=nc, n_sel_blocks=ns),
        grid_spec=grid_spec,
        out_shape=[jax.ShapeDtypeStruct((nb, rows, LANES), F32),
                   jax.ShapeDtypeStruct((nb, rows, sel_w), F32)],
        compiler_params=_cparams(("arbitrary",)),
        name="nsa_decode_compress",
    )(page_table, qn_b, cnew_b, pool, wts['wc1'], wts['pe'], wts['b1'], wts['w2'], impmap)


def _nsa_dec_sel_kernel(pt_ref, qr_ref, snew_ref, wnew_ref, selm_ref, ocmp_ref, gates_ref, wst_ref, pool_ref,
                        o_ref, kbuf, sem, *, n_pages, t_new, t_pad, tk):
    b = pl.program_id(0)
    nb = pl.num_programs(0)
    slot = b % 2
    past = n_pages * PAGE_SIZE
    rows = NSA_HEADS * t_pad

    def copy(sl):
        return lambda page, p: pltpu.make_async_copy(pool_ref.at[page], kbuf.at[sl, p], sem.at[sl])

    @pl.when(b == 0)
    def _():
        _page_copies(pt_ref, 0, n_pages, copy(0))

    @pl.when(b + 1 < nb)
    def _():
        _page_copies(pt_ref, b + 1, n_pages, copy(1 - slot))

    _page_wait_all(pool_ref, kbuf.at[slot], sem.at[slot])

    blk_per_chunk = tk // SEL_BLOCK
    lane_lo = lax.broadcasted_iota(jnp.int32, (rows, LANES), 1) < SEL_BLOCK
    n_new = snew_ref.shape[0]
    t_q = lax.broadcasted_iota(jnp.int32, (rows, n_new), 0) & (t_pad - 1)
    t_k = lax.broadcasted_iota(jnp.int32, (rows, n_new), 1)
    new_causal = (t_k <= t_q) & (t_k < t_new)
    w_buf = wst_ref.shape[1]
    t_qw = lax.broadcasted_iota(jnp.int32, (rows, w_buf), 0) & (t_pad - 1)
    i_w = lax.broadcasted_iota(jnp.int32, (rows, w_buf), 1)
    win_mask = (t_qw + w_buf - i_w) < WINDOW
    snew = snew_ref[...]
    wnew = wnew_ref[...]
    qr = qr_ref[...]
    selm = selm_ref[...]

    def fresh():
        return jnp.full((rows, 1), NEG_INF, F32), jnp.zeros((rows, 1), F32), jnp.zeros((rows, LANES), F32)

    m, l, acc = fresh()
    pages_per_chunk = tk // PAGE_SIZE
    for c in range(past // tk):
        chunk_pages = range(c * pages_per_chunk, (c + 1) * pages_per_chunk)
        kt = jnp.concatenate([kbuf[slot, p, 0:LANES, :].astype(BF16) for p in chunk_pages], axis=1)
        vt = jnp.concatenate([kbuf[slot, p, LANES:2 * LANES, :].astype(BF16) for p in chunk_pages], axis=1)
        pieces = []
        for i in range(tk // LANES):
            jb = c * blk_per_chunk + 2 * i
            pieces.append(jnp.where(lane_lo, selm[:, jb:jb + 1], selm[:, jb + 1:jb + 2]))
        mask = jnp.concatenate(pieces, axis=-1) > 0.5
        m, l, acc = _online_update(_dot(qr, kt), mask, vt, m, l, acc, v_transposed=True)
    nb_past = past // SEL_BLOCK
    mask = (selm[:, nb_past:nb_past + 1] > 0.5) & new_causal
    m, l, acc = _online_update(_dot_nt(qr, snew[:, 0:LANES].astype(BF16)), mask,
                               snew[:, LANES:2 * LANES].astype(BF16), m, l, acc)
    o_sel = _finalize(l, acc)

    m, l, acc = fresh()
    m, l, acc = _online_update(_dot(qr, wst_ref[0:LANES, :].astype(BF16)), win_mask,
                               wst_ref[LANES:2 * LANES, :].astype(BF16), m, l, acc, v_transposed=True)
    m, l, acc = _online_update(_dot_nt(qr, wnew[:, 0:LANES].astype(BF16)), new_causal,
                               wnew[:, LANES:2 * LANES].astype(BF16), m, l, acc)
    o_win = _finalize(l, acc)

    o_ref[...] = _mix_heads(gates_ref[...], ocmp_ref[...], o_sel, o_win, t_pad)


def _nsa_decode_sel(page_table, qr_b, snew_b, wnew_b, selm, ocmp, gates_b, win_state, pool, t_new, t_pad):
    nb, n_pages = page_table.shape
    past = n_pages * PAGE_SIZE
    tk = 1024
    assert past % tk == 0
    per_b = lambda a: pl.BlockSpec((None,) + a.shape[1:], lambda b, pt: (b,) + (0,) * (a.ndim - 1))
    grid_spec = pltpu.PrefetchScalarGridSpec(
        num_scalar_prefetch=1,
        grid=(nb,),
        in_specs=[per_b(qr_b), per_b(snew_b), per_b(wnew_b), per_b(selm), per_b(ocmp), per_b(gates_b),
                  per_b(win_state), pl.BlockSpec(memory_space=pl.ANY)],
        out_specs=pl.BlockSpec((None, t_pad, NSA_WIDTH), lambda b, pt: (b, 0, 0)),
        scratch_shapes=[pltpu.VMEM((2, n_pages, NSA_KV_WIDTH, PAGE_SIZE), F32), pltpu.SemaphoreType.DMA((2,))],
    )
    return pl.pallas_call(
        functools.partial(_nsa_dec_sel_kernel, n_pages=n_pages, t_new=t_new, t_pad=t_pad, tk=tk),
        grid_spec=grid_spec,
        out_shape=jax.ShapeDtypeStruct((nb, t_pad, NSA_WIDTH), F32),
        compiler_params=_cparams(("arbitrary",)),
        name="nsa_decode_select_window",
    )(page_table, qr_b, snew_b, wnew_b, selm, ocmp, gates_b, win_state, pool)


def _prep_weights(w_in, q_norm_g, w_uq, kv_norm_g, w_uk, w_uv, cmp_pos_emb, cmp_w1, cmp_b1, cmp_w2, w_o, ln_g, ln_b):
    cuts = np.cumsum([MLA_Q_LORA, MLA_KV_LORA, MLA_ROPE_DIM, MLA_WIDTH, NSA_WIDTH, NSA_KV_WIDTH, NSA_KV_WIDTH,
                      NSA_KV_WIDTH, 3 * NSA_HEADS])[:].tolist()
    c_q, c_kv, k_r, z_mla, q_n, cmp_kv, sel_kv, win_kv, g_br, z_nsa = jnp.split(w_in, cuts, axis=1)
    pad = jnp.zeros((D_MODEL, LANES - MLA_ROPE_DIM - 3 * NSA_HEADS), w_in.dtype)
    w_in_p = jnp.concatenate([c_q, c_kv, z_mla, q_n, cmp_kv, sel_kv, win_kv, z_nsa, k_r, g_br, pad], axis=1)
    assert w_in_p.shape[1] == IN_WIDTH_P
    uq = w_uq.reshape(MLA_Q_LORA, MLA_HEADS, MLA_NOPE_DIM + MLA_ROPE_DIM)
    w_uq_p = jnp.concatenate([uq[:, :, :MLA_NOPE_DIM].reshape(MLA_Q_LORA, -1),
                              uq[:, :, MLA_NOPE_DIM:].reshape(MLA_Q_LORA, -1)], axis=1)
    w_uk_t = jnp.transpose(w_uk.reshape(MLA_KV_LORA, MLA_HEADS, MLA_NOPE_DIM), (1, 2, 0))
    w_uv_h = jnp.transpose(w_uv.reshape(MLA_KV_LORA, MLA_HEADS, MLA_V_DIM), (1, 0, 2))
    ratio = CMP_BLOCK // CMP_STRIDE
    eye = jnp.eye(NSA_KV_HEADS, dtype=w_in.dtype)
    w1r = cmp_w1.reshape(2, ratio, CMP_STRIDE, NSA_HEAD_DIM, NSA_HEAD_DIM)
    wc1 = jnp.einsum('gh,srpde->spgdrhe', eye, w1r).reshape(2, CMP_PAIRS, 2 * LANES, ratio * LANES)
    w2 = jnp.einsum('gh,sde->sgdhe', eye, cmp_w2).reshape(2, LANES, LANES)
    pe = jnp.transpose(cmp_pos_emb, (1, 0, 2))
    pe = jnp.concatenate([pe, pe], axis=-1).reshape(2, ratio, CMP_PAIRS, 2 * LANES)
    pe = jnp.transpose(pe, (0, 2, 1, 3))
    pe = jnp.pad(pe, ((0, 0), (0, 0), (0, 8 - ratio), (0, 0)))
    b1 = jnp.concatenate([cmp_b1, cmp_b1], axis=-1).reshape(2, 1, LANES)
    return dict(w_in=w_in_p.astype(BF16), q_norm_g=q_norm_g.reshape(1, -1), w_uq=w_uq_p.astype(BF16),
                w_uk=w_uk_t.astype(BF16), kv_norm_g=kv_norm_g.reshape(1, -1), w_uv=w_uv_h.astype(BF16),
                wc1=wc1.astype(BF16), w2=w2.astype(BF16), pe=pe, b1=b1,
                w_o=w_o.astype(BF16), ln_g=ln_g.reshape(1, -1), ln_b=ln_b.reshape(1, -1))


def _prompt_layer(x, wts):
    batch, seq, _ = x.shape
    x2d = x.reshape(batch * seq, D_MODEL)
    tm = min(256, seq)
    P = _project(x2d, _rope_tables(jnp.arange(seq, dtype=jnp.int32)), wts, tm, feature_major=True)
    o_mla = _mla_prompt(P['qmla'], P['mrowb'], wts['w_uv'], batch, seq)
    kvc = _compress_prompt(P['cmps'], wts, batch, seq)
    o_nsa = _nsa_prompt(P, kvc, batch, seq)
    y = _finish(x2d, o_mla, o_nsa, P['szm'], P['szn'], wts, tm)
    kvd = (2, NSA_KV_HEADS, NSA_HEAD_DIM)
    w_keep = min(WINDOW, seq)
    token_major = lambda a: jnp.swapaxes(a, 1, 2)
    return (y.reshape(batch, seq, D_MODEL),
            token_major(P['mrow']),
            token_major(P['cmp']).reshape((batch, seq) + kvd),
            token_major(P['sel']).reshape((batch, seq) + kvd),
            token_major(P['win'][:, :, seq - w_keep:]).reshape((batch, w_keep) + kvd))


def _sample_layer(x, cache_mla, cache_cmp, cache_sel, win_state, page_table, wts):
    nb, t_new, _ = x.shape
    n_pages = page_table.shape[1]
    past = n_pages * PAGE_SIZE
    n = nb * t_new
    t_pad = 8
    x2d = x.reshape(n, D_MODEL)
    pos = past + (jnp.arange(n, dtype=jnp.int32) % t_new)
    P = _project(x2d, _rope_tables(pos), wts, n)
    kvd = (2, NSA_KV_HEADS, NSA_HEAD_DIM)
    n_pool = cache_mla.shape[0]

    q_b = jnp.transpose(P['qmla'][:, :, :MLA_ROW].reshape(MLA_HEADS, nb, t_new, MLA_ROW), (1, 0, 2, 3))
    q_b = q_b.reshape(nb, -1, MLA_ROW)
    knew_b = jnp.pad(P['mrowb'][:, :MLA_ROW].reshape(nb, t_new, MLA_ROW), ((0, 0), (0, 16 - t_new), (0, 0)))
    feat_major = lambda a: jnp.swapaxes(a.reshape(a.shape[0], a.shape[1], -1), 1, 2)
    o_lat = _mla_decode(page_table, q_b, knew_b, feat_major(cache_mla))
    o_heads = jnp.transpose(o_lat.reshape(nb, MLA_HEADS, t_new, MLA_KV_LORA), (1, 0, 2, 3)).reshape(MLA_HEADS, n, -1)
    o_mla = _uv(o_heads, wts['w_uv'])

    def q_rows(q):
        q = jnp.transpose(q.reshape(NSA_HEADS, nb, t_new, LANES), (1, 0, 2, 3))
        q = jnp.pad(q, ((0, 0), (0, 0), (0, t_pad - t_new), (0, 0)))
        return q.reshape(nb, NSA_HEADS * t_pad, LANES)

    cnew_b = P['cmp'].reshape(nb, t_new, NSA_KV_WIDTH)
    ocmp, selm = _nsa_decode_cmp(page_table, q_rows(P['qn']), cnew_b, feat_major(cache_cmp), wts, t_new, t_pad)
    pad_rows = lambda a, r: jnp.pad(a.reshape(nb, t_new, -1), ((0, 0), (0, r - t_new), (0, 0)))
    o_nsa = _nsa_decode_sel(page_table, q_rows(P['qrot']), pad_rows(P['sel'], 16), pad_rows(P['win'], 16), selm,
                            ocmp, pad_rows(P['gates'], t_pad), feat_major(win_state), feat_major(cache_sel),
                            t_new, t_pad)
    o_nsa = o_nsa[:, :t_new].reshape(n, NSA_WIDTH)

    y = _finish(x2d, o_mla, o_nsa, P['szm'], P['szn'], wts, n)
    wst = win_state.reshape(nb, -1, NSA_KV_WIDTH)
    win_all = jnp.concatenate([wst, P['win'].reshape(nb, t_new, NSA_KV_WIDTH)], axis=1)[:, t_new:]
    return (y.reshape(nb, t_new, D_MODEL),
            P['mrow'].reshape(nb, t_new, MLA_ROW),
            P['cmp'].reshape((nb, t_new) + kvd),
            P['sel'].reshape((nb, t_new) + kvd),
            win_all.reshape((nb, win_all.shape[1]) + kvd))


def kernel(x_prompt, x_sample, cache_mla, cache_cmp_kv, cache_sel_kv, state_win_kv, page_table, w_in, q_norm_g, w_uq,
           kv_norm_g, w_uk, w_uv, cmp_pos_emb, cmp_w1, cmp_b1, cmp_w2, w_o, ln_g, ln_b):
    assert w_in.shape[0] == DEPTH
    wts = _prep_weights(w_in[0], q_norm_g[0], w_uq[0], kv_norm_g[0], w_uk[0], w_uv[0], cmp_pos_emb[0], cmp_w1[0],
                        cmp_b1[0], cmp_w2[0], w_o[0], ln_g[0], ln_b[0])
    yp, p_mla, p_cmp, p_sel, p_win = _prompt_layer(x_prompt, wts)
    drop_depth = lambda a: a.reshape(a.shape[1:])
    ys, s_mla, s_cmp, s_sel, s_win = _sample_layer(x_sample, drop_depth(cache_mla), drop_depth(cache_cmp_kv),
                                                   drop_depth(cache_sel_kv), drop_depth(state_win_kv), page_table, wts)
    add_depth = lambda a: a[None]
    return (yp, ys) + tuple(add_depth(a) for a in (p_mla, p_cmp, p_sel, p_win, s_mla, s_cmp, s_sel, s_win))
```

```python
import functools

import numpy as np
import jax
import jax.numpy as jnp
from jax import lax
from jax.experimental import pallas as pl
from jax.experimental.pallas import tpu as pltpu

F32 = jnp.float32
BF16 = jnp.bfloat16

D_MODEL = 1024
PAGE_SIZE = 128
MLA_HEADS = 8
MLA_V_DIM = 64
MLA_NOPE_DIM = 64
MLA_ROPE_DIM = 32
MLA_Q_LORA = 256
MLA_KV_LORA = 128
MLA_WIDTH = MLA_HEADS * MLA_V_DIM
MLA_ROW = MLA_KV_LORA + MLA_ROPE_DIM
NSA_HEADS = 8
NSA_KV_HEADS = 2
NSA_HEAD_DIM = 64
NSA_GROUP = NSA_HEADS // NSA_KV_HEADS
NSA_WIDTH = NSA_HEADS * NSA_HEAD_DIM
NSA_KV_WIDTH = 2 * NSA_KV_HEADS * NSA_HEAD_DIM
CMP_BLOCK = 32
CMP_STRIDE = 16
SEL_BLOCK = 64
SEL_TOP_N = 16
WINDOW = 512
Q_BLOCK = 128
ROPE_THETA = 10000.0
RMS_EPS = 1e-6
LN_EPS = 1e-5
NEG_INF = -1e30
MLA_SCALE = (MLA_NOPE_DIM + MLA_ROPE_DIM) ** -0.5
NSA_SCALE = NSA_HEAD_DIM ** -0.5
DEPTH = 1
ALPHA = (2 * DEPTH) ** 0.25

LANES = 128
LOG2E = 1.4426950408889634
MLA_KROW = 2 * LANES
MLA_ONES_LANE = MLA_ROW
NSA_KROW = 3 * LANES

SEG_CQ = (0, 256)
SEG_CKV = (256, 128)
SEG_ZMLA = (384, 512)
SEG_QN = (896, 512)
SEG_CMP = (1408, 256)
SEG_SEL = (1664, 256)
SEG_WIN = (1920, 256)
SEG_ZNSA = (2176, 512)
SEG_MISC = (2688, 128)
IN_WIDTH_P = 2816
MISC_GATE0 = MLA_ROPE_DIM

VMEM_LIMIT = 48 * 1024 * 1024

def _cparams(sem, flags=None):
    return pltpu.CompilerParams(dimension_semantics=sem, vmem_limit_bytes=VMEM_LIMIT, flags=flags)


def _sigmoid(x):
    return 1.0 / (1.0 + jnp.exp(-x))


def _silu(x):
    return x * _sigmoid(x)


def _rope_slab(x, c, sa, sb, half):
    return x * c + pltpu.roll(x, LANES - half, 1) * sa + pltpu.roll(x, half, 1) * sb


def _dot_nt(a, b):
    return lax.dot_general(a, b, (((1,), (1,)), ((), ())), preferred_element_type=F32)


def _dot(a, b):
    return jnp.dot(a, b, preferred_element_type=F32)


def _masked_softmax(s, mask):
    sm = jnp.where(mask, s, NEG_INF)
    m = jnp.max(sm, axis=-1, keepdims=True)
    e = jnp.exp2(sm - m)
    p = e / jnp.sum(e, axis=-1, keepdims=True)
    return jnp.where(mask, p, 0.0)


def _attend_segments(parts):
    is_bias = lambda mask: mask is not None and mask.dtype != jnp.bool_
    sm = [s if mask is None else (s + mask if is_bias(mask) else jnp.where(mask, s, NEG_INF))
          for s, mask, _ in parts]
    m = functools.reduce(jnp.maximum, [jnp.max(x, axis=-1, keepdims=True) for x in sm])
    l, acc = 0.0, 0.0
    for x, (_, mask, pv) in zip(sm, parts):
        p = jnp.exp2(x - m)
        if mask is not None and not is_bias(mask):
            p = jnp.where(mask, p, 0.0)
        l = l + jnp.sum(p, axis=-1, keepdims=True)
        acc = acc + pv(p.astype(BF16))
    ok = l > 0.0
    return jnp.where(ok, acc / jnp.where(ok, l, 1.0), 0.0)


def _flash_step(s, bias, v_ext, m, acc):
    sm = s if bias is None else s + bias
    m_new = jnp.maximum(m, jnp.max(sm, axis=-1, keepdims=True))
    p = jnp.exp2(sm - m_new)
    return m_new, jnp.exp2(m - m_new) * acc + _dot(p.astype(BF16), v_ext)


def _flash_result(acc, ones_lane):
    return acc[:, 0:LANES] * (1.0 / acc[:, ones_lane:ones_lane + 1])


def _tile_rows(x, reps):
    return jnp.concatenate([x] * reps, axis=0)


def _select_blocks(imp, q_pos, n_sel_blocks):
    rows, width = imp.shape
    j = lax.broadcasted_iota(jnp.int32, (rows, width), 1)
    cur = lax.shift_right_logical(q_pos, 6)
    avail = (j * SEL_BLOCK <= q_pos) & (j < n_sel_blocks)
    forced = (j == 0) | (j == cur) | (j == cur - 1)
    val = jnp.where(avail, jnp.where(forced, jnp.inf, imp), -jnp.inf)
    rank = jnp.zeros((rows, width), F32)
    for jp in range(n_sel_blocks):
        col = val[:, jp:jp + 1]
        ahead = (col > val) | ((col == val) & (j > jp))
        rank = rank + jnp.where(ahead, 1.0, 0.0)
    keep = (rank < float(min(SEL_TOP_N, n_sel_blocks))) & avail
    return jnp.where(keep, 1.0, 0.0)


def _select_blocks_t(imp_t, q_pos, n_sel_blocks):
    n_rows, width = imp_t.shape
    j = lax.broadcasted_iota(jnp.int32, (n_rows, width), 0)
    cur = lax.shift_right_logical(q_pos, 6)
    avail = (j * SEL_BLOCK <= q_pos) & (j < n_sel_blocks)
    forced = (j == 0) | (j == cur) | (j == cur - 1)
    val = jnp.where(avail, jnp.where(forced, jnp.inf, imp_t), -jnp.inf)
    n_pieces = -(-n_sel_blocks // 8)
    pieces = [val[8 * v:8 * v + 8] for v in range(n_pieces)]
    jrow = lax.broadcasted_iota(jnp.int32, (8, width), 0)
    ranks = [jnp.zeros((8, width), F32) for _ in range(n_pieces)]
    for jp in range(n_sel_blocks):
        cand = jnp.broadcast_to(val[jp:jp + 1], (8, width))
        for v in range(n_pieces):
            if 8 * v > jp:
                ahead = cand >= pieces[v]
            elif 8 * v + 7 < jp:
                ahead = cand > pieces[v]
            else:
                ahead = (cand > pieces[v]) | ((cand == pieces[v]) & (jrow > jp - 8 * v))
            ranks[v] = ranks[v] + jnp.where(ahead, 1.0, 0.0)
    if n_rows > 8 * n_pieces:
        ranks.append(jnp.zeros((n_rows - 8 * n_pieces, width), F32))
    rank = jnp.concatenate(ranks, axis=0)
    keep = (rank < float(min(SEL_TOP_N, n_sel_blocks))) & avail
    return jnp.where(keep, 1.0, 0.0)


def _proj_kernel(x_ref, w_ref, qg_ref, wuq_ref, wuk_ref, kvg_ref, tab_ref,
                 qmla_ref, mrow_ref, mrowb_ref, szm_ref, qn_ref, qrot_ref, cmp_ref, cmps_ref,
                 sel_ref, selb_ref, win_ref, winb_ref, gates_ref, szn_ref, *, feature_major):
    xb = x_ref[...].astype(BF16)

    def seg(s):
        return _dot(xb, w_ref[:, s[0]:s[0] + s[1]])

    cn, san, sbn = tab_ref[0], tab_ref[1], tab_ref[2]
    cm, sam, sbm = tab_ref[3], tab_ref[4], tab_ref[5]

    cq = seg(SEG_CQ)
    r = cq * lax.rsqrt(jnp.mean(cq * cq, axis=-1, keepdims=True) + RMS_EPS) * qg_ref[...]
    q = _dot(r.astype(BF16), wuq_ref[...])
    nope_w = MLA_HEADS * MLA_NOPE_DIM
    lane = lax.broadcasted_iota(jnp.int32, (q.shape[0], LANES), 1)
    for h in range(MLA_HEADS):
        ql = _dot(q[:, h * MLA_NOPE_DIM:(h + 1) * MLA_NOPE_DIM].astype(BF16), wuk_ref[h])
        qmla_ref[h, :, 0:LANES] = (ql * (MLA_SCALE * LOG2E)).astype(BF16)
    heads_per_slab = LANES // MLA_ROPE_DIM
    for jj in range(MLA_HEADS // heads_per_slab):
        qr = _rope_slab(q[:, nope_w + jj * LANES: nope_w + (jj + 1) * LANES], cm, sam, sbm, MLA_ROPE_DIM // 2)
        qr = qr * (MLA_SCALE * LOG2E)
        for hh in range(heads_per_slab):
            front = qr if hh == 0 else pltpu.roll(qr, LANES - hh * MLA_ROPE_DIM, 1)
            qmla_ref[jj * heads_per_slab + hh, :, LANES:2 * LANES] = (
                jnp.where(lane < MLA_ROPE_DIM, front, 0.0).astype(BF16))

    ckv = seg(SEG_CKV)
    lat = ckv * lax.rsqrt(jnp.mean(ckv * ckv, axis=-1, keepdims=True) + RMS_EPS) * kvg_ref[...]
    misc = seg(SEG_MISC)
    kr = _rope_slab(misc, cm, sam, sbm, MLA_ROPE_DIM // 2)
    if feature_major:
        mrow_ref[0:MLA_KV_LORA, :] = lat.T
        mrow_ref[MLA_KV_LORA:MLA_ROW, :] = kr.T[0:MLA_ROPE_DIM]
    else:
        mrow_ref[:, 0:MLA_KV_LORA] = lat
        mrow_ref[:, MLA_KV_LORA:MLA_ROW] = kr[:, 0:MLA_ROPE_DIM]
    mrowb_ref[:, 0:LANES] = lat.astype(BF16)
    ones_col = jnp.where(lane == MLA_ONES_LANE - LANES, 1.0, 0.0)
    mrowb_ref[:, LANES:2 * LANES] = jnp.where(lane < MLA_ROPE_DIM, kr, ones_col).astype(BF16)
    gates_ref[...] = _sigmoid(misc)

    szm_ref[...] = _silu(seg(SEG_ZMLA))
    szn_ref[...] = _silu(seg(SEG_ZNSA))

    qn = seg(SEG_QN)
    for jj in range(NSA_WIDTH // LANES):
        raw = qn[:, jj * LANES:(jj + 1) * LANES] * (NSA_SCALE * LOG2E)
        rot = _rope_slab(qn[:, jj * LANES:(jj + 1) * LANES], cn, san, sbn, NSA_HEAD_DIM // 2) * (NSA_SCALE * LOG2E)
        for src, dst in ((raw, qn_ref), (rot, qrot_ref)):
            swapped = pltpu.roll(src, NSA_HEAD_DIM, 1)
            for half in range(2):
                hd = 2 * jj + half
                g = hd // NSA_GROUP
                keep = (lane >= g * NSA_HEAD_DIM) & (lane < (g + 1) * NSA_HEAD_DIM)
                dst[hd] = jnp.where(keep, src if half == g else swapped, 0.0).astype(BF16)

    def put_rows(dst, first, second):
        if feature_major:
            dst[0:LANES, :] = first.T
            dst[LANES:2 * LANES, :] = second.T
        else:
            dst[:, 0:LANES] = first
            dst[:, LANES:2 * LANES] = second

    cmpv = seg(SEG_CMP)
    put_rows(cmp_ref, cmpv[:, 0:LANES], cmpv[:, LANES:2 * LANES])
    cmps_ref[0] = cmpv[:, 0:LANES]
    cmps_ref[1] = cmpv[:, LANES:2 * LANES]

    for src, dst, dstb in ((SEG_SEL, sel_ref, selb_ref), (SEG_WIN, win_ref, winb_ref)):
        kv = seg(src)
        k = _rope_slab(kv[:, 0:LANES], cn, san, sbn, NSA_HEAD_DIM // 2)
        v = kv[:, LANES:2 * LANES]
        put_rows(dst, k, v)
        dstb[:, 0:LANES] = k.astype(BF16)
        dstb[:, LANES:2 * LANES] = v.astype(BF16)
        dstb[:, 2 * LANES:3 * LANES] = jnp.where(lane == 0, 1.0, 0.0).astype(BF16)


def _rope_tables(pos):
    def tab(d):
        inv = 1.0 / (ROPE_THETA ** (jnp.arange(0, d, 2, dtype=F32) / d))
        ang = pos.astype(F32)[:, None] * inv[None, :]
        cos, sin = jnp.cos(ang), jnp.sin(ang)
        zero = jnp.zeros_like(sin)
        reps = LANES // d
        return (jnp.tile(jnp.concatenate([cos, cos], -1), (1, reps)),
                jnp.tile(jnp.concatenate([-sin, zero], -1), (1, reps)),
                jnp.tile(jnp.concatenate([zero, sin], -1), (1, reps)))
    return jnp.stack(tab(NSA_HEAD_DIM) + tab(MLA_ROPE_DIM))


def _project(x2d, tabs, wts, tm, feature_major=False):
    n = x2d.shape[0]
    p_rows = tabs.shape[1]
    assert n % tm == 0 and p_rows % tm == 0
    nt = p_rows // tm
    row = lambda w: pl.BlockSpec((tm, w), lambda i: (i, 0))
    full = lambda a: pl.BlockSpec(a.shape, lambda i: (0,) * a.ndim)
    if feature_major:
        out_rows = lambda w: ((n // p_rows, w, p_rows), F32, pl.BlockSpec((None, w, tm), lambda i: (i // nt, 0, i % nt)))
    else:
        out_rows = lambda w: ((n, w), F32, row(w))
    out_shapes = dict(
        qmla=((MLA_HEADS, n, MLA_KROW), BF16, pl.BlockSpec((MLA_HEADS, tm, MLA_KROW), lambda i: (0, i, 0))),
        mrow=out_rows(MLA_ROW),
        mrowb=((n, MLA_KROW), BF16, row(MLA_KROW)),
        szm=((n, MLA_WIDTH), F32, row(MLA_WIDTH)),
        qn=((NSA_HEADS, n, LANES), BF16, pl.BlockSpec((NSA_HEADS, tm, LANES), lambda i: (0, i, 0))),
        qrot=((NSA_HEADS, n, LANES), BF16, pl.BlockSpec((NSA_HEADS, tm, LANES), lambda i: (0, i, 0))),
        cmp=out_rows(NSA_KV_WIDTH),
        cmps=((2, n, LANES), F32, pl.BlockSpec((2, tm, LANES), lambda i: (0, i, 0))),
        sel=out_rows(NSA_KV_WIDTH),
        selb=((n, NSA_KROW), BF16, row(NSA_KROW)),
        win=out_rows(NSA_KV_WIDTH),
        winb=((n, NSA_KROW), BF16, row(NSA_KROW)),
        gates=((n, LANES), F32, row(LANES)),
        szn=((n, NSA_WIDTH), F32, row(NSA_WIDTH)),
    )
    names = list(out_shapes)
    outs = pl.pallas_call(
        functools.partial(_proj_kernel, feature_major=feature_major),
        grid=(n // tm,),
        in_specs=[row(D_MODEL), full(wts['w_in']), full(wts['q_norm_g']), full(wts['w_uq']), full(wts['w_uk']),
                  full(wts['kv_norm_g']), pl.BlockSpec((6, tm, LANES), lambda i: (0, i % nt, 0))],
        out_specs=[out_shapes[k][2] for k in names],
        out_shape=[jax.ShapeDtypeStruct(out_shapes[k][0], out_shapes[k][1]) for k in names],
        compiler_params=_cparams(("parallel",)),
        name="projection",
    )(x2d, wts['w_in'], wts['q_norm_g'], wts['w_uq'], wts['w_uk'], wts['kv_norm_g'], tabs)
    return dict(zip(names, outs))


def _uv_project(o, wuv_ref, rows):
    return jnp.concatenate(
        [_dot(o[h * rows:(h + 1) * rows].astype(BF16), wuv_ref[h]) for h in range(MLA_HEADS)], axis=-1)


def _flash_pipeline(n_chunks, prepare, scores, consume, s_a, s_b):
    scores(0, s_a)

    def body(i, _):
        c0 = 2 * i
        aux = prepare(c0)
        scores(c0 + 1, s_b)
        consume(c0, s_a, aux)

        @pl.when(c0 + 1 < n_chunks)
        def _():
            aux = prepare(c0 + 1)
            scores(c0 + 2, s_a)
            consume(c0 + 1, s_b, aux)
        return 0

    lax.fori_loop(0, (n_chunks + 1) // 2, body, 0)


def _mla_prompt_kernel(q_ref, k_ref, wuv_ref, o_ref, s_a, s_b, m_ref, acc_ref, *, tq, tk):
    qi = pl.program_id(1)
    rows = MLA_HEADS * tq
    q = q_ref[...].reshape(rows, MLA_KROW)
    n_chunks = (qi * tq) // tk + 1
    col = lax.broadcasted_iota(jnp.int32, (tq, tk), 1)
    q_pos = qi * tq + lax.broadcasted_iota(jnp.int32, (tq, tk), 0)
    m_ref[...] = jnp.full(m_ref.shape, NEG_INF, F32)
    acc_ref[...] = jnp.zeros(acc_ref.shape, F32)

    def chunk(c):
        return k_ref[pl.ds(pl.multiple_of(c * tk, tk), tk), :]

    def scores(c, dst):
        dst[...] = _dot_nt(q, chunk(jnp.minimum(c, n_chunks - 1)))

    def prepare(c):
        return _tile_rows(jnp.where(c * tk + col <= q_pos, 0.0, NEG_INF), MLA_HEADS)

    def consume(c, src, bias):
        m_ref[...], acc_ref[...] = _flash_step(src[...], bias, chunk(c), m_ref[...], acc_ref[...])

    _flash_pipeline(n_chunks, prepare, scores, consume, s_a, s_b)
    o_ref[...] = _uv_project(_flash_result(acc_ref[...], MLA_ONES_LANE), wuv_ref, tq)


def _mla_prompt(qmla, mrowb, wuv, batch, seq):
    tq = min(256, seq)
    tk = min(512, seq)
    nq = seq // tq
    assert tk % tq == 0 and seq % tk == 0
    return pl.pallas_call(
        functools.partial(_mla_prompt_kernel, tq=tq, tk=tk),
        grid=(batch, nq),
        in_specs=[pl.BlockSpec((MLA_HEADS, tq, MLA_KROW), lambda b, i: (0, b * nq + i, 0)),
                  pl.BlockSpec((None, seq, MLA_KROW), lambda b, i: (b, 0, 0)),
                  pl.BlockSpec(wuv.shape, lambda b, i: (0, 0, 0))],
        out_specs=pl.BlockSpec((tq, MLA_WIDTH), lambda b, i: (b * nq + i, 0)),
        out_shape=jax.ShapeDtypeStruct((batch * seq, MLA_WIDTH), F32),
        scratch_shapes=[pltpu.VMEM((MLA_HEADS * tq, tk), F32), pltpu.VMEM((MLA_HEADS * tq, tk), F32),
                        pltpu.VMEM((MLA_HEADS * tq, 1), F32), pltpu.VMEM((MLA_HEADS * tq, MLA_KROW), F32)],
        compiler_params=_cparams(("parallel", "parallel")),
        name="mla_prompt",
    )(qmla, mrowb.reshape(batch, seq, MLA_KROW), wuv)


CMP_PAIRS = CMP_STRIDE // 2
CHUNK_PITCH = 24


def _compress_slab(load_pair, wc1_ref, pe_ref, b1_ref, w2_ref, s, n_chunk):
    proj = jnp.zeros((n_chunk, 2 * LANES), F32)
    pe = jnp.zeros((8, 2 * LANES), F32)
    for j in range(CMP_PAIRS):
        w = wc1_ref[s, j]
        proj = proj + _dot(load_pair(j).astype(BF16), w)
        pe = pe + _dot(pe_ref[s, j].astype(BF16), w)
    bias = b1_ref[s] + pe[0:1, 0:LANES] + pe[1:2, LANES:2 * LANES]
    hid = proj[:, 0:LANES] + pltpu.roll(proj[:, LANES:2 * LANES], n_chunk - 1, 0) + bias
    return _dot(_silu(hid).astype(BF16), w2_ref[s])


def _compress_prompt_kernel(x_ref, wc1_ref, pe_ref, b1_ref, w2_ref, o_ref, *, n_chunk):
    for s in range(2):
        row = lambda p, s=s: x_ref[s, pl.ds(p, n_chunk, stride=CMP_STRIDE), :]
        load = lambda j, row=row: jnp.concatenate([row(2 * j), row(2 * j + 1)], axis=-1)
        o_ref[s] = _compress_slab(load, wc1_ref, pe_ref, b1_ref, w2_ref, s, n_chunk).astype(BF16)


def _compress_prompt(cmps, wts, batch, seq):
    n_chunk = seq // CMP_STRIDE
    full = lambda a: pl.BlockSpec(a.shape, lambda b: (0,) * a.ndim)
    return pl.pallas_call(
        functools.partial(_compress_prompt_kernel, n_chunk=n_chunk),
        grid=(batch,),
        in_specs=[pl.BlockSpec((2, seq, LANES), lambda b: (0, b, 0)),
                  full(wts['wc1']), full(wts['pe']), full(wts['b1']), full(wts['w2'])],
        out_specs=pl.BlockSpec((None, 2, n_chunk, LANES), lambda b: (b, 0, 0, 0)),
        out_shape=jax.ShapeDtypeStruct((batch, 2, n_chunk, LANES), BF16),
        compiler_params=_cparams(("parallel",)),
        name="compress_prompt",
    )(cmps, wts['wc1'], wts['pe'], wts['b1'], wts['w2'])


def _cmp_branch(q, kc, vc, q_pos_rows, n_cmp_blocks, impmap_ref, rows_t, imp_transposed=False):
    s = _dot_nt(q, kc)
    n = lax.broadcasted_iota(jnp.int32, (rows_t, s.shape[1]), 1)
    visible = (n * CMP_STRIDE + (CMP_BLOCK - 1) <= q_pos_rows[0:rows_t]) & (n < n_cmp_blocks)
    sm = s + _tile_rows(jnp.where(visible, 0.0, NEG_INF), NSA_HEADS)
    e = jnp.exp2(sm - jnp.max(sm, axis=-1, keepdims=True))
    any_visible = (q_pos_rows >= CMP_BLOCK - 1) & (n_cmp_blocks > 0)
    p = e * jnp.where(any_visible, 1.0 / jnp.sum(e, axis=-1, keepdims=True), 0.0)
    o_cmp = _dot(p.astype(BF16), vc)
    imps = []
    for g in range(NSA_KV_HEADS):
        base = g * NSA_GROUP * rows_t
        psum = p[base:base + rows_t]
        for hh in range(1, NSA_GROUP):
            psum = psum + p[base + hh * rows_t: base + (hh + 1) * rows_t]
        hi = psum.astype(BF16)
        lo = (psum - hi.astype(F32)).astype(BF16)
        if imp_transposed:
            imps.append(_dot_nt(impmap_ref[...], hi) + _dot_nt(impmap_ref[...], lo))
        else:
            imps.append(_dot(hi, impmap_ref[...]) + _dot(lo, impmap_ref[...]))
    return o_cmp, imps


def _mix_heads(gates, o_cmp, o_sel, o_win, rows_t):
    outs = []
    for hd in range(NSA_HEADS):
        g = hd // NSA_GROUP
        rs = slice(hd * rows_t, (hd + 1) * rows_t)
        ls = slice(g * NSA_HEAD_DIM, (g + 1) * NSA_HEAD_DIM)
        gcol = lambda br: gates[:, MISC_GATE0 + br * NSA_HEADS + hd: MISC_GATE0 + br * NSA_HEADS + hd + 1]
        outs.append(gcol(0) * o_cmp[rs, ls] + gcol(1) * o_sel[rs, ls] + gcol(2) * o_win[rs, ls])
    return jnp.concatenate(outs, axis=-1)


def _group_rows(per_group, reps):
    return jnp.concatenate([per_group[g] for g in range(NSA_KV_HEADS) for _ in range(reps)], axis=0)


def _nsa_prompt_kernel(qn_ref, qrot_ref, kvc_ref, sel_ref, win_ref, gates_ref, impmap_ref, o_ref,
                       s_a, s_b, m_ref, acc_ref, *, tq, tk, n_cmp_blocks, n_sel_blocks, wlen):
    qi = pl.program_id(1)
    rows = NSA_HEADS * tq
    q_pos_t = qi * tq + lax.broadcasted_iota(jnp.int32, (tq, 1), 0)
    q_pos_r = qi * tq + (lax.broadcasted_iota(jnp.int32, (rows, 1), 0) & (tq - 1))
    qn = qn_ref[...].reshape(rows, LANES)
    qrot = qrot_ref[...].reshape(rows, LANES)
    init = (jnp.full((rows, 1), NEG_INF, F32), jnp.zeros((rows, 2 * LANES), F32))

    w_start = pl.multiple_of(jnp.maximum(qi * tq + tq - wlen, 0), tq)
    d = q_pos_t - (w_start + lax.broadcasted_iota(jnp.int32, (tq, wlen), 1))
    bias = _tile_rows(jnp.where((d >= 0) & (d < WINDOW), 0.0, NEG_INF), NSA_HEADS)
    _, acc = _flash_step(_dot_nt(qrot, win_ref[pl.ds(w_start, wlen), 0:LANES]), bias,
                         win_ref[pl.ds(w_start, wlen), LANES:3 * LANES], *init)
    o_win = _flash_result(acc, LANES)

    o_cmp, imps_t = _cmp_branch(qn, kvc_ref[0], kvc_ref[1], q_pos_r, n_cmp_blocks, impmap_ref, tq,
                                imp_transposed=True)
    q_pos_lane = qi * tq + lax.broadcasted_iota(jnp.int32, (1, tq), 1)
    selms = [_select_blocks_t(imp_t, q_pos_lane, n_sel_blocks).T.astype(BF16) for imp_t in imps_t]

    blk_per_chunk = tk // SEL_BLOCK
    n_chunks = (qi * tq + tq + tk - 1) // tk
    e_row = lax.broadcasted_iota(jnp.int32, (LANES, tk), 0)
    e_col = lax.shift_right_logical(lax.broadcasted_iota(jnp.int32, (LANES, tk), 1), 6)
    col_t = lax.broadcasted_iota(jnp.int32, (tq, tk), 1)

    def sel_scores(c, dst):
        start = pl.multiple_of(jnp.minimum(c, n_chunks - 1) * tk, tk)
        dst[...] = _dot_nt(qrot, sel_ref[pl.ds(start, tk), 0:LANES])

    def sel_prepare(c):
        expand = jnp.where(e_row == c * blk_per_chunk + e_col, 1.0, 0.0).astype(BF16)
        causal = (c * tk + col_t) <= q_pos_t
        return _group_rows([jnp.where((_dot(sm, expand) > 0.5) & causal, 0.0, NEG_INF) for sm in selms], NSA_GROUP)

    def sel_consume(c, src, bias):
        v_ext = sel_ref[pl.ds(pl.multiple_of(c * tk, tk), tk), LANES:3 * LANES]
        m_ref[...], acc_ref[...] = _flash_step(src[...], bias, v_ext, m_ref[...], acc_ref[...])

    m_ref[...], acc_ref[...] = init
    _flash_pipeline(n_chunks, sel_prepare, sel_scores, sel_consume, s_a, s_b)
    o_sel = _flash_result(acc_ref[...], LANES)

    o_ref[...] = _mix_heads(gates_ref[...], o_cmp, o_sel, o_win, tq)


def _importance_map(nc_rows, nc, ns, width):
    i = np.arange(nc_rows)[:, None]
    j = np.arange(width)[None, :]
    lo = np.maximum(i * CMP_STRIDE, j * SEL_BLOCK)
    hi = np.minimum(i * CMP_STRIDE + CMP_BLOCK, (j + 1) * SEL_BLOCK)
    m = np.maximum(hi - lo, 0).astype(np.float32) / CMP_BLOCK
    m = m * (i < nc) * (j < ns)
    return jnp.asarray(m, dtype=BF16)


def _nsa_prompt(P, kvc, batch, seq):
    tq = min(256, seq)
    tk = min(512, seq)
    nq = seq // tq
    n_chunk = seq // CMP_STRIDE
    nc = n_chunk - CMP_BLOCK // CMP_STRIDE + 1
    ns = seq // SEL_BLOCK
    assert ns <= LANES and seq % tk == 0 and seq % CMP_STRIDE == 0 and seq % SEL_BLOCK == 0
    wlen = min(WINDOW + tq, seq)
    impmap = _importance_map(n_chunk, nc, ns, LANES).T
    blk = lambda w: pl.BlockSpec((tq, w), lambda b, i: (b * nq + i, 0))
    qblk = pl.BlockSpec((NSA_HEADS, tq, LANES), lambda b, i: (0, b * nq + i, 0))
    per_b = lambda a: pl.BlockSpec((None,) + a.shape[1:], lambda b, i: (b,) + (0,) * (a.ndim - 1))
    selb = P['selb'].reshape(batch, seq, NSA_KROW)
    winb = P['winb'].reshape(batch, seq, NSA_KROW)
    return pl.pallas_call(
        functools.partial(_nsa_prompt_kernel, tq=tq, tk=tk, n_cmp_blocks=nc, n_sel_blocks=ns, wlen=wlen),
        grid=(batch, nq),
        in_specs=[qblk, qblk, per_b(kvc), per_b(selb), per_b(winb), blk(LANES),
                  pl.BlockSpec(impmap.shape, lambda b, i: (0, 0))],
        out_specs=blk(NSA_WIDTH),
        out_shape=jax.ShapeDtypeStruct((batch * seq, NSA_WIDTH), F32),
        scratch_shapes=[pltpu.VMEM((NSA_HEADS * tq, tk), F32), pltpu.VMEM((NSA_HEADS * tq, tk), F32),
                        pltpu.VMEM((NSA_HEADS * tq, 1), F32), pltpu.VMEM((NSA_HEADS * tq, 2 * LANES), F32)],
        compiler_params=_cparams(("parallel", "parallel")),
        name="nsa_prompt",
    )(P['qn'], P['qrot'], kvc, selb, winb, P['gates'], impmap)


def _finish_kernel(x_ref, omla_ref, onsa_ref, szm_ref, szn_ref, wo_ref, g_ref, b_ref, y_ref):
    mixed = jnp.concatenate([omla_ref[...] * szm_ref[...], onsa_ref[...] * szn_ref[...]], axis=-1)
    h = ALPHA * x_ref[...] + _dot(mixed.astype(BF16), wo_ref[...])
    mu = jnp.mean(h, axis=-1, keepdims=True)
    d = h - mu
    var = jnp.mean(d * d, axis=-1, keepdims=True)
    y_ref[...] = d * lax.rsqrt(var + LN_EPS) * g_ref[...] + b_ref[...]


def _finish(x2d, omla, onsa, szm, szn, wts, tm):
    n = x2d.shape[0]
    row = lambda w: pl.BlockSpec((tm, w), lambda i: (i, 0))
    full = lambda a: pl.BlockSpec(a.shape, lambda i: (0,) * a.ndim)
    return pl.pallas_call(
        _finish_kernel,
        grid=(n // tm,),
        in_specs=[row(D_MODEL), row(MLA_WIDTH), row(NSA_WIDTH), row(MLA_WIDTH), row(NSA_WIDTH),
                  full(wts['w_o']), full(wts['ln_g']), full(wts['ln_b'])],
        out_specs=row(D_MODEL),
        out_shape=jax.ShapeDtypeStruct((n, D_MODEL), F32),
        compiler_params=_cparams(("parallel",)),
        name="finish",
    )(x2d, omla, onsa, szm, szn, wts['w_o'], wts['ln_g'], wts['ln_b'])


def _page_copies(pt_ref, b, n_pages, make_copy):
    assert n_pages % 2 == 0

    def start(i, _):
        make_copy(pt_ref[b, 2 * i], 2 * i).start(priority=0)
        make_copy(pt_ref[b, 2 * i + 1], 2 * i + 1).start(priority=1)
        return 0
    lax.fori_loop(0, n_pages // 2, start, 0, unroll=4)


def _page_wait_all(pool_ref, slot_buf, sem):
    n_pages = slot_buf.shape[0]
    pltpu.make_async_copy(pool_ref.at[pl.ds(0, n_pages)], slot_buf, sem).wait()


def _mla_decode_kernel(pt_ref, q_ref, knew_ref, pool_ref, o_ref, kbuf, sem, *, n_pages, t_new, tk):
    b = pl.program_id(0)
    nb = pl.num_programs(0)
    slot = b % 2
    past = n_pages * PAGE_SIZE

    def copy(sl):
        return lambda page, p: pltpu.make_async_copy(pool_ref.at[page], kbuf.at[sl, p], sem.at[sl])

    @pl.when(b == 0)
    def _():
        _page_copies(pt_ref, 0, n_pages, copy(0))

    @pl.when(b + 1 < nb)
    def _():
        _page_copies(pt_ref, b + 1, n_pages, copy(1 - slot))

    _page_wait_all(pool_ref, kbuf.at[slot], sem.at[slot])

    q = q_ref[...]
    rows = q.shape[0]
    pages_per_chunk = tk // PAGE_SIZE
    parts = []
    for c in range(n_pages // pages_per_chunk):
        kt = jnp.concatenate([kbuf[slot, c * pages_per_chunk + i].astype(BF16) for i in range(pages_per_chunk)],
                             axis=1)
        parts.append((_dot(q, kt), None, lambda p, kt=kt: _dot_nt(p, kt[0:MLA_KV_LORA])))
    kn = knew_ref[...]
    t_q = lax.broadcasted_iota(jnp.int32, (rows, kn.shape[0]), 0) & (t_new - 1)
    t_k = lax.broadcasted_iota(jnp.int32, (rows, kn.shape[0]), 1)
    parts.append((_dot_nt(q, kn), t_k <= t_q, lambda p: _dot(p, kn[:, 0:MLA_KV_LORA])))
    o_ref[...] = _attend_segments(parts)


def _mla_decode(page_table, q_b, knew_b, pool):
    nb, n_pages = page_table.shape
    rows = q_b.shape[1]
    t_new = rows // MLA_HEADS
    tk = 1024
    past = n_pages * PAGE_SIZE
    assert past % tk == 0 and (t_new & (t_new - 1)) == 0
    grid_spec = pltpu.PrefetchScalarGridSpec(
        num_scalar_prefetch=1,
        grid=(nb,),
        in_specs=[pl.BlockSpec((None, rows, MLA_ROW), lambda b, pt: (b, 0, 0)),
                  pl.BlockSpec((None,) + knew_b.shape[1:], lambda b, pt: (b, 0, 0)),
                  pl.BlockSpec(memory_space=pl.ANY)],
        out_specs=pl.BlockSpec((None, rows, MLA_KV_LORA), lambda b, pt: (b, 0, 0)),
        scratch_shapes=[pltpu.VMEM((2, n_pages, MLA_ROW, PAGE_SIZE), F32), pltpu.SemaphoreType.DMA((2,))],
    )
    return pl.pallas_call(
        functools.partial(_mla_decode_kernel, n_pages=n_pages, t_new=t_new, tk=tk),
        grid_spec=grid_spec,
        out_shape=jax.ShapeDtypeStruct((nb, rows, MLA_KV_LORA), F32),
        compiler_params=_cparams(("arbitrary",)),
        name="mla_decode",
    )(page_table, q_b, knew_b, pool)


def _uv_kernel(o_ref, wuv_ref, y_ref):
    rows = o_ref.shape[1]
    y_ref[...] = _uv_project(o_ref[...].reshape(MLA_HEADS * rows, MLA_KV_LORA), wuv_ref, rows)


def _uv(o_heads, wuv):
    rows = o_heads.shape[1]
    return pl.pallas_call(
        _uv_kernel,
        out_shape=jax.ShapeDtypeStruct((rows, MLA_WIDTH), F32),
        name="mla_value_up",
    )(o_heads, wuv)


def _nsa_dec_cmp_kernel(pt_ref, qn_ref, cnew_ref, pool_ref, wc1_ref, pe_ref, b1_ref, w2_ref, impmap_ref,
                        ocmp_ref, selm_ref, pbuf, xrows, sem, *, n_pages, t_new, t_pad, n_chunk, n_cmp_blocks,
                        n_sel_blocks):
    b = pl.program_id(0)
    nb = pl.num_programs(0)
    slot = b % 2
    past = n_pages * PAGE_SIZE

    def copy(sl):
        return lambda page, p: pltpu.make_async_copy(pool_ref.at[page], pbuf.at[sl, p], sem.at[sl])

    @pl.when(b == 0)
    def _():
        _page_copies(pt_ref, 0, n_pages, copy(0))

    @pl.when(b + 1 < nb)
    def _():
        _page_copies(pt_ref, b + 1, n_pages, copy(1 - slot))

    _page_wait_all(pool_ref, pbuf.at[slot], sem.at[slot])

    chunks_per_page = PAGE_SIZE // CMP_STRIDE

    def relayout(p, _):
        r0 = pl.multiple_of(p * (chunks_per_page * CHUNK_PITCH), 8)
        for s in range(2):
            t = pbuf[slot, p, s * LANES:(s + 1) * LANES, :].T
            for c in range(chunks_per_page):
                xrows[s, pl.ds(r0 + c * CHUNK_PITCH, CMP_STRIDE), :] = t[c * CMP_STRIDE:(c + 1) * CMP_STRIDE]
        return 0

    lax.fori_loop(0, n_pages, relayout, 0, unroll=4)

    cnew = cnew_ref[...]
    base = n_pages * chunks_per_page * CHUNK_PITCH
    tail = xrows.shape[1] - base
    for s in range(2):
        xrows[s, pl.ds(base, tail), :] = jnp.zeros((tail, LANES), F32)
        xrows[s, pl.ds(base, t_new), :] = cnew[:, s * LANES:(s + 1) * LANES]

    kvc = []
    for s in range(2):
        row = lambda p, s=s: xrows[s, pl.ds(p, n_chunk, stride=CHUNK_PITCH), :]
        load = lambda j, row=row: jnp.concatenate([row(2 * j), row(2 * j + 1)], axis=-1)
        kvc.append(_compress_slab(load, wc1_ref, pe_ref, b1_ref, w2_ref, s, n_chunk).astype(BF16))

    rows = NSA_HEADS * t_pad
    q_pos_r = past + (lax.broadcasted_iota(jnp.int32, (rows, 1), 0) & (t_pad - 1))
    q_pos_t = past + lax.broadcasted_iota(jnp.int32, (t_pad, 1), 0)
    o_cmp, imps = _cmp_branch(qn_ref[...], kvc[0][0:n_cmp_blocks], kvc[1][0:n_cmp_blocks], q_pos_r,
                              n_cmp_blocks, impmap_ref, t_pad)
    ocmp_ref[...] = o_cmp
    selm_ref[...] = _group_rows([_select_blocks(imp, q_pos_t, n_sel_blocks) for imp in imps], NSA_GROUP)


def _nsa_decode_cmp(page_table, qn_b, cnew_b, pool, wts, t_new, t_pad):
    nb, n_pages = page_table.shape
    past = n_pages * PAGE_SIZE
    total = past + t_new
    n_chunk_true = -(-total // CMP_STRIDE)
    nc = n_chunk_true - CMP_BLOCK // CMP_STRIDE + 1
    n_chunk = -(-n_chunk_true // 8) * 8
    assert nc % LANES == 0, "compressed-block count must be lane aligned"
    ns = past // SEL_BLOCK + (-(-t_new // SEL_BLOCK))
    sel_w = -(-ns // LANES) * LANES
    impmap = _importance_map(nc, nc, ns, sel_w)
    rows = NSA_HEADS * t_pad
    full = lambda a: pl.BlockSpec(a.shape, lambda b, pt: (0,) * a.ndim)
    per_b = lambda a: pl.BlockSpec((None,) + a.shape[1:], lambda b, pt: (b,) + (0,) * (a.ndim - 1))
    grid_spec = pltpu.PrefetchScalarGridSpec(
        num_scalar_prefetch=1,
        grid=(nb,),
        in_specs=[per_b(qn_b), per_b(cnew_b), pl.BlockSpec(memory_space=pl.ANY),
                  full(wts['wc1']), full(wts['pe']), full(wts['b1']), full(wts['w2']), full(impmap)],
        out_specs=[pl.BlockSpec((None, rows, LANES), lambda b, pt: (b, 0, 0)),
                   pl.BlockSpec((None, rows, sel_w), lambda b, pt: (b, 0, 0))],
        scratch_shapes=[pltpu.VMEM((2, n_pages, NSA_KV_WIDTH, PAGE_SIZE), F32),
                        pltpu.VMEM((2, n_chunk * CHUNK_PITCH, LANES), F32),
                        pltpu.SemaphoreType.DMA((2,))],
    )
    return pl.pallas_call(
        functools.partial(_nsa_dec_cmp_kernel, n_pages=n_pages, t_new=t_new, t_pad=t_pad, n_chunk=n_chunk,
                          n_cmp_blocks=nc, n_sel_blocks=ns),
        grid_spec=grid_spec,
        out_shape=[jax.ShapeDtypeStruct((nb, rows, LANES), F32),
                   jax.ShapeDtypeStruct((nb, rows, sel_w), F32)],
        compiler_params=_cparams(("arbitrary",)),
        name="nsa_decode_compress",
    )(page_table, qn_b, cnew_b, pool, wts['wc1'], wts['pe'], wts['b1'], wts['w2'], impmap)


def _nsa_dec_sel_kernel(pt_ref, qr_ref, snew_ref, wnew_ref, selm_ref, ocmp_ref, gates_ref, wst_ref, pool_ref,
                        o_ref, kbuf, sem, *, n_pages, t_new, t_pad, tk):
    b = pl.program_id(0)
    nb = pl.num_programs(0)
    slot = b % 2
    past = n_pages * PAGE_SIZE
    rows = NSA_HEADS * t_pad

    def copy(sl):
        return lambda page, p: pltpu.make_async_copy(pool_ref.at[page], kbuf.at[sl, p], sem.at[sl])

    @pl.when(b == 0)
    def _():
        _page_copies(pt_ref, 0, n_pages, copy(0))

    @pl.when(b + 1 < nb)
    def _():
        _page_copies(pt_ref, b + 1, n_pages, copy(1 - slot))

    _page_wait_all(pool_ref, kbuf.at[slot], sem.at[slot])

    blk_per_chunk = tk // SEL_BLOCK
    lane_lo = lax.broadcasted_iota(jnp.int32, (t_pad, LANES), 1) < SEL_BLOCK
    n_new = snew_ref.shape[0]
    t_q = lax.broadcasted_iota(jnp.int32, (rows, n_new), 0) & (t_pad - 1)
    t_k = lax.broadcasted_iota(jnp.int32, (rows, n_new), 1)
    new_causal = (t_k <= t_q) & (t_k < t_new)
    w_buf = wst_ref.shape[1]
    t_qw = lax.broadcasted_iota(jnp.int32, (rows, w_buf), 0) & (t_pad - 1)
    i_w = lax.broadcasted_iota(jnp.int32, (rows, w_buf), 1)
    win_mask = (t_qw + w_buf - i_w) < WINDOW
    snew = snew_ref[...]
    wnew = wnew_ref[...]
    qr = qr_ref[...]
    selm = selm_ref[...]
    sel_groups = [selm[g * NSA_GROUP * t_pad: g * NSA_GROUP * t_pad + t_pad] for g in range(NSA_KV_HEADS)]

    parts = []
    pages_per_chunk = tk // PAGE_SIZE
    for c in range(past // tk):
        chunk_pages = range(c * pages_per_chunk, (c + 1) * pages_per_chunk)
        kt = jnp.concatenate([kbuf[slot, p, 0:LANES, :].astype(BF16) for p in chunk_pages], axis=1)
        vt = jnp.concatenate([kbuf[slot, p, LANES:2 * LANES, :].astype(BF16) for p in chunk_pages], axis=1)
        biases = []
        for sel_g in sel_groups:
            pieces = []
            for i in range(tk // LANES):
                jb = c * blk_per_chunk + 2 * i
                pieces.append(jnp.where(lane_lo, sel_g[:, jb:jb + 1], sel_g[:, jb + 1:jb + 2]))
            biases.append(jnp.where(jnp.concatenate(pieces, axis=-1) > 0.5, 0.0, NEG_INF))
        parts.append((_dot(qr, kt), _group_rows(biases, NSA_GROUP), lambda p, vt=vt: _dot_nt(p, vt)))
    nb_past = past // SEL_BLOCK
    mask = (selm[:, nb_past:nb_past + 1] > 0.5) & new_causal
    parts.append((_dot_nt(qr, snew[:, 0:LANES].astype(BF16)), mask,
                  lambda p: _dot(p, snew[:, LANES:2 * LANES].astype(BF16))))
    o_sel = _attend_segments(parts)

    o_win = _attend_segments([
        (_dot(qr, wst_ref[0:LANES, :].astype(BF16)), win_mask,
         lambda p: _dot_nt(p, wst_ref[LANES:2 * LANES, :].astype(BF16))),
        (_dot_nt(qr, wnew[:, 0:LANES].astype(BF16)), new_causal,
         lambda p: _dot(p, wnew[:, LANES:2 * LANES].astype(BF16)))])

    o_ref[...] = _mix_heads(gates_ref[...], ocmp_ref[...], o_sel, o_win, t_pad)


def _nsa_decode_sel(page_table, qr_b, snew_b, wnew_b, selm, ocmp, gates_b, win_state, pool, t_new, t_pad):
    nb, n_pages = page_table.shape
    past = n_pages * PAGE_SIZE
    tk = 1024
    assert past % tk == 0
    per_b = lambda a: pl.BlockSpec((None,) + a.shape[1:], lambda b, pt: (b,) + (0,) * (a.ndim - 1))
    grid_spec = pltpu.PrefetchScalarGridSpec(
        num_scalar_prefetch=1,
        grid=(nb,),
        in_specs=[per_b(qr_b), per_b(snew_b), per_b(wnew_b), per_b(selm), per_b(ocmp), per_b(gates_b),
                  per_b(win_state), pl.BlockSpec(memory_space=pl.ANY)],
        out_specs=pl.BlockSpec((None, t_pad, NSA_WIDTH), lambda b, pt: (b, 0, 0)),
        scratch_shapes=[pltpu.VMEM((2, n_pages, NSA_KV_WIDTH, PAGE_SIZE), F32), pltpu.SemaphoreType.DMA((2,))],
    )
    return pl.pallas_call(
        functools.partial(_nsa_dec_sel_kernel, n_pages=n_pages, t_new=t_new, t_pad=t_pad, tk=tk),
        grid_spec=grid_spec,
        out_shape=jax.ShapeDtypeStruct((nb, t_pad, NSA_WIDTH), F32),
        compiler_params=_cparams(("arbitrary",)),
        name="nsa_decode_select_window",
    )(page_table, qr_b, snew_b, wnew_b, selm, ocmp, gates_b, win_state, pool)


def _prep_weights(w_in, q_norm_g, w_uq, kv_norm_g, w_uk, w_uv, cmp_pos_emb, cmp_w1, cmp_b1, cmp_w2, w_o, ln_g, ln_b):
    cuts = np.cumsum([MLA_Q_LORA, MLA_KV_LORA, MLA_ROPE_DIM, MLA_WIDTH, NSA_WIDTH, NSA_KV_WIDTH, NSA_KV_WIDTH,
                      NSA_KV_WIDTH, 3 * NSA_HEADS])[:].tolist()
    c_q, c_kv, k_r, z_mla, q_n, cmp_kv, sel_kv, win_kv, g_br, z_nsa = jnp.split(w_in, cuts, axis=1)
    pad = jnp.zeros((D_MODEL, LANES - MLA_ROPE_DIM - 3 * NSA_HEADS), w_in.dtype)
    w_in_p = jnp.concatenate([c_q, c_kv, z_mla, q_n, cmp_kv, sel_kv, win_kv, z_nsa, k_r, g_br, pad], axis=1)
    assert w_in_p.shape[1] == IN_WIDTH_P
    uq = w_uq.reshape(MLA_Q_LORA, MLA_HEADS, MLA_NOPE_DIM + MLA_ROPE_DIM)
    w_uq_p = jnp.concatenate([uq[:, :, :MLA_NOPE_DIM].reshape(MLA_Q_LORA, -1),
                              uq[:, :, MLA_NOPE_DIM:].reshape(MLA_Q_LORA, -1)], axis=1)
    w_uk_t = jnp.transpose(w_uk.reshape(MLA_KV_LORA, MLA_HEADS, MLA_NOPE_DIM), (1, 2, 0))
    w_uv_h = jnp.transpose(w_uv.reshape(MLA_KV_LORA, MLA_HEADS, MLA_V_DIM), (1, 0, 2))
    ratio = CMP_BLOCK // CMP_STRIDE
    eye = jnp.eye(NSA_KV_HEADS, dtype=w_in.dtype)
    w1r = cmp_w1.reshape(2, ratio, CMP_STRIDE, NSA_HEAD_DIM, NSA_HEAD_DIM)
    wc1 = jnp.einsum('gh,srpde->spgdrhe', eye, w1r).reshape(2, CMP_PAIRS, 2 * LANES, ratio * LANES)
    w2 = jnp.einsum('gh,sde->sgdhe', eye, cmp_w2).reshape(2, LANES, LANES)
    pe = jnp.transpose(cmp_pos_emb, (1, 0, 2))
    pe = jnp.concatenate([pe, pe], axis=-1).reshape(2, ratio, CMP_PAIRS, 2 * LANES)
    pe = jnp.transpose(pe, (0, 2, 1, 3))
    pe = jnp.pad(pe, ((0, 0), (0, 0), (0, 8 - ratio), (0, 0)))
    b1 = jnp.concatenate([cmp_b1, cmp_b1], axis=-1).reshape(2, 1, LANES)
    return dict(w_in=w_in_p.astype(BF16), q_norm_g=q_norm_g.reshape(1, -1), w_uq=w_uq_p.astype(BF16),
                w_uk=w_uk_t.astype(BF16), kv_norm_g=kv_norm_g.reshape(1, -1), w_uv=w_uv_h.astype(BF16),
                wc1=wc1.astype(BF16), w2=w2.astype(BF16), pe=pe, b1=b1,
                w_o=w_o.astype(BF16), ln_g=ln_g.reshape(1, -1), ln_b=ln_b.reshape(1, -1))


def _prompt_layer(x, wts):
    batch, seq, _ = x.shape
    x2d = x.reshape(batch * seq, D_MODEL)
    tm = min(256, seq)
    P = _project(x2d, _rope_tables(jnp.arange(seq, dtype=jnp.int32)), wts, tm, feature_major=True)
    o_mla = _mla_prompt(P['qmla'], P['mrowb'], wts['w_uv'], batch, seq)
    kvc = _compress_prompt(P['cmps'], wts, batch, seq)
    o_nsa = _nsa_prompt(P, kvc, batch, seq)
    y = _finish(x2d, o_mla, o_nsa, P['szm'], P['szn'], wts, tm)
    kvd = (2, NSA_KV_HEADS, NSA_HEAD_DIM)
    w_keep = min(WINDOW, seq)
    token_major = lambda a: jnp.swapaxes(a, 1, 2)
    return (y.reshape(batch, seq, D_MODEL),
            token_major(P['mrow']),
            token_major(P['cmp']).reshape((batch, seq) + kvd),
            token_major(P['sel']).reshape((batch, seq) + kvd),
            token_major(P['win'][:, :, seq - w_keep:]).reshape((batch, w_keep) + kvd))


def _sample_layer(x, cache_mla, cache_cmp, cache_sel, win_state, page_table, wts):
    nb, t_new, _ = x.shape
    n_pages = page_table.shape[1]
    past = n_pages * PAGE_SIZE
    n = nb * t_new
    t_pad = 8
    x2d = x.reshape(n, D_MODEL)
    pos = past + (jnp.arange(n, dtype=jnp.int32) % t_new)
    P = _project(x2d, _rope_tables(pos), wts, n)
    kvd = (2, NSA_KV_HEADS, NSA_HEAD_DIM)
    n_pool = cache_mla.shape[0]

    q_b = jnp.transpose(P['qmla'][:, :, :MLA_ROW].reshape(MLA_HEADS, nb, t_new, MLA_ROW), (1, 0, 2, 3))
    q_b = q_b.reshape(nb, -1, MLA_ROW)
    knew_b = jnp.pad(P['mrowb'][:, :MLA_ROW].reshape(nb, t_new, MLA_ROW), ((0, 0), (0, 16 - t_new), (0, 0)))
    feat_major = lambda a: jnp.swapaxes(a.reshape(a.shape[0], a.shape[1], -1), 1, 2)
    o_lat = _mla_decode(page_table, q_b, knew_b, feat_major(cache_mla))
    o_heads = jnp.transpose(o_lat.reshape(nb, MLA_HEADS, t_new, MLA_KV_LORA), (1, 0, 2, 3)).reshape(MLA_HEADS, n, -1)
    o_mla = _uv(o_heads, wts['w_uv'])

    def q_rows(q):
        q = jnp.transpose(q.reshape(NSA_HEADS, nb, t_new, LANES), (1, 0, 2, 3))
        q = jnp.pad(q, ((0, 0), (0, 0), (0, t_pad - t_new), (0, 0)))
        return q.reshape(nb, NSA_HEADS * t_pad, LANES)

    cnew_b = P['cmp'].reshape(nb, t_new, NSA_KV_WIDTH)
    ocmp, selm = _nsa_decode_cmp(page_table, q_rows(P['qn']), cnew_b, feat_major(cache_cmp), wts, t_new, t_pad)
    pad_rows = lambda a, r: jnp.pad(a.reshape(nb, t_new, -1), ((0, 0), (0, r - t_new), (0, 0)))
    o_nsa = _nsa_decode_sel(page_table, q_rows(P['qrot']), pad_rows(P['sel'], 16), pad_rows(P['win'], 16), selm,
                            ocmp, pad_rows(P['gates'], t_pad), feat_major(win_state), feat_major(cache_sel),
                            t_new, t_pad)
    o_nsa = o_nsa[:, :t_new].reshape(n, NSA_WIDTH)

    y = _finish(x2d, o_mla, o_nsa, P['szm'], P['szn'], wts, n)
    wst = win_state.reshape(nb, -1, NSA_KV_WIDTH)
    win_all = jnp.concatenate([wst, P['win'].reshape(nb, t_new, NSA_KV_WIDTH)], axis=1)[:, t_new:]
    return (y.reshape(nb, t_new, D_MODEL),
            P['mrow'].reshape(nb, t_new, MLA_ROW),
            P['cmp'].reshape((nb, t_new) + kvd),
            P['sel'].reshape((nb, t_new) + kvd),
            win_all.reshape((nb, win_all.shape[1]) + kvd))


def kernel(x_prompt, x_sample, cache_mla, cache_cmp_kv, cache_sel_kv, state_win_kv, page_table, w_in, q_norm_g, w_uq,
           kv_norm_g, w_uk, w_uv, cmp_pos_emb, cmp_w1, cmp_b1, cmp_w2, w_o, ln_g, ln_b):
    assert w_in.shape[0] == DEPTH
    wts = _prep_weights(w_in[0], q_norm_g[0], w_uq[0], kv_norm_g[0], w_uk[0], w_uv[0], cmp_pos_emb[0], cmp_w1[0],
                        cmp_b1[0], cmp_w2[0], w_o[0], ln_g[0], ln_b[0])
    yp, p_mla, p_cmp, p_sel, p_win = _prompt_layer(x_prompt, wts)
    drop_depth = lambda a: a.reshape(a.shape[1:])
    ys, s_mla, s_cmp, s_sel, s_win = _sample_layer(x_sample, drop_depth(cache_mla), drop_depth(cache_cmp_kv),
                                                   drop_depth(cache_sel_kv), drop_depth(state_win_kv), page_table, wts)
    add_depth = lambda a: a[None]
    return (yp, ys) + tuple(add_depth(a) for a in (p_mla, p_cmp, p_sel, p_win, s_mla, s_cmp, s_sel, s_win))
```

```python
import functools

import numpy as np
import jax
import jax.numpy as jnp
from jax import lax
from jax.experimental import pallas as pl
from jax.experimental.pallas import tpu as pltpu

F32 = jnp.float32
BF16 = jnp.bfloat16

D_MODEL = 1024
PAGE_SIZE = 128
MLA_HEADS = 8
MLA_V_DIM = 64
MLA_NOPE_DIM = 64
MLA_ROPE_DIM = 32
MLA_Q_LORA = 256
MLA_KV_LORA = 128
MLA_WIDTH = MLA_HEADS * MLA_V_DIM
MLA_ROW = MLA_KV_LORA + MLA_ROPE_DIM
NSA_HEADS = 8
NSA_KV_HEADS = 2
NSA_HEAD_DIM = 64
NSA_GROUP = NSA_HEADS // NSA_KV_HEADS
NSA_WIDTH = NSA_HEADS * NSA_HEAD_DIM
NSA_KV_WIDTH = 2 * NSA_KV_HEADS * NSA_HEAD_DIM
CMP_BLOCK = 32
CMP_STRIDE = 16
SEL_BLOCK = 64
SEL_TOP_N = 16
WINDOW = 512
Q_BLOCK = 128
ROPE_THETA = 10000.0
RMS_EPS = 1e-6
LN_EPS = 1e-5
NEG_INF = -1e30
MLA_SCALE = (MLA_NOPE_DIM + MLA_ROPE_DIM) ** -0.5
NSA_SCALE = NSA_HEAD_DIM ** -0.5
DEPTH = 1
ALPHA = (2 * DEPTH) ** 0.25

LANES = 128
LOG2E = 1.4426950408889634
MLA_KROW = 2 * LANES
MLA_ONES_LANE = MLA_ROW
NSA_KROW = 3 * LANES

SEG_CQ = (0, 256)
SEG_CKV = (256, 128)
SEG_ZMLA = (384, 512)
SEG_QN = (896, 512)
SEG_CMP = (1408, 256)
SEG_SEL = (1664, 256)
SEG_WIN = (1920, 256)
SEG_ZNSA = (2176, 512)
SEG_MISC = (2688, 128)
IN_WIDTH_P = 2816
MISC_GATE0 = MLA_ROPE_DIM

VMEM_LIMIT = 48 * 1024 * 1024

def _cparams(sem, flags=None):
    return pltpu.CompilerParams(dimension_semantics=sem, vmem_limit_bytes=VMEM_LIMIT, flags=flags)


def _sigmoid(x):
    return 1.0 / (1.0 + jnp.exp(-x))


def _silu(x):
    return x * _sigmoid(x)


def _rope_slab(x, c, sa, sb, half):
    return x * c + pltpu.roll(x, LANES - half, 1) * sa + pltpu.roll(x, half, 1) * sb


def _dot_nt(a, b):
    return lax.dot_general(a, b, (((1,), (1,)), ((), ())), preferred_element_type=F32)


def _dot(a, b):
    return jnp.dot(a, b, preferred_element_type=F32)


def _masked_softmax(s, mask):
    sm = jnp.where(mask, s, NEG_INF)
    m = jnp.max(sm, axis=-1, keepdims=True)
    e = jnp.exp2(sm - m)
    p = e / jnp.sum(e, axis=-1, keepdims=True)
    return jnp.where(mask, p, 0.0)


def _attend_segments(parts):
    is_bias = lambda mask: mask is not None and mask.dtype != jnp.bool_
    sm = [s if mask is None else (s + mask if is_bias(mask) else jnp.where(mask, s, NEG_INF))
          for s, mask, _ in parts]
    m = functools.reduce(jnp.maximum, [jnp.max(x, axis=-1, keepdims=True) for x in sm])
    l, acc = 0.0, 0.0
    for x, (_, mask, pv) in zip(sm, parts):
        p = jnp.exp2(x - m)
        if mask is not None and not is_bias(mask):
            p = jnp.where(mask, p, 0.0)
        l = l + jnp.sum(p, axis=-1, keepdims=True)
        acc = acc + pv(p.astype(BF16))
    ok = l > 0.0
    return jnp.where(ok, acc / jnp.where(ok, l, 1.0), 0.0)


def _flash_step(s, bias, v_ext, m, acc):
    sm = s if bias is None else s + bias
    m_new = jnp.maximum(m, jnp.max(sm, axis=-1, keepdims=True))
    p = jnp.exp2(sm - m_new)
    return m_new, jnp.exp2(m - m_new) * acc + _dot(p.astype(BF16), v_ext)


def _flash_result(acc, ones_lane):
    return acc[:, 0:LANES] * (1.0 / acc[:, ones_lane:ones_lane + 1])


def _tile_rows(x, reps):
    return jnp.concatenate([x] * reps, axis=0)


def _select_blocks(imp, q_pos, n_sel_blocks):
    rows, width = imp.shape
    j = lax.broadcasted_iota(jnp.int32, (rows, width), 1)
    cur = lax.shift_right_logical(q_pos, 6)
    avail = (j * SEL_BLOCK <= q_pos) & (j < n_sel_blocks)
    forced = (j == 0) | (j == cur) | (j == cur - 1)
    val = jnp.where(avail, jnp.where(forced, jnp.inf, imp), -jnp.inf)
    rank = jnp.zeros((rows, width), F32)
    for jp in range(n_sel_blocks):
        col = val[:, jp:jp + 1]
        ahead = (col > val) | ((col == val) & (j > jp))
        rank = rank + jnp.where(ahead, 1.0, 0.0)
    keep = (rank < float(min(SEL_TOP_N, n_sel_blocks))) & avail
    return jnp.where(keep, 1.0, 0.0)


def _select_blocks_t(imp_t, q_pos, n_sel_blocks):
    n_rows, width = imp_t.shape
    j = lax.broadcasted_iota(jnp.int32, (n_rows, width), 0)
    cur = lax.shift_right_logical(q_pos, 6)
    avail = (j * SEL_BLOCK <= q_pos) & (j < n_sel_blocks)
    forced = (j == 0) | (j == cur) | (j == cur - 1)
    val = jnp.where(avail, jnp.where(forced, jnp.inf, imp_t), -jnp.inf)
    n_pieces = -(-n_sel_blocks // 8)
    pieces = [val[8 * v:8 * v + 8] for v in range(n_pieces)]
    jrow = lax.broadcasted_iota(jnp.int32, (8, width), 0)
    ranks = [jnp.zeros((8, width), F32) for _ in range(n_pieces)]
    for jp in range(n_sel_blocks):
        cand = jnp.broadcast_to(val[jp:jp + 1], (8, width))
        for v in range(n_pieces):
            if 8 * v > jp:
                ahead = cand >= pieces[v]
            elif 8 * v + 7 < jp:
                ahead = cand > pieces[v]
            else:
                ahead = (cand > pieces[v]) | ((cand == pieces[v]) & (jrow > jp - 8 * v))
            ranks[v] = ranks[v] + jnp.where(ahead, 1.0, 0.0)
    if n_rows > 8 * n_pieces:
        ranks.append(jnp.zeros((n_rows - 8 * n_pieces, width), F32))
    rank = jnp.concatenate(ranks, axis=0)
    keep = (rank < float(min(SEL_TOP_N, n_sel_blocks))) & avail
    return jnp.where(keep, 1.0, 0.0)


def _proj_kernel(x_ref, w_ref, qg_ref, wuq_ref, wuk_ref, kvg_ref, tab_ref,
                 qmla_ref, mrow_ref, mrowb_ref, szm_ref, qn_ref, qrot_ref, cmp_ref, cmps_ref,
                 sel_ref, selb_ref, win_ref, winb_ref, gates_ref, szn_ref, *, feature_major):
    xb = x_ref[...].astype(BF16)

    def seg(s):
        return _dot(xb, w_ref[:, s[0]:s[0] + s[1]])

    cn, san, sbn = tab_ref[0], tab_ref[1], tab_ref[2]
    cm, sam, sbm = tab_ref[3], tab_ref[4], tab_ref[5]

    cq = seg(SEG_CQ)
    r = cq * lax.rsqrt(jnp.mean(cq * cq, axis=-1, keepdims=True) + RMS_EPS) * qg_ref[...]
    q = _dot(r.astype(BF16), wuq_ref[...])
    nope_w = MLA_HEADS * MLA_NOPE_DIM
    lane = lax.broadcasted_iota(jnp.int32, (q.shape[0], LANES), 1)
    for h in range(MLA_HEADS):
        ql = _dot(q[:, h * MLA_NOPE_DIM:(h + 1) * MLA_NOPE_DIM].astype(BF16), wuk_ref[h])
        qmla_ref[h, :, 0:LANES] = (ql * (MLA_SCALE * LOG2E)).astype(BF16)
    heads_per_slab = LANES // MLA_ROPE_DIM
    for jj in range(MLA_HEADS // heads_per_slab):
        qr = _rope_slab(q[:, nope_w + jj * LANES: nope_w + (jj + 1) * LANES], cm, sam, sbm, MLA_ROPE_DIM // 2)
        qr = qr * (MLA_SCALE * LOG2E)
        for hh in range(heads_per_slab):
            front = qr if hh == 0 else pltpu.roll(qr, LANES - hh * MLA_ROPE_DIM, 1)
            qmla_ref[jj * heads_per_slab + hh, :, LANES:2 * LANES] = (
                jnp.where(lane < MLA_ROPE_DIM, front, 0.0).astype(BF16))

    ckv = seg(SEG_CKV)
    lat = ckv * lax.rsqrt(jnp.mean(ckv * ckv, axis=-1, keepdims=True) + RMS_EPS) * kvg_ref[...]
    misc = seg(SEG_MISC)
    kr = _rope_slab(misc, cm, sam, sbm, MLA_ROPE_DIM // 2)
    if feature_major:
        mrow_ref[0:MLA_KV_LORA, :] = lat.T
        mrow_ref[MLA_KV_LORA:MLA_ROW, :] = kr.T[0:MLA_ROPE_DIM]
    else:
        mrow_ref[:, 0:MLA_KV_LORA] = lat
        mrow_ref[:, MLA_KV_LORA:MLA_ROW] = kr[:, 0:MLA_ROPE_DIM]
    mrowb_ref[:, 0:LANES] = lat.astype(BF16)
    ones_col = jnp.where(lane == MLA_ONES_LANE - LANES, 1.0, 0.0)
    mrowb_ref[:, LANES:2 * LANES] = jnp.where(lane < MLA_ROPE_DIM, kr, ones_col).astype(BF16)
    gates_ref[...] = _sigmoid(misc)

    szm_ref[...] = _silu(seg(SEG_ZMLA))
    szn_ref[...] = _silu(seg(SEG_ZNSA))

    qn = seg(SEG_QN)
    for jj in range(NSA_WIDTH // LANES):
        raw = qn[:, jj * LANES:(jj + 1) * LANES] * (NSA_SCALE * LOG2E)
        rot = _rope_slab(qn[:, jj * LANES:(jj + 1) * LANES], cn, san, sbn, NSA_HEAD_DIM // 2) * (NSA_SCALE * LOG2E)
        for src, dst in ((raw, qn_ref), (rot, qrot_ref)):
            swapped = pltpu.roll(src, NSA_HEAD_DIM, 1)
            for half in range(2):
                hd = 2 * jj + half
                g = hd // NSA_GROUP
                keep = (lane >= g * NSA_HEAD_DIM) & (lane < (g + 1) * NSA_HEAD_DIM)
                dst[hd] = jnp.where(keep, src if half == g else swapped, 0.0).astype(BF16)

    def put_rows(dst, first, second):
        if feature_major:
            dst[0:LANES, :] = first.T
            dst[LANES:2 * LANES, :] = second.T
        else:
            dst[:, 0:LANES] = first
            dst[:, LANES:2 * LANES] = second

    cmpv = seg(SEG_CMP)
    put_rows(cmp_ref, cmpv[:, 0:LANES], cmpv[:, LANES:2 * LANES])
    cmps_ref[0] = cmpv[:, 0:LANES]
    cmps_ref[1] = cmpv[:, LANES:2 * LANES]

    for src, dst, dstb in ((SEG_SEL, sel_ref, selb_ref), (SEG_WIN, win_ref, winb_ref)):
        kv = seg(src)
        k = _rope_slab(kv[:, 0:LANES], cn, san, sbn, NSA_HEAD_DIM // 2)
        v = kv[:, LANES:2 * LANES]
        put_rows(dst, k, v)
        dstb[:, 0:LANES] = k.astype(BF16)
        dstb[:, LANES:2 * LANES] = v.astype(BF16)
        dstb[:, 2 * LANES:3 * LANES] = jnp.where(lane == 0, 1.0, 0.0).astype(BF16)


def _rope_tables(pos):
    def tab(d):
        inv = 1.0 / (ROPE_THETA ** (jnp.arange(0, d, 2, dtype=F32) / d))
        ang = pos.astype(F32)[:, None] * inv[None, :]
        cos, sin = jnp.cos(ang), jnp.sin(ang)
        zero = jnp.zeros_like(sin)
        reps = LANES // d
        return (jnp.tile(jnp.concatenate([cos, cos], -1), (1, reps)),
                jnp.tile(jnp.concatenate([-sin, zero], -1), (1, reps)),
                jnp.tile(jnp.concatenate([zero, sin], -1), (1, reps)))
    return jnp.stack(tab(NSA_HEAD_DIM) + tab(MLA_ROPE_DIM))


def _project(x2d, tabs, wts, tm, feature_major=False):
    n = x2d.shape[0]
    p_rows = tabs.shape[1]
    assert n % tm == 0 and p_rows % tm == 0
    nt = p_rows // tm
    row = lambda w: pl.BlockSpec((tm, w), lambda i: (i, 0))
    full = lambda a: pl.BlockSpec(a.shape, lambda i: (0,) * a.ndim)
    if feature_major:
        out_rows = lambda w: ((n // p_rows, w, p_rows), F32, pl.BlockSpec((None, w, tm), lambda i: (i // nt, 0, i % nt)))
    else:
        out_rows = lambda w: ((n, w), F32, row(w))
    out_shapes = dict(
        qmla=((MLA_HEADS, n, MLA_KROW), BF16, pl.BlockSpec((MLA_HEADS, tm, MLA_KROW), lambda i: (0, i, 0))),
        mrow=out_rows(MLA_ROW),
        mrowb=((n, MLA_KROW), BF16, row(MLA_KROW)),
        szm=((n, MLA_WIDTH), F32, row(MLA_WIDTH)),
        qn=((NSA_HEADS, n, LANES), BF16, pl.BlockSpec((NSA_HEADS, tm, LANES), lambda i: (0, i, 0))),
        qrot=((NSA_HEADS, n, LANES), BF16, pl.BlockSpec((NSA_HEADS, tm, LANES), lambda i: (0, i, 0))),
        cmp=out_rows(NSA_KV_WIDTH),
        cmps=((2, n, LANES), F32, pl.BlockSpec((2, tm, LANES), lambda i: (0, i, 0))),
        sel=out_rows(NSA_KV_WIDTH),
        selb=((n, NSA_KROW), BF16, row(NSA_KROW)),
        win=out_rows(NSA_KV_WIDTH),
        winb=((n, NSA_KROW), BF16, row(NSA_KROW)),
        gates=((n, LANES), F32, row(LANES)),
        szn=((n, NSA_WIDTH), F32, row(NSA_WIDTH)),
    )
    names = list(out_shapes)
    outs = pl.pallas_call(
        functools.partial(_proj_kernel, feature_major=feature_major),
        grid=(n // tm,),
        in_specs=[row(D_MODEL), full(wts['w_in']), full(wts['q_norm_g']), full(wts['w_uq']), full(wts['w_uk']),
                  full(wts['kv_norm_g']), pl.BlockSpec((6, tm, LANES), lambda i: (0, i % nt, 0))],
        out_specs=[out_shapes[k][2] for k in names],
        out_shape=[jax.ShapeDtypeStruct(out_shapes[k][0], out_shapes[k][1]) for k in names],
        compiler_params=_cparams(("parallel",)),
        name="projection",
    )(x2d, wts['w_in'], wts['q_norm_g'], wts['w_uq'], wts['w_uk'], wts['kv_norm_g'], tabs)
    return dict(zip(names, outs))


def _uv_project(o, wuv_ref, rows):
    return jnp.concatenate(
        [_dot(o[h * rows:(h + 1) * rows].astype(BF16), wuv_ref[h]) for h in range(MLA_HEADS)], axis=-1)


def _flash_pipeline(n_chunks, prepare, scores, consume, s_a, s_b):
    scores(0, s_a)

    def body(i, _):
        c0 = 2 * i
        aux = prepare(c0)
        scores(c0 + 1, s_b)
        consume(c0, s_a, aux)

        @pl.when(c0 + 1 < n_chunks)
        def _():
            aux = prepare(c0 + 1)
            scores(c0 + 2, s_a)
            consume(c0 + 1, s_b, aux)
        return 0

    lax.fori_loop(0, (n_chunks + 1) // 2, body, 0)


def _mla_prompt_kernel(q_ref, k_ref, wuv_ref, o_ref, s_a, s_b, m_ref, acc_ref, *, tq, tk):
    qi = pl.program_id(1)
    rows = MLA_HEADS * tq
    q = q_ref[...].reshape(rows, MLA_KROW)
    n_chunks = (qi * tq) // tk + 1
    col = lax.broadcasted_iota(jnp.int32, (tq, tk), 1)
    q_pos = qi * tq + lax.broadcasted_iota(jnp.int32, (tq, tk), 0)
    m_ref[...] = jnp.full(m_ref.shape, NEG_INF, F32)
    acc_ref[...] = jnp.zeros(acc_ref.shape, F32)

    def chunk(c):
        return k_ref[pl.ds(pl.multiple_of(c * tk, tk), tk), :]

    def scores(c, dst):
        dst[...] = _dot_nt(q, chunk(jnp.minimum(c, n_chunks - 1)))

    def prepare(c):
        return _tile_rows(jnp.where(c * tk + col <= q_pos, 0.0, NEG_INF), MLA_HEADS)

    def consume(c, src, bias):
        m_ref[...], acc_ref[...] = _flash_step(src[...], bias, chunk(c), m_ref[...], acc_ref[...])

    _flash_pipeline(n_chunks, prepare, scores, consume, s_a, s_b)
    o_ref[...] = _uv_project(_flash_result(acc_ref[...], MLA_ONES_LANE), wuv_ref, tq)


def _mla_prompt(qmla, mrowb, wuv, batch, seq):
    tq = min(256, seq)
    tk = min(512, seq)
    nq = seq // tq
    assert tk % tq == 0 and seq % tk == 0
    return pl.pallas_call(
        functools.partial(_mla_prompt_kernel, tq=tq, tk=tk),
        grid=(batch, nq),
        in_specs=[pl.BlockSpec((MLA_HEADS, tq, MLA_KROW), lambda b, i: (0, b * nq + i, 0)),
                  pl.BlockSpec((None, seq, MLA_KROW), lambda b, i: (b, 0, 0)),
                  pl.BlockSpec(wuv.shape, lambda b, i: (0, 0, 0))],
        out_specs=pl.BlockSpec((tq, MLA_WIDTH), lambda b, i: (b * nq + i, 0)),
        out_shape=jax.ShapeDtypeStruct((batch * seq, MLA_WIDTH), F32),
        scratch_shapes=[pltpu.VMEM((MLA_HEADS * tq, tk), F32), pltpu.VMEM((MLA_HEADS * tq, tk), F32),
                        pltpu.VMEM((MLA_HEADS * tq, 1), F32), pltpu.VMEM((MLA_HEADS * tq, MLA_KROW), F32)],
        compiler_params=_cparams(("parallel", "parallel")),
        name="mla_prompt",
    )(qmla, mrowb.reshape(batch, seq, MLA_KROW), wuv)


CMP_PAIRS = CMP_STRIDE // 2
CHUNK_PITCH = 24


def _compress_slab(load_pair, wc1_ref, pe_ref, b1_ref, w2_ref, s, n_chunk):
    proj = jnp.zeros((n_chunk, 2 * LANES), F32)
    pe = jnp.zeros((8, 2 * LANES), F32)
    for j in range(CMP_PAIRS):
        w = wc1_ref[s, j]
        proj = proj + _dot(load_pair(j).astype(BF16), w)
        pe = pe + _dot(pe_ref[s, j].astype(BF16), w)
    bias = b1_ref[s] + pe[0:1, 0:LANES] + pe[1:2, LANES:2 * LANES]
    hid = proj[:, 0:LANES] + pltpu.roll(proj[:, LANES:2 * LANES], n_chunk - 1, 0) + bias
    return _dot(_silu(hid).astype(BF16), w2_ref[s])


def _compress_prompt_kernel(x_ref, wc1_ref, pe_ref, b1_ref, w2_ref, o_ref, *, n_chunk):
    for s in range(2):
        row = lambda p, s=s: x_ref[s, pl.ds(p, n_chunk, stride=CMP_STRIDE), :]
        load = lambda j, row=row: jnp.concatenate([row(2 * j), row(2 * j + 1)], axis=-1)
        o_ref[s] = _compress_slab(load, wc1_ref, pe_ref, b1_ref, w2_ref, s, n_chunk).astype(BF16)


def _compress_prompt(cmps, wts, batch, seq):
    n_chunk = seq // CMP_STRIDE
    full = lambda a: pl.BlockSpec(a.shape, lambda b: (0,) * a.ndim)
    return pl.pallas_call(
        functools.partial(_compress_prompt_kernel, n_chunk=n_chunk),
        grid=(batch,),
        in_specs=[pl.BlockSpec((2, seq, LANES), lambda b: (0, b, 0)),
                  full(wts['wc1']), full(wts['pe']), full(wts['b1']), full(wts['w2'])],
        out_specs=pl.BlockSpec((None, 2, n_chunk, LANES), lambda b: (b, 0, 0, 0)),
        out_shape=jax.ShapeDtypeStruct((batch, 2, n_chunk, LANES), BF16),
        compiler_params=_cparams(("parallel",)),
        name="compress_prompt",
    )(cmps, wts['wc1'], wts['pe'], wts['b1'], wts['w2'])


def _cmp_branch(q, kc, vc, q_pos_rows, n_cmp_blocks, impmap_ref, rows_t, imp_transposed=False):
    s = _dot_nt(q, kc)
    n = lax.broadcasted_iota(jnp.int32, (rows_t, s.shape[1]), 1)
    visible = (n * CMP_STRIDE + (CMP_BLOCK - 1) <= q_pos_rows[0:rows_t]) & (n < n_cmp_blocks)
    sm = s + _tile_rows(jnp.where(visible, 0.0, NEG_INF), NSA_HEADS)
    e = jnp.exp2(sm - jnp.max(sm, axis=-1, keepdims=True))
    any_visible = (q_pos_rows >= CMP_BLOCK - 1) & (n_cmp_blocks > 0)
    p = e * jnp.where(any_visible, 1.0 / jnp.sum(e, axis=-1, keepdims=True), 0.0)
    o_cmp = _dot(p.astype(BF16), vc)
    imps = []
    for g in range(NSA_KV_HEADS):
        base = g * NSA_GROUP * rows_t
        psum = p[base:base + rows_t]
        for hh in range(1, NSA_GROUP):
            psum = psum + p[base + hh * rows_t: base + (hh + 1) * rows_t]
        hi = psum.astype(BF16)
        lo = (psum - hi.astype(F32)).astype(BF16)
        if imp_transposed:
            imps.append(_dot_nt(impmap_ref[...], hi) + _dot_nt(impmap_ref[...], lo))
        else:
            imps.append(_dot(hi, impmap_ref[...]) + _dot(lo, impmap_ref[...]))
    return o_cmp, imps


def _mix_heads(gates, o_cmp, o_sel, o_win, rows_t):
    outs = []
    for hd in range(NSA_HEADS):
        g = hd // NSA_GROUP
        rs = slice(hd * rows_t, (hd + 1) * rows_t)
        ls = slice(g * NSA_HEAD_DIM, (g + 1) * NSA_HEAD_DIM)
        gcol = lambda br: gates[:, MISC_GATE0 + br * NSA_HEADS + hd: MISC_GATE0 + br * NSA_HEADS + hd + 1]
        outs.append(gcol(0) * o_cmp[rs, ls] + gcol(1) * o_sel[rs, ls] + gcol(2) * o_win[rs, ls])
    return jnp.concatenate(outs, axis=-1)


def _group_rows(per_group, reps):
    return jnp.concatenate([per_group[g] for g in range(NSA_KV_HEADS) for _ in range(reps)], axis=0)


def _nsa_prompt_kernel(qn_ref, qrot_ref, kvc_ref, sel_ref, win_ref, gates_ref, impmap_ref, o_ref,
                       s_a, s_b, m_ref, acc_ref, *, tq, tk, n_cmp_blocks, n_sel_blocks, wlen):
    qi = pl.program_id(1)
    rows = NSA_HEADS * tq
    q_pos_t = qi * tq + lax.broadcasted_iota(jnp.int32, (tq, 1), 0)
    q_pos_r = qi * tq + (lax.broadcasted_iota(jnp.int32, (rows, 1), 0) & (tq - 1))
    qn = qn_ref[...].reshape(rows, LANES)
    qrot = qrot_ref[...].reshape(rows, LANES)
    init = (jnp.full((rows, 1), NEG_INF, F32), jnp.zeros((rows, 2 * LANES), F32))

    w_start = pl.multiple_of(jnp.maximum(qi * tq + tq - wlen, 0), tq)
    d = q_pos_t - (w_start + lax.broadcasted_iota(jnp.int32, (tq, wlen), 1))
    bias = _tile_rows(jnp.where((d >= 0) & (d < WINDOW), 0.0, NEG_INF), NSA_HEADS)
    _, acc = _flash_step(_dot_nt(qrot, win_ref[pl.ds(w_start, wlen), 0:LANES]), bias,
                         win_ref[pl.ds(w_start, wlen), LANES:3 * LANES], *init)
    o_win = _flash_result(acc, LANES)

    o_cmp, imps_t = _cmp_branch(qn, kvc_ref[0], kvc_ref[1], q_pos_r, n_cmp_blocks, impmap_ref, tq,
                                imp_transposed=True)
    q_pos_lane = qi * tq + lax.broadcasted_iota(jnp.int32, (1, tq), 1)
    selms = [_select_blocks_t(imp_t, q_pos_lane, n_sel_blocks).T.astype(BF16) for imp_t in imps_t]

    blk_per_chunk = tk // SEL_BLOCK
    n_chunks = (qi * tq + tq + tk - 1) // tk
    e_row = lax.broadcasted_iota(jnp.int32, (LANES, tk), 0)
    e_col = lax.shift_right_logical(lax.broadcasted_iota(jnp.int32, (LANES, tk), 1), 6)
    col_t = lax.broadcasted_iota(jnp.int32, (tq, tk), 1)

    def sel_scores(c, dst):
        start = pl.multiple_of(jnp.minimum(c, n_chunks - 1) * tk, tk)
        dst[...] = _dot_nt(qrot, sel_ref[pl.ds(start, tk), 0:LANES])

    def sel_prepare(c):
        expand = jnp.where(e_row == c * blk_per_chunk + e_col, 1.0, 0.0).astype(BF16)
        causal = (c * tk + col_t) <= q_pos_t
        return _group_rows([jnp.where((_dot(sm, expand) > 0.5) & causal, 0.0, NEG_INF) for sm in selms], NSA_GROUP)

    def sel_consume(c, src, bias):
        v_ext = sel_ref[pl.ds(pl.multiple_of(c * tk, tk), tk), LANES:3 * LANES]
        m_ref[...], acc_ref[...] = _flash_step(src[...], bias, v_ext, m_ref[...], acc_ref[...])

    m_ref[...], acc_ref[...] = init
    _flash_pipeline(n_chunks, sel_prepare, sel_scores, sel_consume, s_a, s_b)
    o_sel = _flash_result(acc_ref[...], LANES)

    o_ref[...] = _mix_heads(gates_ref[...], o_cmp, o_sel, o_win, tq)


def _importance_map(nc_rows, nc, ns, width):
    i = np.arange(nc_rows)[:, None]
    j = np.arange(width)[None, :]
    lo = np.maximum(i * CMP_STRIDE, j * SEL_BLOCK)
    hi = np.minimum(i * CMP_STRIDE + CMP_BLOCK, (j + 1) * SEL_BLOCK)
    m = np.maximum(hi - lo, 0).astype(np.float32) / CMP_BLOCK
    m = m * (i < nc) * (j < ns)
    return jnp.asarray(m, dtype=BF16)


def _nsa_prompt(P, kvc, batch, seq):
    tq = min(256, seq)
    tk = min(512, seq)
    nq = seq // tq
    n_chunk = seq // CMP_STRIDE
    nc = n_chunk - CMP_BLOCK // CMP_STRIDE + 1
    ns = seq // SEL_BLOCK
    assert ns <= LANES and seq % tk == 0 and seq % CMP_STRIDE == 0 and seq % SEL_BLOCK == 0
    wlen = min(WINDOW + tq, seq)
    impmap = _importance_map(n_chunk, nc, ns, LANES).T
    blk = lambda w: pl.BlockSpec((tq, w), lambda b, i: (b * nq + i, 0))
    qblk = pl.BlockSpec((NSA_HEADS, tq, LANES), lambda b, i: (0, b * nq + i, 0))
    per_b = lambda a: pl.BlockSpec((None,) + a.shape[1:], lambda b, i: (b,) + (0,) * (a.ndim - 1))
    selb = P['selb'].reshape(batch, seq, NSA_KROW)
    winb = P['winb'].reshape(batch, seq, NSA_KROW)
    return pl.pallas_call(
        functools.partial(_nsa_prompt_kernel, tq=tq, tk=tk, n_cmp_blocks=nc, n_sel_blocks=ns, wlen=wlen),
        grid=(batch, nq),
        in_specs=[qblk, qblk, per_b(kvc), per_b(selb), per_b(winb), blk(LANES),
                  pl.BlockSpec(impmap.shape, lambda b, i: (0, 0))],
        out_specs=blk(NSA_WIDTH),
        out_shape=jax.ShapeDtypeStruct((batch * seq, NSA_WIDTH), F32),
        scratch_shapes=[pltpu.VMEM((NSA_HEADS * tq, tk), F32), pltpu.VMEM((NSA_HEADS * tq, tk), F32),
                        pltpu.VMEM((NSA_HEADS * tq, 1), F32), pltpu.VMEM((NSA_HEADS * tq, 2 * LANES), F32)],
        compiler_params=_cparams(("parallel", "parallel")),
        name="nsa_prompt",
    )(P['qn'], P['qrot'], kvc, selb, winb, P['gates'], impmap)


def _finish_kernel(x_ref, omla_ref, onsa_ref, szm_ref, szn_ref, wo_ref, g_ref, b_ref, y_ref):
    mixed = jnp.concatenate([omla_ref[...] * szm_ref[...], onsa_ref[...] * szn_ref[...]], axis=-1)
    h = ALPHA * x_ref[...] + _dot(mixed.astype(BF16), wo_ref[...])
    mu = jnp.mean(h, axis=-1, keepdims=True)
    d = h - mu
    var = jnp.mean(d * d, axis=-1, keepdims=True)
    y_ref[...] = d * lax.rsqrt(var + LN_EPS) * g_ref[...] + b_ref[...]


def _finish(x2d, omla, onsa, szm, szn, wts, tm):
    n = x2d.shape[0]
    row = lambda w: pl.BlockSpec((tm, w), lambda i: (i, 0))
    full = lambda a: pl.BlockSpec(a.shape, lambda i: (0,) * a.ndim)
    return pl.pallas_call(
        _finish_kernel,
        grid=(n // tm,),
        in_specs=[row(D_MODEL), row(MLA_WIDTH), row(NSA_WIDTH), row(MLA_WIDTH), row(NSA_WIDTH),
                  full(wts['w_o']), full(wts['ln_g']), full(wts['ln_b'])],
        out_specs=row(D_MODEL),
        out_shape=jax.ShapeDtypeStruct((n, D_MODEL), F32),
        compiler_params=_cparams(("parallel",)),
        name="finish",
    )(x2d, omla, onsa, szm, szn, wts['w_o'], wts['ln_g'], wts['ln_b'])


def _page_copies(pt_ref, b, n_pages, make_copy):
    assert n_pages % 2 == 0

    def start(i, _):
        make_copy(pt_ref[b, 2 * i], 2 * i).start(priority=0)
        make_copy(pt_ref[b, 2 * i + 1], 2 * i + 1).start(priority=1)
        return 0
    lax.fori_loop(0, n_pages // 2, start, 0, unroll=4)


def _page_wait_all(pool_ref, slot_buf, sem):
    n_pages = slot_buf.shape[0]
    pltpu.make_async_copy(pool_ref.at[pl.ds(0, n_pages)], slot_buf, sem).wait()


def _mla_decode_kernel(pt_ref, q_ref, knew_ref, pool_ref, o_ref, kbuf, sem, *, n_pages, t_new, tk):
    b = pl.program_id(0)
    nb = pl.num_programs(0)
    slot = b % 2
    past = n_pages * PAGE_SIZE

    def copy(sl):
        return lambda page, p: pltpu.make_async_copy(pool_ref.at[page], kbuf.at[sl, p], sem.at[sl])

    @pl.when(b == 0)
    def _():
        _page_copies(pt_ref, 0, n_pages, copy(0))

    @pl.when(b + 1 < nb)
    def _():
        _page_copies(pt_ref, b + 1, n_pages, copy(1 - slot))

    _page_wait_all(pool_ref, kbuf.at[slot], sem.at[slot])

    q = q_ref[...]
    rows = q.shape[0]
    pages_per_chunk = tk // PAGE_SIZE
    parts = []
    for c in range(n_pages // pages_per_chunk):
        kt = jnp.concatenate([kbuf[slot, c * pages_per_chunk + i].astype(BF16) for i in range(pages_per_chunk)],
                             axis=1)
        parts.append((_dot(q, kt), None, lambda p, kt=kt: _dot_nt(p, kt[0:MLA_KV_LORA])))
    kn = knew_ref[...]
    t_q = lax.broadcasted_iota(jnp.int32, (rows, kn.shape[0]), 0) & (t_new - 1)
    t_k = lax.broadcasted_iota(jnp.int32, (rows, kn.shape[0]), 1)
    parts.append((_dot_nt(q, kn), t_k <= t_q, lambda p: _dot(p, kn[:, 0:MLA_KV_LORA])))
    o_ref[...] = _attend_segments(parts)


def _mla_decode(page_table, q_b, knew_b, pool):
    nb, n_pages = page_table.shape
    rows = q_b.shape[1]
    t_new = rows // MLA_HEADS
    tk = 1024
    past = n_pages * PAGE_SIZE
    assert past % tk == 0 and (t_new & (t_new - 1)) == 0
    grid_spec = pltpu.PrefetchScalarGridSpec(
        num_scalar_prefetch=1,
        grid=(nb,),
        in_specs=[pl.BlockSpec((None, rows, MLA_ROW), lambda b, pt: (b, 0, 0)),
                  pl.BlockSpec((None,) + knew_b.shape[1:], lambda b, pt: (b, 0, 0)),
                  pl.BlockSpec(memory_space=pl.ANY)],
        out_specs=pl.BlockSpec((None, rows, MLA_KV_LORA), lambda b, pt: (b, 0, 0)),
        scratch_shapes=[pltpu.VMEM((2, n_pages, MLA_ROW, PAGE_SIZE), F32), pltpu.SemaphoreType.DMA((2,))],
    )
    return pl.pallas_call(
        functools.partial(_mla_decode_kernel, n_pages=n_pages, t_new=t_new, tk=tk),
        grid_spec=grid_spec,
        out_shape=jax.ShapeDtypeStruct((nb, rows, MLA_KV_LORA), F32),
        compiler_params=_cparams(("arbitrary",)),
        name="mla_decode",
    )(page_table, q_b, knew_b, pool)


def _uv_kernel(o_ref, wuv_ref, y_ref):
    rows = o_ref.shape[1]
    y_ref[...] = _uv_project(o_ref[...].reshape(MLA_HEADS * rows, MLA_KV_LORA), wuv_ref, rows)


def _uv(o_heads, wuv):
    rows = o_heads.shape[1]
    return pl.pallas_call(
        _uv_kernel,
        out_shape=jax.ShapeDtypeStruct((rows, MLA_WIDTH), F32),
        name="mla_value_up",
    )(o_heads, wuv)


def _nsa_dec_cmp_kernel(pt_ref, qn_ref, cnew_ref, pool_ref, wc1_ref, pe_ref, b1_ref, w2_ref, impmap_ref,
                        ocmp_ref, selm_ref, pbuf, xrows_k, xrows_v, sem, *, n_pages, t_new, t_pad, n_chunk,
                        n_cmp_blocks, n_sel_blocks):
    b = pl.program_id(0)
    nb = pl.num_programs(0)
    slot = b % 2
    past = n_pages * PAGE_SIZE

    def copy(sl):
        return lambda page, p: pltpu.make_async_copy(pool_ref.at[page], pbuf.at[sl, p], sem.at[sl])

    @pl.when(b == 0)
    def _():
        _page_copies(pt_ref, 0, n_pages, copy(0))

    @pl.when(b + 1 < nb)
    def _():
        _page_copies(pt_ref, b + 1, n_pages, copy(1 - slot))

    _page_wait_all(pool_ref, pbuf.at[slot], sem.at[slot])

    chunks_per_page = PAGE_SIZE // CMP_STRIDE
    xrows = (xrows_k, xrows_v)

    def relayout(p, s):
        r0 = p * (chunks_per_page * CHUNK_PITCH)
        if not isinstance(p, int):
            r0 = pl.multiple_of(r0, 8)
        t = pbuf[slot, p, s * LANES:(s + 1) * LANES, :].astype(BF16).T.astype(F32)
        for c in range(chunks_per_page):
            xrows[s][pl.ds(r0 + c * CHUNK_PITCH, CMP_STRIDE), :] = t[c * CMP_STRIDE:(c + 1) * CMP_STRIDE]

    cnew = cnew_ref[...]
    base = n_pages * chunks_per_page * CHUNK_PITCH
    tail = xrows_k.shape[0] - base
    for s in range(2):
        xrows[s][pl.ds(base, tail), :] = jnp.zeros((tail, LANES), F32)
        xrows[s][pl.ds(base, t_new), :] = cnew[:, s * LANES:(s + 1) * LANES]

    def compress(s):
        row = lambda p: xrows[s][pl.ds(p, n_chunk, stride=CHUNK_PITCH), :]
        load = lambda j: jnp.concatenate([row(2 * j), row(2 * j + 1)], axis=-1)
        return _compress_slab(load, wc1_ref, pe_ref, b1_ref, w2_ref, s, n_chunk).astype(BF16)

    def relayout_keys(p, _):
        relayout(p, 0)
        return 0

    lax.fori_loop(0, n_pages, relayout_keys, 0, unroll=8)
    for p in range(n_pages):
        relayout(p, 1)
    kvc = [compress(0), compress(1)]

    rows = NSA_HEADS * t_pad
    q_pos_r = past + (lax.broadcasted_iota(jnp.int32, (rows, 1), 0) & (t_pad - 1))
    q_pos_t = past + lax.broadcasted_iota(jnp.int32, (t_pad, 1), 0)
    o_cmp, imps = _cmp_branch(qn_ref[...], kvc[0][0:n_cmp_blocks], kvc[1][0:n_cmp_blocks], q_pos_r,
                              n_cmp_blocks, impmap_ref, t_pad)
    ocmp_ref[...] = o_cmp
    selm_ref[...] = _group_rows([_select_blocks(imp, q_pos_t, n_sel_blocks) for imp in imps], NSA_GROUP)


def _nsa_decode_cmp(page_table, qn_b, cnew_b, pool, wts, t_new, t_pad):
    nb, n_pages = page_table.shape
    past = n_pages * PAGE_SIZE
    total = past + t_new
    n_chunk_true = -(-total // CMP_STRIDE)
    nc = n_chunk_true - CMP_BLOCK // CMP_STRIDE + 1
    n_chunk = -(-n_chunk_true // 8) * 8
    assert nc % LANES == 0, "compressed-block count must be lane aligned"
    ns = past // SEL_BLOCK + (-(-t_new // SEL_BLOCK))
    sel_w = -(-ns // LANES) * LANES
    impmap = _importance_map(nc, nc, ns, sel_w)
    rows = NSA_HEADS * t_pad
    full = lambda a: pl.BlockSpec(a.shape, lambda b, pt: (0,) * a.ndim)
    per_b = lambda a: pl.BlockSpec((None,) + a.shape[1:], lambda b, pt: (b,) + (0,) * (a.ndim - 1))
    grid_spec = pltpu.PrefetchScalarGridSpec(
        num_scalar_prefetch=1,
        grid=(nb,),
        in_specs=[per_b(qn_b), per_b(cnew_b), pl.BlockSpec(memory_space=pl.ANY),
                  full(wts['wc1']), full(wts['pe']), full(wts['b1']), full(wts['w2']), full(impmap)],
        out_specs=[pl.BlockSpec((None, rows, LANES), lambda b, pt: (b, 0, 0)),
                   pl.BlockSpec((None, rows, sel_w), lambda b, pt: (b, 0, 0))],
        scratch_shapes=[pltpu.VMEM((2, n_pages, NSA_KV_WIDTH, PAGE_SIZE), F32),
                        pltpu.VMEM((n_chunk * CHUNK_PITCH, LANES), F32),
                        pltpu.VMEM((n_chunk * CHUNK_PITCH, LANES), F32),
                        pltpu.SemaphoreType.DMA((2,))],
    )
    return pl.pallas_call(
        functools.partial(_nsa_dec_cmp_kernel, n_pages=n_pages, t_new=t_new, t_pad=t_pad, n_chunk=n_chunk,
                          n_cmp_blocks=nc, n_sel_blocks=ns),
        grid_spec=grid_spec,
        out_shape=[jax.ShapeDtypeStruct((nb, rows, LANES), F32),
                   jax.ShapeDtypeStruct((nb, rows, sel_w), F32)],
        compiler_params=_cparams(("arbitrary",)),
        name="nsa_decode_compress",
    )(page_table, qn_b, cnew_b, pool, wts['wc1'], wts['pe'], wts['b1'], wts['w2'], impmap)


def _nsa_dec_sel_kernel(pt_ref, qr_ref, snew_ref, wnew_ref, selm_ref, ocmp_ref, gates_ref, wst_ref, pool_ref,
                        o_ref, kbuf, sem, *, n_pages, t_new, t_pad, tk):
    b = pl.program_id(0)
    nb = pl.num_programs(0)
    slot = b % 2
    past = n_pages * PAGE_SIZE
    rows = NSA_HEADS * t_pad

    def copy(sl):
        return lambda page, p: pltpu.make_async_copy(pool_ref.at[page], kbuf.at[sl, p], sem.at[sl])

    @pl.when(b == 0)
    def _():
        _page_copies(pt_ref, 0, n_pages, copy(0))

    @pl.when(b + 1 < nb)
    def _():
        _page_copies(pt_ref, b + 1, n_pages, copy(1 - slot))

    _page_wait_all(pool_ref, kbuf.at[slot], sem.at[slot])

    blk_per_chunk = tk // SEL_BLOCK
    lane_lo = lax.broadcasted_iota(jnp.int32, (t_pad, LANES), 1) < SEL_BLOCK
    n_new = snew_ref.shape[0]
    t_q = lax.broadcasted_iota(jnp.int32, (rows, n_new), 0) & (t_pad - 1)
    t_k = lax.broadcasted_iota(jnp.int32, (rows, n_new), 1)
    new_causal = (t_k <= t_q) & (t_k < t_new)
    w_buf = wst_ref.shape[1]
    t_qw = lax.broadcasted_iota(jnp.int32, (rows, w_buf), 0) & (t_pad - 1)
    i_w = lax.broadcasted_iota(jnp.int32, (rows, w_buf), 1)
    win_mask = (t_qw + w_buf - i_w) < WINDOW
    snew = snew_ref[...]
    wnew = wnew_ref[...]
    qr = qr_ref[...]
    selm = selm_ref[...]
    sel_groups = [selm[g * NSA_GROUP * t_pad: g * NSA_GROUP * t_pad + t_pad] for g in range(NSA_KV_HEADS)]

    parts = []
    pages_per_chunk = tk // PAGE_SIZE
    for c in range(past // tk):
        chunk_pages = range(c * pages_per_chunk, (c + 1) * pages_per_chunk)
        kt = jnp.concatenate([kbuf[slot, p, 0:LANES, :].astype(BF16) for p in chunk_pages], axis=1)
        vt = jnp.concatenate([kbuf[slot, p, LANES:2 * LANES, :].astype(BF16) for p in chunk_pages], axis=1)
        biases = []
        for sel_g in sel_groups:
            pieces = []
            for i in range(tk // LANES):
                jb = c * blk_per_chunk + 2 * i
                pieces.append(jnp.where(lane_lo, sel_g[:, jb:jb + 1], sel_g[:, jb + 1:jb + 2]))
            biases.append(jnp.where(jnp.concatenate(pieces, axis=-1) > 0.5, 0.0, NEG_INF))
        parts.append((_dot(qr, kt), _group_rows(biases, NSA_GROUP), lambda p, vt=vt: _dot_nt(p, vt)))
    nb_past = past // SEL_BLOCK
    mask = (selm[:, nb_past:nb_past + 1] > 0.5) & new_causal
    parts.append((_dot_nt(qr, snew[:, 0:LANES].astype(BF16)), mask,
                  lambda p: _dot(p, snew[:, LANES:2 * LANES].astype(BF16))))
    o_sel = _attend_segments(parts)

    o_win = _attend_segments([
        (_dot(qr, wst_ref[0:LANES, :].astype(BF16)), win_mask,
         lambda p: _dot_nt(p, wst_ref[LANES:2 * LANES, :].astype(BF16))),
        (_dot_nt(qr, wnew[:, 0:LANES].astype(BF16)), new_causal,
         lambda p: _dot(p, wnew[:, LANES:2 * LANES].astype(BF16)))])

    o_ref[...] = _mix_heads(gates_ref[...], ocmp_ref[...], o_sel, o_win, t_pad)


def _nsa_decode_sel(page_table, qr_b, snew_b, wnew_b, selm, ocmp, gates_b, win_state, pool, t_new, t_pad):
    nb, n_pages = page_table.shape
    past = n_pages * PAGE_SIZE
    tk = 1024
    assert past % tk == 0
    per_b = lambda a: pl.BlockSpec((None,) + a.shape[1:], lambda b, pt: (b,) + (0,) * (a.ndim - 1))
    grid_spec = pltpu.PrefetchScalarGridSpec(
        num_scalar_prefetch=1,
        grid=(nb,),
        in_specs=[per_b(qr_b), per_b(snew_b), per_b(wnew_b), per_b(selm), per_b(ocmp), per_b(gates_b),
                  per_b(win_state), pl.BlockSpec(memory_space=pl.ANY)],
        out_specs=pl.BlockSpec((None, t_pad, NSA_WIDTH), lambda b, pt: (b, 0, 0)),
        scratch_shapes=[pltpu.VMEM((2, n_pages, NSA_KV_WIDTH, PAGE_SIZE), F32), pltpu.SemaphoreType.DMA((2,))],
    )
    return pl.pallas_call(
        functools.partial(_nsa_dec_sel_kernel, n_pages=n_pages, t_new=t_new, t_pad=t_pad, tk=tk),
        grid_spec=grid_spec,
        out_shape=jax.ShapeDtypeStruct((nb, t_pad, NSA_WIDTH), F32),
        compiler_params=_cparams(("arbitrary",)),
        name="nsa_decode_select_window",
    )(page_table, qr_b, snew_b, wnew_b, selm, ocmp, gates_b, win_state, pool)


def _prep_weights(w_in, q_norm_g, w_uq, kv_norm_g, w_uk, w_uv, cmp_pos_emb, cmp_w1, cmp_b1, cmp_w2, w_o, ln_g, ln_b):
    cuts = np.cumsum([MLA_Q_LORA, MLA_KV_LORA, MLA_ROPE_DIM, MLA_WIDTH, NSA_WIDTH, NSA_KV_WIDTH, NSA_KV_WIDTH,
                      NSA_KV_WIDTH, 3 * NSA_HEADS])[:].tolist()
    c_q, c_kv, k_r, z_mla, q_n, cmp_kv, sel_kv, win_kv, g_br, z_nsa = jnp.split(w_in, cuts, axis=1)
    pad = jnp.zeros((D_MODEL, LANES - MLA_ROPE_DIM - 3 * NSA_HEADS), w_in.dtype)
    w_in_p = jnp.concatenate([c_q, c_kv, z_mla, q_n, cmp_kv, sel_kv, win_kv, z_nsa, k_r, g_br, pad], axis=1)
    assert w_in_p.shape[1] == IN_WIDTH_P
    uq = w_uq.reshape(MLA_Q_LORA, MLA_HEADS, MLA_NOPE_DIM + MLA_ROPE_DIM)
    w_uq_p = jnp.concatenate([uq[:, :, :MLA_NOPE_DIM].reshape(MLA_Q_LORA, -1),
                              uq[:, :, MLA_NOPE_DIM:].reshape(MLA_Q_LORA, -1)], axis=1)
    w_uk_t = jnp.transpose(w_uk.reshape(MLA_KV_LORA, MLA_HEADS, MLA_NOPE_DIM), (1, 2, 0))
    w_uv_h = jnp.transpose(w_uv.reshape(MLA_KV_LORA, MLA_HEADS, MLA_V_DIM), (1, 0, 2))
    ratio = CMP_BLOCK // CMP_STRIDE
    eye = jnp.eye(NSA_KV_HEADS, dtype=w_in.dtype)
    w1r = cmp_w1.reshape(2, ratio, CMP_STRIDE, NSA_HEAD_DIM, NSA_HEAD_DIM)
    wc1 = jnp.einsum('gh,srpde->spgdrhe', eye, w1r).reshape(2, CMP_PAIRS, 2 * LANES, ratio * LANES)
    w2 = jnp.einsum('gh,sde->sgdhe', eye, cmp_w2).reshape(2, LANES, LANES)
    pe = jnp.transpose(cmp_pos_emb, (1, 0, 2))
    pe = jnp.concatenate([pe, pe], axis=-1).reshape(2, ratio, CMP_PAIRS, 2 * LANES)
    pe = jnp.transpose(pe, (0, 2, 1, 3))
    pe = jnp.pad(pe, ((0, 0), (0, 0), (0, 8 - ratio), (0, 0)))
    b1 = jnp.concatenate([cmp_b1, cmp_b1], axis=-1).reshape(2, 1, LANES)
    return dict(w_in=w_in_p.astype(BF16), q_norm_g=q_norm_g.reshape(1, -1), w_uq=w_uq_p.astype(BF16),
                w_uk=w_uk_t.astype(BF16), kv_norm_g=kv_norm_g.reshape(1, -1), w_uv=w_uv_h.astype(BF16),
                wc1=wc1.astype(BF16), w2=w2.astype(BF16), pe=pe, b1=b1,
                w_o=w_o.astype(BF16), ln_g=ln_g.reshape(1, -1), ln_b=ln_b.reshape(1, -1))


def _prompt_layer(x, wts):
    batch, seq, _ = x.shape
    x2d = x.reshape(batch * seq, D_MODEL)
    tm = min(512, seq)
    P = _project(x2d, _rope_tables(jnp.arange(seq, dtype=jnp.int32)), wts, tm, feature_major=True)
    o_mla = _mla_prompt(P['qmla'], P['mrowb'], wts['w_uv'], batch, seq)
    kvc = _compress_prompt(P['cmps'], wts, batch, seq)
    o_nsa = _nsa_prompt(P, kvc, batch, seq)
    y = _finish(x2d, o_mla, o_nsa, P['szm'], P['szn'], wts, tm)
    kvd = (2, NSA_KV_HEADS, NSA_HEAD_DIM)
    w_keep = min(WINDOW, seq)
    token_major = lambda a: jnp.swapaxes(a, 1, 2)
    return (y.reshape(batch, seq, D_MODEL),
            token_major(P['mrow']),
            token_major(P['cmp']).reshape((batch, seq) + kvd),
            token_major(P['sel']).reshape((batch, seq) + kvd),
            token_major(P['win'][:, :, seq - w_keep:]).reshape((batch, w_keep) + kvd))


def _sample_layer(x, cache_mla, cache_cmp, cache_sel, win_state, page_table, wts):
    nb, t_new, _ = x.shape
    n_pages = page_table.shape[1]
    past = n_pages * PAGE_SIZE
    n = nb * t_new
    t_pad = 8
    x2d = x.reshape(n, D_MODEL)
    pos = past + (jnp.arange(n, dtype=jnp.int32) % t_new)
    P = _project(x2d, _rope_tables(pos), wts, n)
    kvd = (2, NSA_KV_HEADS, NSA_HEAD_DIM)
    n_pool = cache_mla.shape[0]

    q_b = jnp.transpose(P['qmla'][:, :, :MLA_ROW].reshape(MLA_HEADS, nb, t_new, MLA_ROW), (1, 0, 2, 3))
    q_b = q_b.reshape(nb, -1, MLA_ROW)
    knew_b = jnp.pad(P['mrowb'][:, :MLA_ROW].reshape(nb, t_new, MLA_ROW), ((0, 0), (0, 16 - t_new), (0, 0)))
    feat_major = lambda a: jnp.swapaxes(a.reshape(a.shape[0], a.shape[1], -1), 1, 2)
    o_lat = _mla_decode(page_table, q_b, knew_b, feat_major(cache_mla))
    o_heads = jnp.transpose(o_lat.reshape(nb, MLA_HEADS, t_new, MLA_KV_LORA), (1, 0, 2, 3)).reshape(MLA_HEADS, n, -1)
    o_mla = _uv(o_heads, wts['w_uv'])

    def q_rows(q):
        q = jnp.transpose(q.reshape(NSA_HEADS, nb, t_new, LANES), (1, 0, 2, 3))
        q = jnp.pad(q, ((0, 0), (0, 0), (0, t_pad - t_new), (0, 0)))
        return q.reshape(nb, NSA_HEADS * t_pad, LANES)

    cnew_b = P['cmp'].reshape(nb, t_new, NSA_KV_WIDTH)
    ocmp, selm = _nsa_decode_cmp(page_table, q_rows(P['qn']), cnew_b, feat_major(cache_cmp), wts, t_new, t_pad)
    pad_rows = lambda a, r: jnp.pad(a.reshape(nb, t_new, -1), ((0, 0), (0, r - t_new), (0, 0)))
    o_nsa = _nsa_decode_sel(page_table, q_rows(P['qrot']), pad_rows(P['sel'], 16), pad_rows(P['win'], 16), selm,
                            ocmp, pad_rows(P['gates'], t_pad), feat_major(win_state), feat_major(cache_sel),
                            t_new, t_pad)
    o_nsa = o_nsa[:, :t_new].reshape(n, NSA_WIDTH)

    y = _finish(x2d, o_mla, o_nsa, P['szm'], P['szn'], wts, n)
    wst = win_state.reshape(nb, -1, NSA_KV_WIDTH)
    win_all = jnp.concatenate([wst, P['win'].reshape(nb, t_new, NSA_KV_WIDTH)], axis=1)[:, t_new:]
    return (y.reshape(nb, t_new, D_MODEL),
            P['mrow'].reshape(nb, t_new, MLA_ROW),
            P['cmp'].reshape((nb, t_new) + kvd),
            P['sel'].reshape((nb, t_new) + kvd),
            win_all.reshape((nb, win_all.shape[1]) + kvd))


def kernel(x_prompt, x_sample, cache_mla, cache_cmp_kv, cache_sel_kv, state_win_kv, page_table, w_in, q_norm_g, w_uq,
           kv_norm_g, w_uk, w_uv, cmp_pos_emb, cmp_w1, cmp_b1, cmp_w2, w_o, ln_g, ln_b):
    assert w_in.shape[0] == DEPTH
    wts = _prep_weights(w_in[0], q_norm_g[0], w_uq[0], kv_norm_g[0], w_uk[0], w_uv[0], cmp_pos_emb[0], cmp_w1[0],
                        cmp_b1[0], cmp_w2[0], w_o[0], ln_g[0], ln_b[0])
    yp, p_mla, p_cmp, p_sel, p_win = _prompt_layer(x_prompt, wts)
    drop_depth = lambda a: a.reshape(a.shape[1:])
    ys, s_mla, s_cmp, s_sel, s_win = _sample_layer(x_sample, drop_depth(cache_mla), drop_depth(cache_cmp_kv),
                                                   drop_depth(cache_sel_kv), drop_depth(state_win_kv), page_table, wts)
    add_depth = lambda a: a[None]
    return (yp, ys) + tuple(add_depth(a) for a in (p_mla, p_cmp, p_sel, p_win, s_mla, s_cmp, s_sel, s_win))
```

```python
import functools

import numpy as np
import jax
import jax.numpy as jnp
from jax import lax
from jax.experimental import pallas as pl
from jax.experimental.pallas import tpu as pltpu

F32 = jnp.float32
BF16 = jnp.bfloat16

D_MODEL = 1024
PAGE_SIZE = 128
MLA_HEADS = 8
MLA_V_DIM = 64
MLA_NOPE_DIM = 64
MLA_ROPE_DIM = 32
MLA_Q_LORA = 256
MLA_KV_LORA = 128
MLA_WIDTH = MLA_HEADS * MLA_V_DIM
MLA_ROW = MLA_KV_LORA + MLA_ROPE_DIM
NSA_HEADS = 8
NSA_KV_HEADS = 2
NSA_HEAD_DIM = 64
NSA_GROUP = NSA_HEADS // NSA_KV_HEADS
NSA_WIDTH = NSA_HEADS * NSA_HEAD_DIM
NSA_KV_WIDTH = 2 * NSA_KV_HEADS * NSA_HEAD_DIM
CMP_BLOCK = 32
CMP_STRIDE = 16
SEL_BLOCK = 64
SEL_TOP_N = 16
WINDOW = 512
ROPE_THETA = 10000.0
RMS_EPS = 1e-6
LN_EPS = 1e-5
NEG_INF = -1e30
MLA_SCALE = (MLA_NOPE_DIM + MLA_ROPE_DIM) ** -0.5
NSA_SCALE = NSA_HEAD_DIM ** -0.5
DEPTH = 1
ALPHA = (2 * DEPTH) ** 0.25

LANES = 128
LOG2E = 1.4426950408889634
MLA_KROW = 2 * LANES
MLA_ONES_LANE = MLA_ROW
NSA_KROW = 3 * LANES

SEG_CQ = (0, 256)
SEG_CKV = (256, 128)
SEG_ZMLA = (384, 512)
SEG_QN = (896, 512)
SEG_CMP = (1408, 256)
SEG_SEL = (1664, 256)
SEG_WIN = (1920, 256)
SEG_ZNSA = (2176, 512)
SEG_MISC = (2688, 128)
IN_WIDTH_P = 2816
MISC_GATE0 = MLA_ROPE_DIM

VMEM_LIMIT = 48 * 1024 * 1024

def _cparams(sem, flags=None):
    return pltpu.CompilerParams(dimension_semantics=sem, vmem_limit_bytes=VMEM_LIMIT, flags=flags)


def _sigmoid(x):
    return 1.0 / (1.0 + jnp.exp(-x))


def _silu(x):
    return x * _sigmoid(x)


def _rope_slab(x, c, sa, sb, half):
    return x * c + pltpu.roll(x, LANES - half, 1) * sa + pltpu.roll(x, half, 1) * sb


def _dot_nt(a, b):
    return lax.dot_general(a, b, (((1,), (1,)), ((), ())), preferred_element_type=F32)


def _dot(a, b):
    return jnp.dot(a, b, preferred_element_type=F32)


def _attend_segments(parts):
    is_bias = lambda mask: mask is not None and mask.dtype != jnp.bool_
    sm = [s if mask is None else (s + mask if is_bias(mask) else jnp.where(mask, s, NEG_INF))
          for s, mask, _ in parts]
    m = functools.reduce(jnp.maximum, [jnp.max(x, axis=-1, keepdims=True) for x in sm])
    l, acc = 0.0, 0.0
    for x, (_, mask, pv) in zip(sm, parts):
        p = jnp.exp2(x - m)
        if mask is not None and not is_bias(mask):
            p = jnp.where(mask, p, 0.0)
        l = l + jnp.sum(p, axis=-1, keepdims=True)
        acc = acc + pv(p.astype(BF16))
    ok = l > 0.0
    return jnp.where(ok, acc / jnp.where(ok, l, 1.0), 0.0)


def _flash_step(s, bias, v_ext, m, acc):
    sm = s if bias is None else s + bias
    m_new = jnp.maximum(m, jnp.max(sm, axis=-1, keepdims=True))
    p = jnp.exp2(sm - m_new)
    return m_new, jnp.exp2(m - m_new) * acc + _dot(p.astype(BF16), v_ext)


def _flash_result(acc, ones_lane):
    return acc[:, 0:LANES] * (1.0 / acc[:, ones_lane:ones_lane + 1])


def _tile_rows(x, reps):
    return jnp.concatenate([x] * reps, axis=0)


def _select_blocks(imp, q_pos, n_sel_blocks):
    rows, width = imp.shape
    j = lax.broadcasted_iota(jnp.int32, (rows, width), 1)
    cur = lax.shift_right_logical(q_pos, 6)
    avail = (j * SEL_BLOCK <= q_pos) & (j < n_sel_blocks)
    forced = (j == 0) | (j == cur) | (j == cur - 1)
    val = jnp.where(avail, jnp.where(forced, jnp.inf, imp), -jnp.inf)
    rank = jnp.zeros((rows, width), F32)
    for jp in range(n_sel_blocks):
        col = val[:, jp:jp + 1]
        ahead = (col > val) | ((col == val) & (j > jp))
        rank = rank + jnp.where(ahead, 1.0, 0.0)
    keep = (rank < float(min(SEL_TOP_N, n_sel_blocks))) & avail
    return jnp.where(keep, 1.0, 0.0)


def _select_blocks_t(imp_t, q_pos, n_sel_blocks):
    n_rows, width = imp_t.shape
    j = lax.broadcasted_iota(jnp.int32, (n_rows, width), 0)
    cur = lax.shift_right_logical(q_pos, 6)
    avail = (j * SEL_BLOCK <= q_pos) & (j < n_sel_blocks)
    forced = (j == 0) | (j == cur) | (j == cur - 1)
    val = jnp.where(avail, jnp.where(forced, jnp.inf, imp_t), -jnp.inf)
    n_pieces = -(-n_sel_blocks // 8)
    pieces = [val[8 * v:8 * v + 8] for v in range(n_pieces)]
    jrow = lax.broadcasted_iota(jnp.int32, (8, width), 0)
    ranks = [jnp.zeros((8, width), F32) for _ in range(n_pieces)]
    for jp in range(n_sel_blocks):
        cand = jnp.broadcast_to(val[jp:jp + 1], (8, width))
        for v in range(n_pieces):
            if 8 * v > jp:
                ahead = cand >= pieces[v]
            elif 8 * v + 7 < jp:
                ahead = cand > pieces[v]
            else:
                ahead = (cand > pieces[v]) | ((cand == pieces[v]) & (jrow > jp - 8 * v))
            ranks[v] = ranks[v] + jnp.where(ahead, 1.0, 0.0)
    if n_rows > 8 * n_pieces:
        ranks.append(jnp.zeros((n_rows - 8 * n_pieces, width), F32))
    rank = jnp.concatenate(ranks, axis=0)
    keep = (rank < float(min(SEL_TOP_N, n_sel_blocks))) & avail
    return jnp.where(keep, 1.0, 0.0)


def _proj_kernel(x_ref, w_ref, qg_ref, wuq_ref, wuk_ref, kvg_ref, tab_ref,
                 qmla_ref, mrow_ref, mrowb_ref, szm_ref, qn_ref, qrot_ref, cmp_ref, cmps_ref,
                 sel_ref, selb_ref, win_ref, winb_ref, gates_ref, szn_ref, *, feature_major):
    xb = x_ref[...].astype(BF16)

    def seg(s):
        return _dot(xb, w_ref[:, s[0]:s[0] + s[1]])

    cn, san, sbn = tab_ref[0], tab_ref[1], tab_ref[2]
    cm, sam, sbm = tab_ref[3], tab_ref[4], tab_ref[5]

    cq = seg(SEG_CQ)
    r = cq * lax.rsqrt(jnp.mean(cq * cq, axis=-1, keepdims=True) + RMS_EPS) * qg_ref[...]
    q = _dot(r.astype(BF16), wuq_ref[...])
    nope_w = MLA_HEADS * MLA_NOPE_DIM
    lane = lax.broadcasted_iota(jnp.int32, (q.shape[0], LANES), 1)
    for h in range(MLA_HEADS):
        ql = _dot(q[:, h * MLA_NOPE_DIM:(h + 1) * MLA_NOPE_DIM].astype(BF16), wuk_ref[h])
        qmla_ref[h, :, 0:LANES] = (ql * (MLA_SCALE * LOG2E)).astype(BF16)
    heads_per_slab = LANES // MLA_ROPE_DIM
    for jj in range(MLA_HEADS // heads_per_slab):
        qr = _rope_slab(q[:, nope_w + jj * LANES: nope_w + (jj + 1) * LANES], cm, sam, sbm, MLA_ROPE_DIM // 2)
        qr = qr * (MLA_SCALE * LOG2E)
        for hh in range(heads_per_slab):
            front = qr if hh == 0 else pltpu.roll(qr, LANES - hh * MLA_ROPE_DIM, 1)
            qmla_ref[jj * heads_per_slab + hh, :, LANES:2 * LANES] = (
                jnp.where(lane < MLA_ROPE_DIM, front, 0.0).astype(BF16))

    ckv = seg(SEG_CKV)
    lat = ckv * lax.rsqrt(jnp.mean(ckv * ckv, axis=-1, keepdims=True) + RMS_EPS) * kvg_ref[...]
    misc = seg(SEG_MISC)
    kr = _rope_slab(misc, cm, sam, sbm, MLA_ROPE_DIM // 2)
    if feature_major:
        mrow_ref[0:MLA_KV_LORA, :] = lat.T
        mrow_ref[MLA_KV_LORA:MLA_ROW, :] = kr.T[0:MLA_ROPE_DIM]
    else:
        mrow_ref[:, 0:MLA_KV_LORA] = lat
        mrow_ref[:, MLA_KV_LORA:MLA_ROW] = kr[:, 0:MLA_ROPE_DIM]
    mrowb_ref[:, 0:LANES] = lat.astype(BF16)
    ones_col = jnp.where(lane == MLA_ONES_LANE - LANES, 1.0, 0.0)
    mrowb_ref[:, LANES:2 * LANES] = jnp.where(lane < MLA_ROPE_DIM, kr, ones_col).astype(BF16)
    gates_ref[...] = _sigmoid(misc)

    szm_ref[...] = _silu(seg(SEG_ZMLA))
    szn_ref[...] = _silu(seg(SEG_ZNSA))

    qn = seg(SEG_QN)
    for jj in range(NSA_WIDTH // LANES):
        raw = qn[:, jj * LANES:(jj + 1) * LANES] * (NSA_SCALE * LOG2E)
        rot = _rope_slab(qn[:, jj * LANES:(jj + 1) * LANES], cn, san, sbn, NSA_HEAD_DIM // 2) * (NSA_SCALE * LOG2E)
        for src, dst in ((raw, qn_ref), (rot, qrot_ref)):
            swapped = pltpu.roll(src, NSA_HEAD_DIM, 1)
            for half in range(2):
                hd = 2 * jj + half
                g = hd // NSA_GROUP
                keep = (lane >= g * NSA_HEAD_DIM) & (lane < (g + 1) * NSA_HEAD_DIM)
                dst[hd] = jnp.where(keep, src if half == g else swapped, 0.0).astype(BF16)

    def put_rows(dst, first, second):
        if feature_major:
            dst[0:LANES, :] = first.T
            dst[LANES:2 * LANES, :] = second.T
        else:
            dst[:, 0:LANES] = first
            dst[:, LANES:2 * LANES] = second

    cmpv = seg(SEG_CMP)
    put_rows(cmp_ref, cmpv[:, 0:LANES], cmpv[:, LANES:2 * LANES])
    cmps_ref[0] = cmpv[:, 0:LANES]
    cmps_ref[1] = cmpv[:, LANES:2 * LANES]

    for src, dst, dstb in ((SEG_SEL, sel_ref, selb_ref), (SEG_WIN, win_ref, winb_ref)):
        kv = seg(src)
        k = _rope_slab(kv[:, 0:LANES], cn, san, sbn, NSA_HEAD_DIM // 2)
        v = kv[:, LANES:2 * LANES]
        put_rows(dst, k, v)
        dstb[:, 0:LANES] = k.astype(BF16)
        dstb[:, LANES:2 * LANES] = v.astype(BF16)
        dstb[:, 2 * LANES:3 * LANES] = jnp.where(lane == 0, 1.0, 0.0).astype(BF16)


def _rope_tables(pos):
    def tab(d):
        inv = 1.0 / (ROPE_THETA ** (jnp.arange(0, d, 2, dtype=F32) / d))
        ang = pos.astype(F32)[:, None] * inv[None, :]
        cos, sin = jnp.cos(ang), jnp.sin(ang)
        zero = jnp.zeros_like(sin)
        reps = LANES // d
        return (jnp.tile(jnp.concatenate([cos, cos], -1), (1, reps)),
                jnp.tile(jnp.concatenate([-sin, zero], -1), (1, reps)),
                jnp.tile(jnp.concatenate([zero, sin], -1), (1, reps)))
    return jnp.stack(tab(NSA_HEAD_DIM) + tab(MLA_ROPE_DIM))


def _project(x2d, tabs, wts, tm, feature_major=False):
    n = x2d.shape[0]
    p_rows = tabs.shape[1]
    assert n % tm == 0 and p_rows % tm == 0
    nt = p_rows // tm
    row = lambda w: pl.BlockSpec((tm, w), lambda i: (i, 0))
    full = lambda a: pl.BlockSpec(a.shape, lambda i: (0,) * a.ndim)
    if feature_major:
        out_rows = lambda w: ((n // p_rows, w, p_rows), F32, pl.BlockSpec((None, w, tm), lambda i: (i // nt, 0, i % nt)))
    else:
        out_rows = lambda w: ((n, w), F32, row(w))
    out_shapes = dict(
        qmla=((MLA_HEADS, n, MLA_KROW), BF16, pl.BlockSpec((MLA_HEADS, tm, MLA_KROW), lambda i: (0, i, 0))),
        mrow=out_rows(MLA_ROW),
        mrowb=((n, MLA_KROW), BF16, row(MLA_KROW)),
        szm=((n, MLA_WIDTH), F32, row(MLA_WIDTH)),
        qn=((NSA_HEADS, n, LANES), BF16, pl.BlockSpec((NSA_HEADS, tm, LANES), lambda i: (0, i, 0))),
        qrot=((NSA_HEADS, n, LANES), BF16, pl.BlockSpec((NSA_HEADS, tm, LANES), lambda i: (0, i, 0))),
        cmp=out_rows(NSA_KV_WIDTH),
        cmps=((2, n, LANES), F32, pl.BlockSpec((2, tm, LANES), lambda i: (0, i, 0))),
        sel=out_rows(NSA_KV_WIDTH),
        selb=((n, NSA_KROW), BF16, row(NSA_KROW)),
        win=out_rows(NSA_KV_WIDTH),
        winb=((n, NSA_KROW), BF16, row(NSA_KROW)),
        gates=((n, LANES), F32, row(LANES)),
        szn=((n, NSA_WIDTH), F32, row(NSA_WIDTH)),
    )
    names = list(out_shapes)
    outs = pl.pallas_call(
        functools.partial(_proj_kernel, feature_major=feature_major),
        grid=(n // tm,),
        in_specs=[row(D_MODEL), full(wts['w_in']), full(wts['q_norm_g']), full(wts['w_uq']), full(wts['w_uk']),
                  full(wts['kv_norm_g']), pl.BlockSpec((6, tm, LANES), lambda i: (0, i % nt, 0))],
        out_specs=[out_shapes[k][2] for k in names],
        out_shape=[jax.ShapeDtypeStruct(out_shapes[k][0], out_shapes[k][1]) for k in names],
        compiler_params=_cparams(("parallel",)),
        name="projection",
    )(x2d, wts['w_in'], wts['q_norm_g'], wts['w_uq'], wts['w_uk'], wts['kv_norm_g'], tabs)
    return dict(zip(names, outs))


def _uv_project(o, wuv_ref, rows):
    return jnp.concatenate(
        [_dot(o[h * rows:(h + 1) * rows].astype(BF16), wuv_ref[h]) for h in range(MLA_HEADS)], axis=-1)


def _flash_pipeline(n_chunks, prepare, scores, consume, s_a, s_b):
    scores(0, s_a)

    def body(i, _):
        c0 = 2 * i
        aux = prepare(c0)
        scores(c0 + 1, s_b)
        consume(c0, s_a, aux)

        @pl.when(c0 + 1 < n_chunks)
        def _():
            aux = prepare(c0 + 1)
            scores(c0 + 2, s_a)
            consume(c0 + 1, s_b, aux)
        return 0

    lax.fori_loop(0, (n_chunks + 1) // 2, body, 0)


def _mla_prompt_kernel(q_ref, k_ref, wuv_ref, o_ref, s_a, s_b, m_ref, acc_ref, *, tq, tk):
    qi = pl.program_id(1)
    rows = MLA_HEADS * tq
    q = q_ref[...].reshape(rows, MLA_KROW)
    n_chunks = (qi * tq) // tk + 1
    col = lax.broadcasted_iota(jnp.int32, (tq, tk), 1)
    q_pos = qi * tq + lax.broadcasted_iota(jnp.int32, (tq, tk), 0)
    m_ref[...] = jnp.full(m_ref.shape, NEG_INF, F32)
    acc_ref[...] = jnp.zeros(acc_ref.shape, F32)

    def chunk(c):
        return k_ref[pl.ds(pl.multiple_of(c * tk, tk), tk), :]

    def scores(c, dst):
        dst[...] = _dot_nt(q, chunk(jnp.minimum(c, n_chunks - 1)))

    def prepare(c):
        return _tile_rows(jnp.where(c * tk + col <= q_pos, 0.0, NEG_INF), MLA_HEADS)

    def consume(c, src, bias):
        m_ref[...], acc_ref[...] = _flash_step(src[...], bias, chunk(c), m_ref[...], acc_ref[...])

    _flash_pipeline(n_chunks, prepare, scores, consume, s_a, s_b)
    o_ref[...] = _uv_project(_flash_result(acc_ref[...], MLA_ONES_LANE), wuv_ref, tq)


def _mla_prompt(qmla, mrowb, wuv, batch, seq):
    tq = min(256, seq)
    tk = min(512, seq)
    nq = seq // tq
    assert tk % tq == 0 and seq % tk == 0
    return pl.pallas_call(
        functools.partial(_mla_prompt_kernel, tq=tq, tk=tk),
        grid=(batch, nq),
        in_specs=[pl.BlockSpec((MLA_HEADS, tq, MLA_KROW), lambda b, i: (0, b * nq + i, 0)),
                  pl.BlockSpec((None, seq, MLA_KROW), lambda b, i: (b, 0, 0)),
                  pl.BlockSpec(wuv.shape, lambda b, i: (0, 0, 0))],
        out_specs=pl.BlockSpec((tq, MLA_WIDTH), lambda b, i: (b * nq + i, 0)),
        out_shape=jax.ShapeDtypeStruct((batch * seq, MLA_WIDTH), F32),
        scratch_shapes=[pltpu.VMEM((MLA_HEADS * tq, tk), F32), pltpu.VMEM((MLA_HEADS * tq, tk), F32),
                        pltpu.VMEM((MLA_HEADS * tq, 1), F32), pltpu.VMEM((MLA_HEADS * tq, MLA_KROW), F32)],
        compiler_params=_cparams(("parallel", "parallel")),
        name="mla_prompt",
    )(qmla, mrowb.reshape(batch, seq, MLA_KROW), wuv)


CMP_PAIRS = CMP_STRIDE // 2
CHUNK_PITCH = 24


def _compress_slab(load_pair, wc1_ref, pe_ref, b1_ref, w2_ref, s, n_chunk):
    proj = jnp.zeros((n_chunk, 2 * LANES), F32)
    pe = jnp.zeros((8, 2 * LANES), F32)
    for j in range(CMP_PAIRS):
        w = wc1_ref[s, j]
        proj = proj + _dot(load_pair(j).astype(BF16), w)
        pe = pe + _dot(pe_ref[s, j].astype(BF16), w)
    bias = b1_ref[s] + pe[0:1, 0:LANES] + pe[1:2, LANES:2 * LANES]
    hid = proj[:, 0:LANES] + pltpu.roll(proj[:, LANES:2 * LANES], n_chunk - 1, 0) + bias
    return _dot(_silu(hid).astype(BF16), w2_ref[s])


def _compress_prompt_kernel(x_ref, wc1_ref, pe_ref, b1_ref, w2_ref, o_ref, *, n_chunk):
    for s in range(2):
        row = lambda p, s=s: x_ref[s, pl.ds(p, n_chunk, stride=CMP_STRIDE), :]
        load = lambda j, row=row: jnp.concatenate([row(2 * j), row(2 * j + 1)], axis=-1)
        o_ref[s] = _compress_slab(load, wc1_ref, pe_ref, b1_ref, w2_ref, s, n_chunk).astype(BF16)


def _compress_prompt(cmps, wts, batch, seq):
    n_chunk = seq // CMP_STRIDE
    full = lambda a: pl.BlockSpec(a.shape, lambda b: (0,) * a.ndim)
    return pl.pallas_call(
        functools.partial(_compress_prompt_kernel, n_chunk=n_chunk),
        grid=(batch,),
        in_specs=[pl.BlockSpec((2, seq, LANES), lambda b: (0, b, 0)),
                  full(wts['wc1']), full(wts['pe']), full(wts['b1']), full(wts['w2'])],
        out_specs=pl.BlockSpec((None, 2, n_chunk, LANES), lambda b: (b, 0, 0, 0)),
        out_shape=jax.ShapeDtypeStruct((batch, 2, n_chunk, LANES), BF16),
        compiler_params=_cparams(("parallel",)),
        name="compress_prompt",
    )(cmps, wts['wc1'], wts['pe'], wts['b1'], wts['w2'])


def _cmp_branch(q, kc, vc, q_pos_rows, n_cmp_blocks, impmap_ref, rows_t, imp_transposed=False):
    s = _dot_nt(q, kc)
    n = lax.broadcasted_iota(jnp.int32, (rows_t, s.shape[1]), 1)
    visible = (n * CMP_STRIDE + (CMP_BLOCK - 1) <= q_pos_rows[0:rows_t]) & (n < n_cmp_blocks)
    sm = s + _tile_rows(jnp.where(visible, 0.0, NEG_INF), NSA_HEADS)
    e = jnp.exp2(sm - jnp.max(sm, axis=-1, keepdims=True))
    any_visible = (q_pos_rows >= CMP_BLOCK - 1) & (n_cmp_blocks > 0)
    p = e * jnp.where(any_visible, 1.0 / jnp.sum(e, axis=-1, keepdims=True), 0.0)
    o_cmp = _dot(p.astype(BF16), vc)
    imps = []
    for g in range(NSA_KV_HEADS):
        base = g * NSA_GROUP * rows_t
        psum = p[base:base + rows_t]
        for hh in range(1, NSA_GROUP):
            psum = psum + p[base + hh * rows_t: base + (hh + 1) * rows_t]
        hi = psum.astype(BF16)
        lo = (psum - hi.astype(F32)).astype(BF16)
        if imp_transposed:
            imps.append(_dot_nt(impmap_ref[...], hi) + _dot_nt(impmap_ref[...], lo))
        else:
            imps.append(_dot(hi, impmap_ref[...]) + _dot(lo, impmap_ref[...]))
    return o_cmp, imps


def _mix_heads(gates, o_cmp, o_sel, o_win, rows_t):
    outs = []
    for hd in range(NSA_HEADS):
        g = hd // NSA_GROUP
        rs = slice(hd * rows_t, (hd + 1) * rows_t)
        ls = slice(g * NSA_HEAD_DIM, (g + 1) * NSA_HEAD_DIM)
        gcol = lambda br: gates[:, MISC_GATE0 + br * NSA_HEADS + hd: MISC_GATE0 + br * NSA_HEADS + hd + 1]
        outs.append(gcol(0) * o_cmp[rs, ls] + gcol(1) * o_sel[rs, ls] + gcol(2) * o_win[rs, ls])
    return jnp.concatenate(outs, axis=-1)


def _group_rows(per_group, reps):
    return jnp.concatenate([per_group[g] for g in range(NSA_KV_HEADS) for _ in range(reps)], axis=0)


def _nsa_prompt_kernel(qn_ref, qrot_ref, kvc_ref, sel_ref, win_ref, gates_ref, impmap_ref, o_ref,
                       s_a, s_b, m_ref, acc_ref, *, tq, tk, n_cmp_blocks, n_sel_blocks, wlen):
    qi = pl.program_id(1)
    rows = NSA_HEADS * tq
    q_pos_t = qi * tq + lax.broadcasted_iota(jnp.int32, (tq, 1), 0)
    q_pos_r = qi * tq + (lax.broadcasted_iota(jnp.int32, (rows, 1), 0) & (tq - 1))
    qn = qn_ref[...].reshape(rows, LANES)
    qrot = qrot_ref[...].reshape(rows, LANES)
    init = (jnp.full((rows, 1), NEG_INF, F32), jnp.zeros((rows, 2 * LANES), F32))

    w_start = pl.multiple_of(jnp.maximum(qi * tq + tq - wlen, 0), tq)
    d = q_pos_t - (w_start + lax.broadcasted_iota(jnp.int32, (tq, wlen), 1))
    bias = _tile_rows(jnp.where((d >= 0) & (d < WINDOW), 0.0, NEG_INF), NSA_HEADS)
    _, acc = _flash_step(_dot_nt(qrot, win_ref[pl.ds(w_start, wlen), 0:LANES]), bias,
                         win_ref[pl.ds(w_start, wlen), LANES:3 * LANES], *init)
    o_win = _flash_result(acc, LANES)

    o_cmp, imps_t = _cmp_branch(qn, kvc_ref[0], kvc_ref[1], q_pos_r, n_cmp_blocks, impmap_ref, tq,
                                imp_transposed=True)
    q_pos_lane = qi * tq + lax.broadcasted_iota(jnp.int32, (1, tq), 1)
    selms = [_select_blocks_t(imp_t, q_pos_lane, n_sel_blocks).T.astype(BF16) for imp_t in imps_t]

    blk_per_chunk = tk // SEL_BLOCK
    n_chunks = (qi * tq + tq + tk - 1) // tk
    e_row = lax.broadcasted_iota(jnp.int32, (LANES, tk), 0)
    e_col = lax.shift_right_logical(lax.broadcasted_iota(jnp.int32, (LANES, tk), 1), 6)
    col_t = lax.broadcasted_iota(jnp.int32, (tq, tk), 1)

    def sel_scores(c, dst):
        start = pl.multiple_of(jnp.minimum(c, n_chunks - 1) * tk, tk)
        dst[...] = _dot_nt(qrot, sel_ref[pl.ds(start, tk), 0:LANES])

    def sel_prepare(c):
        expand = jnp.where(e_row == c * blk_per_chunk + e_col, 1.0, 0.0).astype(BF16)
        causal = (c * tk + col_t) <= q_pos_t
        return _group_rows([jnp.where((_dot(sm, expand) > 0.5) & causal, 0.0, NEG_INF) for sm in selms], NSA_GROUP)

    def sel_consume(c, src, bias):
        v_ext = sel_ref[pl.ds(pl.multiple_of(c * tk, tk), tk), LANES:3 * LANES]
        m_ref[...], acc_ref[...] = _flash_step(src[...], bias, v_ext, m_ref[...], acc_ref[...])

    m_ref[...], acc_ref[...] = init
    _flash_pipeline(n_chunks, sel_prepare, sel_scores, sel_consume, s_a, s_b)
    o_sel = _flash_result(acc_ref[...], LANES)

    o_ref[...] = _mix_heads(gates_ref[...], o_cmp, o_sel, o_win, tq)


def _importance_map(nc_rows, nc, ns, width):
    i = np.arange(nc_rows)[:, None]
    j = np.arange(width)[None, :]
    lo = np.maximum(i * CMP_STRIDE, j * SEL_BLOCK)
    hi = np.minimum(i * CMP_STRIDE + CMP_BLOCK, (j + 1) * SEL_BLOCK)
    m = np.maximum(hi - lo, 0).astype(np.float32) / CMP_BLOCK
    m = m * (i < nc) * (j < ns)
    return jnp.asarray(m, dtype=BF16)


def _nsa_prompt(P, kvc, batch, seq):
    tq = min(256, seq)
    tk = min(512, seq)
    nq = seq // tq
    n_chunk = seq // CMP_STRIDE
    nc = n_chunk - CMP_BLOCK // CMP_STRIDE + 1
    ns = seq // SEL_BLOCK
    assert ns <= LANES and seq % tk == 0 and seq % CMP_STRIDE == 0 and seq % SEL_BLOCK == 0
    wlen = min(WINDOW + tq, seq)
    impmap = _importance_map(n_chunk, nc, ns, LANES).T
    blk = lambda w: pl.BlockSpec((tq, w), lambda b, i: (b * nq + i, 0))
    qblk = pl.BlockSpec((NSA_HEADS, tq, LANES), lambda b, i: (0, b * nq + i, 0))
    per_b = lambda a: pl.BlockSpec((None,) + a.shape[1:], lambda b, i: (b,) + (0,) * (a.ndim - 1))
    selb = P['selb'].reshape(batch, seq, NSA_KROW)
    winb = P['winb'].reshape(batch, seq, NSA_KROW)
    return pl.pallas_call(
        functools.partial(_nsa_prompt_kernel, tq=tq, tk=tk, n_cmp_blocks=nc, n_sel_blocks=ns, wlen=wlen),
        grid=(batch, nq),
        in_specs=[qblk, qblk, per_b(kvc), per_b(selb), per_b(winb), blk(LANES),
                  pl.BlockSpec(impmap.shape, lambda b, i: (0, 0))],
        out_specs=blk(NSA_WIDTH),
        out_shape=jax.ShapeDtypeStruct((batch * seq, NSA_WIDTH), F32),
        scratch_shapes=[pltpu.VMEM((NSA_HEADS * tq, tk), F32), pltpu.VMEM((NSA_HEADS * tq, tk), F32),
                        pltpu.VMEM((NSA_HEADS * tq, 1), F32), pltpu.VMEM((NSA_HEADS * tq, 2 * LANES), F32)],
        compiler_params=_cparams(("parallel", "parallel")),
        name="nsa_prompt",
    )(P['qn'], P['qrot'], kvc, selb, winb, P['gates'], impmap)


def _finish_kernel(x_ref, omla_ref, onsa_ref, szm_ref, szn_ref, wo_ref, g_ref, b_ref, y_ref):
    mixed = jnp.concatenate([omla_ref[...] * szm_ref[...], onsa_ref[...] * szn_ref[...]], axis=-1)
    h = ALPHA * x_ref[...] + _dot(mixed.astype(BF16), wo_ref[...])
    mu = jnp.mean(h, axis=-1, keepdims=True)
    d = h - mu
    var = jnp.mean(d * d, axis=-1, keepdims=True)
    y_ref[...] = d * lax.rsqrt(var + LN_EPS) * g_ref[...] + b_ref[...]


def _finish(x2d, omla, onsa, szm, szn, wts, tm):
    n = x2d.shape[0]
    row = lambda w: pl.BlockSpec((tm, w), lambda i: (i, 0))
    full = lambda a: pl.BlockSpec(a.shape, lambda i: (0,) * a.ndim)
    return pl.pallas_call(
        _finish_kernel,
        grid=(n // tm,),
        in_specs=[row(D_MODEL), row(MLA_WIDTH), row(NSA_WIDTH), row(MLA_WIDTH), row(NSA_WIDTH),
                  full(wts['w_o']), full(wts['ln_g']), full(wts['ln_b'])],
        out_specs=row(D_MODEL),
        out_shape=jax.ShapeDtypeStruct((n, D_MODEL), F32),
        compiler_params=_cparams(("parallel",)),
        name="finish",
    )(x2d, omla, onsa, szm, szn, wts['w_o'], wts['ln_g'], wts['ln_b'])


def _page_copies(pt_ref, b, n_pages, make_copy):
    assert n_pages % 2 == 0

    def start(i, _):
        make_copy(pt_ref[b, 2 * i], 2 * i).start(priority=0)
        make_copy(pt_ref[b, 2 * i + 1], 2 * i + 1).start(priority=1)
        return 0
    lax.fori_loop(0, n_pages // 2, start, 0, unroll=4)


def _page_wait_all(pool_ref, slot_buf, sem):
    n_pages = slot_buf.shape[0]
    pltpu.make_async_copy(pool_ref.at[pl.ds(0, n_pages)], slot_buf, sem).wait()


def _mla_decode_kernel(pt_ref, q_ref, knew_ref, pool_ref, o_ref, kbuf, sem, *, n_pages, t_new, tk):
    b = pl.program_id(0)
    nb = pl.num_programs(0)
    slot = b % 2
    past = n_pages * PAGE_SIZE

    def copy(sl):
        return lambda page, p: pltpu.make_async_copy(pool_ref.at[page], kbuf.at[sl, p], sem.at[sl])

    @pl.when(b == 0)
    def _():
        _page_copies(pt_ref, 0, n_pages, copy(0))

    @pl.when(b + 1 < nb)
    def _():
        _page_copies(pt_ref, b + 1, n_pages, copy(1 - slot))

    _page_wait_all(pool_ref, kbuf.at[slot], sem.at[slot])

    q = q_ref[...]
    rows = q.shape[0]
    pages_per_chunk = tk // PAGE_SIZE
    parts = []
    for c in range(n_pages // pages_per_chunk):
        kt = jnp.concatenate([kbuf[slot, c * pages_per_chunk + i].astype(BF16) for i in range(pages_per_chunk)],
                             axis=1)
        parts.append((_dot(q, kt), None, lambda p, kt=kt: _dot_nt(p, kt[0:MLA_KV_LORA])))
    kn = knew_ref[...]
    t_q = lax.broadcasted_iota(jnp.int32, (rows, kn.shape[0]), 0) & (t_new - 1)
    t_k = lax.broadcasted_iota(jnp.int32, (rows, kn.shape[0]), 1)
    parts.append((_dot_nt(q, kn), t_k <= t_q, lambda p: _dot(p, kn[:, 0:MLA_KV_LORA])))
    o_ref[...] = _attend_segments(parts)


def _mla_decode(page_table, q_b, knew_b, pool):
    nb, n_pages = page_table.shape
    rows = q_b.shape[1]
    t_new = rows // MLA_HEADS
    tk = 1024
    past = n_pages * PAGE_SIZE
    assert past % tk == 0 and (t_new & (t_new - 1)) == 0
    grid_spec = pltpu.PrefetchScalarGridSpec(
        num_scalar_prefetch=1,
        grid=(nb,),
        in_specs=[pl.BlockSpec((None, rows, MLA_ROW), lambda b, pt: (b, 0, 0)),
                  pl.BlockSpec((None,) + knew_b.shape[1:], lambda b, pt: (b, 0, 0)),
                  pl.BlockSpec(memory_space=pl.ANY)],
        out_specs=pl.BlockSpec((None, rows, MLA_KV_LORA), lambda b, pt: (b, 0, 0)),
        scratch_shapes=[pltpu.VMEM((2, n_pages, MLA_ROW, PAGE_SIZE), F32), pltpu.SemaphoreType.DMA((2,))],
    )
    return pl.pallas_call(
        functools.partial(_mla_decode_kernel, n_pages=n_pages, t_new=t_new, tk=tk),
        grid_spec=grid_spec,
        out_shape=jax.ShapeDtypeStruct((nb, rows, MLA_KV_LORA), F32),
        compiler_params=_cparams(("arbitrary",)),
        name="mla_decode",
    )(page_table, q_b, knew_b, pool)


def _uv_kernel(o_ref, wuv_ref, y_ref):
    rows = o_ref.shape[1]
    y_ref[...] = _uv_project(o_ref[...].reshape(MLA_HEADS * rows, MLA_KV_LORA), wuv_ref, rows)


def _uv(o_heads, wuv):
    rows = o_heads.shape[1]
    return pl.pallas_call(
        _uv_kernel,
        out_shape=jax.ShapeDtypeStruct((rows, MLA_WIDTH), F32),
        name="mla_value_up",
    )(o_heads, wuv)


def _nsa_dec_cmp_kernel(pt_ref, qn_ref, cnew_ref, pool_ref, wc1_ref, pe_ref, b1_ref, w2_ref, impmap_ref,
                        ocmp_ref, imp_ref, pbuf, xrows_k, xrows_v, sem, *, n_pages, t_new, t_pad, n_chunk,
                        n_cmp_blocks):
    b = pl.program_id(0)
    nb = pl.num_programs(0)
    slot = b % 2
    past = n_pages * PAGE_SIZE

    def copy(sl):
        return lambda page, p: pltpu.make_async_copy(pool_ref.at[page], pbuf.at[sl, p], sem.at[sl])

    @pl.when(b == 0)
    def _():
        _page_copies(pt_ref, 0, n_pages, copy(0))

    @pl.when(b + 1 < nb)
    def _():
        _page_copies(pt_ref, b + 1, n_pages, copy(1 - slot))

    _page_wait_all(pool_ref, pbuf.at[slot], sem.at[slot])

    chunks_per_page = PAGE_SIZE // CMP_STRIDE
    xrows = (xrows_k, xrows_v)

    def relayout(p, s):
        r0 = p * (chunks_per_page * CHUNK_PITCH)
        if not isinstance(p, int):
            r0 = pl.multiple_of(r0, 8)
        t = pbuf[slot, p, s * LANES:(s + 1) * LANES, :].astype(BF16).T.astype(F32)
        for c in range(chunks_per_page):
            xrows[s][pl.ds(r0 + c * CHUNK_PITCH, CMP_STRIDE), :] = t[c * CMP_STRIDE:(c + 1) * CMP_STRIDE]

    cnew = cnew_ref[...]
    base = n_pages * chunks_per_page * CHUNK_PITCH
    tail = xrows_k.shape[0] - base
    for s in range(2):
        xrows[s][pl.ds(base, tail), :] = jnp.zeros((tail, LANES), F32)
        xrows[s][pl.ds(base, t_new), :] = cnew[:, s * LANES:(s + 1) * LANES]

    def compress(s):
        row = lambda p: xrows[s][pl.ds(p, n_chunk, stride=CHUNK_PITCH), :]
        load = lambda j: jnp.concatenate([row(2 * j), row(2 * j + 1)], axis=-1)
        return _compress_slab(load, wc1_ref, pe_ref, b1_ref, w2_ref, s, n_chunk).astype(BF16)

    def relayout_keys(p, _):
        relayout(p, 0)
        return 0

    lax.fori_loop(0, n_pages, relayout_keys, 0, unroll=8)
    for p in range(n_pages):
        relayout(p, 1)
    kvc = [compress(0), compress(1)]

    rows = NSA_HEADS * t_pad
    q_pos_r = past + (lax.broadcasted_iota(jnp.int32, (rows, 1), 0) & (t_pad - 1))
    o_cmp, imps = _cmp_branch(qn_ref[...], kvc[0][0:n_cmp_blocks], kvc[1][0:n_cmp_blocks], q_pos_r,
                              n_cmp_blocks, impmap_ref, t_pad)
    ocmp_ref[...] = o_cmp
    for g in range(NSA_KV_HEADS):
        imp_ref[g] = imps[g]


def _nsa_decode_cmp(page_table, qn_b, cnew_b, pool, wts, t_new, t_pad):
    nb, n_pages = page_table.shape
    past = n_pages * PAGE_SIZE
    total = past + t_new
    n_chunk_true = -(-total // CMP_STRIDE)
    nc = n_chunk_true - CMP_BLOCK // CMP_STRIDE + 1
    n_chunk = -(-n_chunk_true // 8) * 8
    assert nc % LANES == 0, "compressed-block count must be lane aligned"
    ns = past // SEL_BLOCK + (-(-t_new // SEL_BLOCK))
    sel_w = -(-ns // LANES) * LANES
    impmap = _importance_map(nc, nc, ns, sel_w)
    rows = NSA_HEADS * t_pad
    full = lambda a: pl.BlockSpec(a.shape, lambda b, pt: (0,) * a.ndim)
    per_b = lambda a: pl.BlockSpec((None,) + a.shape[1:], lambda b, pt: (b,) + (0,) * (a.ndim - 1))
    grid_spec = pltpu.PrefetchScalarGridSpec(
        num_scalar_prefetch=1,
        grid=(nb,),
        in_specs=[per_b(qn_b), per_b(cnew_b), pl.BlockSpec(memory_space=pl.ANY),
                  full(wts['wc1']), full(wts['pe']), full(wts['b1']), full(wts['w2']), full(impmap)],
        out_specs=[pl.BlockSpec((None, rows, LANES), lambda b, pt: (b, 0, 0)),
                   pl.BlockSpec((None, NSA_KV_HEADS, t_pad, sel_w), lambda b, pt: (b, 0, 0, 0))],
        scratch_shapes=[pltpu.VMEM((2, n_pages, NSA_KV_WIDTH, PAGE_SIZE), F32),
                        pltpu.VMEM((n_chunk * CHUNK_PITCH, LANES), F32),
                        pltpu.VMEM((n_chunk * CHUNK_PITCH, LANES), F32),
                        pltpu.SemaphoreType.DMA((2,))],
    )
    return pl.pallas_call(
        functools.partial(_nsa_dec_cmp_kernel, n_pages=n_pages, t_new=t_new, t_pad=t_pad, n_chunk=n_chunk,
                          n_cmp_blocks=nc),
        grid_spec=grid_spec,
        out_shape=[jax.ShapeDtypeStruct((nb, rows, LANES), F32),
                   jax.ShapeDtypeStruct((nb, NSA_KV_HEADS, t_pad, sel_w), F32)],
        compiler_params=_cparams(("arbitrary",)),
        name="nsa_decode_compress",
    )(page_table, qn_b, cnew_b, pool, wts['wc1'], wts['pe'], wts['b1'], wts['w2'], impmap)


def _nsa_dec_sel_kernel(pt_ref, qr_ref, snew_ref, wnew_ref, imp_ref, ocmp_ref, gates_ref, wst_ref, pool_ref,
                        o_ref, kbuf, sem, *, n_pages, t_new, t_pad, tk, n_sel_blocks):
    b = pl.program_id(0)
    nb = pl.num_programs(0)
    slot = b % 2
    past = n_pages * PAGE_SIZE
    rows = NSA_HEADS * t_pad

    def copy(sl):
        return lambda page, p: pltpu.make_async_copy(pool_ref.at[page], kbuf.at[sl, p], sem.at[sl])

    @pl.when(b == 0)
    def _():
        _page_copies(pt_ref, 0, n_pages, copy(0))

    @pl.when(b + 1 < nb)
    def _():
        _page_copies(pt_ref, b + 1, n_pages, copy(1 - slot))

    _page_wait_all(pool_ref, kbuf.at[slot], sem.at[slot])

    blk_per_chunk = tk // SEL_BLOCK
    lane_lo = lax.broadcasted_iota(jnp.int32, (t_pad, LANES), 1) < SEL_BLOCK
    n_new = snew_ref.shape[0]
    t_q = lax.broadcasted_iota(jnp.int32, (rows, n_new), 0) & (t_pad - 1)
    t_k = lax.broadcasted_iota(jnp.int32, (rows, n_new), 1)
    new_causal = (t_k <= t_q) & (t_k < t_new)
    w_buf = wst_ref.shape[1]
    t_qw = lax.broadcasted_iota(jnp.int32, (rows, w_buf), 0) & (t_pad - 1)
    i_w = lax.broadcasted_iota(jnp.int32, (rows, w_buf), 1)
    win_mask = (t_qw + w_buf - i_w) < WINDOW
    snew = snew_ref[...]
    wnew = wnew_ref[...]
    qr = qr_ref[...]
    q_pos_t = past + lax.broadcasted_iota(jnp.int32, (t_pad, 1), 0)
    sel_groups = [_select_blocks(imp_ref[g], q_pos_t, n_sel_blocks) for g in range(NSA_KV_HEADS)]

    parts = []
    pages_per_chunk = tk // PAGE_SIZE
    for c in range(past // tk):
        chunk_pages = range(c * pages_per_chunk, (c + 1) * pages_per_chunk)
        kt = jnp.concatenate([kbuf[slot, p, 0:LANES, :].astype(BF16) for p in chunk_pages], axis=1)
        vt = jnp.concatenate([kbuf[slot, p, LANES:2 * LANES, :].astype(BF16) for p in chunk_pages], axis=1)
        biases = []
        for sel_g in sel_groups:
            pieces = []
            for i in range(tk // LANES):
                jb = c * blk_per_chunk + 2 * i
                pieces.append(jnp.where(lane_lo, sel_g[:, jb:jb + 1], sel_g[:, jb + 1:jb + 2]))
            biases.append(jnp.where(jnp.concatenate(pieces, axis=-1) > 0.5, 0.0, NEG_INF))
        parts.append((_dot(qr, kt), _group_rows(biases, NSA_GROUP), lambda p, vt=vt: _dot_nt(p, vt)))
    nb_past = past // SEL_BLOCK
    mask = (_group_rows([sel_g[:, nb_past:nb_past + 1] for sel_g in sel_groups], NSA_GROUP) > 0.5) & new_causal
    parts.append((_dot_nt(qr, snew[:, 0:LANES].astype(BF16)), mask,
                  lambda p: _dot(p, snew[:, LANES:2 * LANES].astype(BF16))))
    o_sel = _attend_segments(parts)

    o_win = _attend_segments([
        (_dot(qr, wst_ref[0:LANES, :].astype(BF16)), win_mask,
         lambda p: _dot_nt(p, wst_ref[LANES:2 * LANES, :].astype(BF16))),
        (_dot_nt(qr, wnew[:, 0:LANES].astype(BF16)), new_causal,
         lambda p: _dot(p, wnew[:, LANES:2 * LANES].astype(BF16)))])

    o_ref[...] = _mix_heads(gates_ref[...], ocmp_ref[...], o_sel, o_win, t_pad)


def _nsa_decode_sel(page_table, qr_b, snew_b, wnew_b, imp, ocmp, gates_b, win_state, pool, t_new, t_pad):
    nb, n_pages = page_table.shape
    past = n_pages * PAGE_SIZE
    tk = 1024
    assert past % tk == 0
    ns = past // SEL_BLOCK + (-(-t_new // SEL_BLOCK))
    per_b = lambda a: pl.BlockSpec((None,) + a.shape[1:], lambda b, pt: (b,) + (0,) * (a.ndim - 1))
    grid_spec = pltpu.PrefetchScalarGridSpec(
        num_scalar_prefetch=1,
        grid=(nb,),
        in_specs=[per_b(qr_b), per_b(snew_b), per_b(wnew_b), per_b(imp), per_b(ocmp), per_b(gates_b),
                  per_b(win_state), pl.BlockSpec(memory_space=pl.ANY)],
        out_specs=pl.BlockSpec((None, t_pad, NSA_WIDTH), lambda b, pt: (b, 0, 0)),
        scratch_shapes=[pltpu.VMEM((2, n_pages, NSA_KV_WIDTH, PAGE_SIZE), F32), pltpu.SemaphoreType.DMA((2,))],
    )
    return pl.pallas_call(
        functools.partial(_nsa_dec_sel_kernel, n_pages=n_pages, t_new=t_new, t_pad=t_pad, tk=tk, n_sel_blocks=ns),
        grid_spec=grid_spec,
        out_shape=jax.ShapeDtypeStruct((nb, t_pad, NSA_WIDTH), F32),
        compiler_params=_cparams(("arbitrary",)),
        name="nsa_decode_select_window",
    )(page_table, qr_b, snew_b, wnew_b, imp, ocmp, gates_b, win_state, pool)


def _prep_weights(w_in, q_norm_g, w_uq, kv_norm_g, w_uk, w_uv, cmp_pos_emb, cmp_w1, cmp_b1, cmp_w2, w_o, ln_g, ln_b):
    cuts = np.cumsum([MLA_Q_LORA, MLA_KV_LORA, MLA_ROPE_DIM, MLA_WIDTH, NSA_WIDTH, NSA_KV_WIDTH, NSA_KV_WIDTH,
                      NSA_KV_WIDTH, 3 * NSA_HEADS])[:].tolist()
    c_q, c_kv, k_r, z_mla, q_n, cmp_kv, sel_kv, win_kv, g_br, z_nsa = jnp.split(w_in, cuts, axis=1)
    pad = jnp.zeros((D_MODEL, LANES - MLA_ROPE_DIM - 3 * NSA_HEADS), w_in.dtype)
    w_in_p = jnp.concatenate([c_q, c_kv, z_mla, q_n, cmp_kv, sel_kv, win_kv, z_nsa, k_r, g_br, pad], axis=1)
    assert w_in_p.shape[1] == IN_WIDTH_P
    uq = w_uq.reshape(MLA_Q_LORA, MLA_HEADS, MLA_NOPE_DIM + MLA_ROPE_DIM)
    w_uq_p = jnp.concatenate([uq[:, :, :MLA_NOPE_DIM].reshape(MLA_Q_LORA, -1),
                              uq[:, :, MLA_NOPE_DIM:].reshape(MLA_Q_LORA, -1)], axis=1)
    w_uk_t = jnp.transpose(w_uk.reshape(MLA_KV_LORA, MLA_HEADS, MLA_NOPE_DIM), (1, 2, 0))
    w_uv_h = jnp.transpose(w_uv.reshape(MLA_KV_LORA, MLA_HEADS, MLA_V_DIM), (1, 0, 2))
    ratio = CMP_BLOCK // CMP_STRIDE
    eye = jnp.eye(NSA_KV_HEADS, dtype=w_in.dtype)
    w1r = cmp_w1.reshape(2, ratio, CMP_STRIDE, NSA_HEAD_DIM, NSA_HEAD_DIM)
    wc1 = jnp.einsum('gh,srpde->spgdrhe', eye, w1r).reshape(2, CMP_PAIRS, 2 * LANES, ratio * LANES)
    w2 = jnp.einsum('gh,sde->sgdhe', eye, cmp_w2).reshape(2, LANES, LANES)
    pe = jnp.transpose(cmp_pos_emb, (1, 0, 2))
    pe = jnp.concatenate([pe, pe], axis=-1).reshape(2, ratio, CMP_PAIRS, 2 * LANES)
    pe = jnp.transpose(pe, (0, 2, 1, 3))
    pe = jnp.pad(pe, ((0, 0), (0, 0), (0, 8 - ratio), (0, 0)))
    b1 = jnp.concatenate([cmp_b1, cmp_b1], axis=-1).reshape(2, 1, LANES)
    return dict(w_in=w_in_p.astype(BF16), q_norm_g=q_norm_g.reshape(1, -1), w_uq=w_uq_p.astype(BF16),
                w_uk=w_uk_t.astype(BF16), kv_norm_g=kv_norm_g.reshape(1, -1), w_uv=w_uv_h.astype(BF16),
                wc1=wc1.astype(BF16), w2=w2.astype(BF16), pe=pe, b1=b1,
                w_o=w_o.astype(BF16), ln_g=ln_g.reshape(1, -1), ln_b=ln_b.reshape(1, -1))


def _prompt_layer(x, wts):
    batch, seq, _ = x.shape
    x2d = x.reshape(batch * seq, D_MODEL)
    tm = min(512, seq)
    P = _project(x2d, _rope_tables(jnp.arange(seq, dtype=jnp.int32)), wts, tm, feature_major=True)
    o_mla = _mla_prompt(P['qmla'], P['mrowb'], wts['w_uv'], batch, seq)
    kvc = _compress_prompt(P['cmps'], wts, batch, seq)
    o_nsa = _nsa_prompt(P, kvc, batch, seq)
    y = _finish(x2d, o_mla, o_nsa, P['szm'], P['szn'], wts, tm)
    kvd = (2, NSA_KV_HEADS, NSA_HEAD_DIM)
    w_keep = min(WINDOW, seq)
    token_major = lambda a: jnp.swapaxes(a, 1, 2)
    return (y.reshape(batch, seq, D_MODEL),
            token_major(P['mrow']),
            token_major(P['cmp']).reshape((batch, seq) + kvd),
            token_major(P['sel']).reshape((batch, seq) + kvd),
            token_major(P['win'][:, :, seq - w_keep:]).reshape((batch, w_keep) + kvd))


def _sample_layer(x, cache_mla, cache_cmp, cache_sel, win_state, page_table, wts):
    nb, t_new, _ = x.shape
    n_pages = page_table.shape[1]
    past = n_pages * PAGE_SIZE
    n = nb * t_new
    t_pad = 8
    x2d = x.reshape(n, D_MODEL)
    pos = past + (jnp.arange(n, dtype=jnp.int32) % t_new)
    P = _project(x2d, _rope_tables(pos), wts, n)
    kvd = (2, NSA_KV_HEADS, NSA_HEAD_DIM)

    q_b = jnp.transpose(P['qmla'][:, :, :MLA_ROW].reshape(MLA_HEADS, nb, t_new, MLA_ROW), (1, 0, 2, 3))
    q_b = q_b.reshape(nb, -1, MLA_ROW)
    knew_b = jnp.pad(P['mrowb'][:, :MLA_ROW].reshape(nb, t_new, MLA_ROW), ((0, 0), (0, 16 - t_new), (0, 0)))
    feat_major = lambda a: jnp.swapaxes(a.reshape(a.shape[0], a.shape[1], -1), 1, 2)
    o_lat = _mla_decode(page_table, q_b, knew_b, feat_major(cache_mla))
    o_heads = jnp.transpose(o_lat.reshape(nb, MLA_HEADS, t_new, MLA_KV_LORA), (1, 0, 2, 3)).reshape(MLA_HEADS, n, -1)
    o_mla = _uv(o_heads, wts['w_uv'])

    def q_rows(q):
        q = jnp.transpose(q.reshape(NSA_HEADS, nb, t_new, LANES), (1, 0, 2, 3))
        q = jnp.pad(q, ((0, 0), (0, 0), (0, t_pad - t_new), (0, 0)))
        return q.reshape(nb, NSA_HEADS * t_pad, LANES)

    cnew_b = P['cmp'].reshape(nb, t_new, NSA_KV_WIDTH)
    ocmp, imp = _nsa_decode_cmp(page_table, q_rows(P['qn']), cnew_b, feat_major(cache_cmp), wts, t_new, t_pad)
    pad_rows = lambda a, r: jnp.pad(a.reshape(nb, t_new, -1), ((0, 0), (0, r - t_new), (0, 0)))
    o_nsa = _nsa_decode_sel(page_table, q_rows(P['qrot']), pad_rows(P['sel'], 16), pad_rows(P['win'], 16), imp,
                            ocmp, pad_rows(P['gates'], t_pad), feat_major(win_state), feat_major(cache_sel),
                            t_new, t_pad)
    o_nsa = o_nsa[:, :t_new].reshape(n, NSA_WIDTH)

    y = _finish(x2d, o_mla, o_nsa, P['szm'], P['szn'], wts, n)
    wst = win_state.reshape(nb, -1, NSA_KV_WIDTH)
    win_all = jnp.concatenate([wst, P['win'].reshape(nb, t_new, NSA_KV_WIDTH)], axis=1)[:, t_new:]
    return (y.reshape(nb, t_new, D_MODEL),
            P['mrow'].reshape(nb, t_new, MLA_ROW),
            P['cmp'].reshape((nb, t_new) + kvd),
            P['sel'].reshape((nb, t_new) + kvd),
            win_all.reshape((nb, win_all.shape[1]) + kvd))


def kernel(x_prompt, x_sample, cache_mla, cache_cmp_kv, cache_sel_kv, state_win_kv, page_table, w_in, q_norm_g, w_uq,
           kv_norm_g, w_uk, w_uv, cmp_pos_emb, cmp_w1, cmp_b1, cmp_w2, w_o, ln_g, ln_b):
    assert w_in.shape[0] == DEPTH
    wts = _prep_weights(w_in[0], q_norm_g[0], w_uq[0], kv_norm_g[0], w_uk[0], w_uv[0], cmp_pos_emb[0], cmp_w1[0],
                        cmp_b1[0], cmp_w2[0], w_o[0], ln_g[0], ln_b[0])
    yp, p_mla, p_cmp, p_sel, p_win = _prompt_layer(x_prompt, wts)
    drop_depth = lambda a: a.reshape(a.shape[1:])
    ys, s_mla, s_cmp, s_sel, s_win = _sample_layer(x_sample, drop_depth(cache_mla), drop_depth(cache_cmp_kv),
                                                   drop_depth(cache_sel_kv), drop_depth(state_win_kv), page_table, wts)
    add_depth = lambda a: a[None]
    return (yp, ys) + tuple(add_depth(a) for a in (p_mla, p_cmp, p_sel, p_win, s_mla, s_cmp, s_sel, s_win))
```

```python
import functools

import numpy as np
import jax
import jax.numpy as jnp
from jax import lax
from jax.experimental import pallas as pl
from jax.experimental.pallas import tpu as pltpu

F32 = jnp.float32
BF16 = jnp.bfloat16

D_MODEL = 1024
PAGE_SIZE = 128
MLA_HEADS = 8
MLA_V_DIM = 64
MLA_NOPE_DIM = 64
MLA_ROPE_DIM = 32
MLA_Q_LORA = 256
MLA_KV_LORA = 128
MLA_WIDTH = MLA_HEADS * MLA_V_DIM
MLA_ROW = MLA_KV_LORA + MLA_ROPE_DIM
NSA_HEADS = 8
NSA_KV_HEADS = 2
NSA_HEAD_DIM = 64
NSA_GROUP = NSA_HEADS // NSA_KV_HEADS
NSA_WIDTH = NSA_HEADS * NSA_HEAD_DIM
NSA_KV_WIDTH = 2 * NSA_KV_HEADS * NSA_HEAD_DIM
CMP_BLOCK = 32
CMP_STRIDE = 16
SEL_BLOCK = 64
SEL_TOP_N = 16
WINDOW = 512
ROPE_THETA = 10000.0
RMS_EPS = 1e-6
LN_EPS = 1e-5
NEG_INF = -1e30
MLA_SCALE = (MLA_NOPE_DIM + MLA_ROPE_DIM) ** -0.5
NSA_SCALE = NSA_HEAD_DIM ** -0.5
DEPTH = 1
ALPHA = (2 * DEPTH) ** 0.25

LANES = 128
LOG2E = 1.4426950408889634
MLA_KROW = 2 * LANES
MLA_ONES_LANE = MLA_ROW
NSA_KROW = 3 * LANES

SEG_CQ = (0, 256)
SEG_CKV = (256, 128)
SEG_ZMLA = (384, 512)
SEG_QN = (896, 512)
SEG_CMP = (1408, 256)
SEG_SEL = (1664, 256)
SEG_WIN = (1920, 256)
SEG_ZNSA = (2176, 512)
SEG_MISC = (2688, 128)
IN_WIDTH_P = 2816
MISC_GATE0 = MLA_ROPE_DIM

VMEM_LIMIT = 48 * 1024 * 1024

def _cparams(sem, flags=None):
    return pltpu.CompilerParams(dimension_semantics=sem, vmem_limit_bytes=VMEM_LIMIT, flags=flags)


def _sigmoid(x):
    return 1.0 / (1.0 + jnp.exp(-x))


def _silu(x):
    return x * _sigmoid(x)


def _rope_slab(x, c, sa, sb, half):
    return x * c + pltpu.roll(x, LANES - half, 1) * sa + pltpu.roll(x, half, 1) * sb


def _dot_nt(a, b):
    return lax.dot_general(a, b, (((1,), (1,)), ((), ())), preferred_element_type=F32)


def _dot(a, b):
    return jnp.dot(a, b, preferred_element_type=F32)


def _attend_segments(parts):
    is_bias = lambda mask: mask is not None and mask.dtype != jnp.bool_
    sm = [s if mask is None else (s + mask if is_bias(mask) else jnp.where(mask, s, NEG_INF))
          for s, mask, _ in parts]
    m = functools.reduce(jnp.maximum, [jnp.max(x, axis=-1, keepdims=True) for x in sm])
    l, acc = 0.0, 0.0
    for x, (_, mask, pv) in zip(sm, parts):
        p = jnp.exp2(x - m)
        if mask is not None and not is_bias(mask):
            p = jnp.where(mask, p, 0.0)
        l = l + jnp.sum(p, axis=-1, keepdims=True)
        acc = acc + pv(p.astype(BF16))
    ok = l > 0.0
    return jnp.where(ok, acc / jnp.where(ok, l, 1.0), 0.0)


def _flash_step(s, bias, v_ext, m, acc):
    sm = s if bias is None else s + bias
    m_new = jnp.maximum(m, jnp.max(sm, axis=-1, keepdims=True))
    p = jnp.exp2(sm - m_new)
    return m_new, jnp.exp2(m - m_new) * acc + _dot(p.astype(BF16), v_ext)


def _flash_result(acc, ones_lane):
    return acc[:, 0:LANES] * (1.0 / acc[:, ones_lane:ones_lane + 1])


def _tile_rows(x, reps):
    return jnp.concatenate([x] * reps, axis=0)


def _select_blocks(imp, q_pos, n_sel_blocks):
    rows, width = imp.shape
    j = lax.broadcasted_iota(jnp.int32, (rows, width), 1)
    cur = lax.shift_right_logical(q_pos, 6)
    avail = (j * SEL_BLOCK <= q_pos) & (j < n_sel_blocks)
    forced = (j == 0) | (j == cur) | (j == cur - 1)
    val = jnp.where(avail, jnp.where(forced, jnp.inf, imp), -jnp.inf)
    rank = jnp.zeros((rows, width), F32)
    for jp in range(n_sel_blocks):
        col = val[:, jp:jp + 1]
        ahead = (col > val) | ((col == val) & (j > jp))
        rank = rank + jnp.where(ahead, 1.0, 0.0)
    keep = (rank < float(min(SEL_TOP_N, n_sel_blocks))) & avail
    return jnp.where(keep, 1.0, 0.0)


def _select_blocks_t(imp_t, q_pos, n_sel_blocks):
    n_rows, width = imp_t.shape
    j = lax.broadcasted_iota(jnp.int32, (n_rows, width), 0)
    cur = lax.shift_right_logical(q_pos, 6)
    avail = (j * SEL_BLOCK <= q_pos) & (j < n_sel_blocks)
    forced = (j == 0) | (j == cur) | (j == cur - 1)
    val = jnp.where(avail, jnp.where(forced, jnp.inf, imp_t), -jnp.inf)
    n_pieces = -(-n_sel_blocks // 8)
    pieces = [val[8 * v:8 * v + 8] for v in range(n_pieces)]
    jrow = lax.broadcasted_iota(jnp.int32, (8, width), 0)
    ranks = [jnp.zeros((8, width), F32) for _ in range(n_pieces)]
    for jp in range(n_sel_blocks):
        cand = jnp.broadcast_to(val[jp:jp + 1], (8, width))
        for v in range(n_pieces):
            if 8 * v > jp:
                ahead = cand >= pieces[v]
            elif 8 * v + 7 < jp:
                ahead = cand > pieces[v]
            else:
                ahead = (cand > pieces[v]) | ((cand == pieces[v]) & (jrow > jp - 8 * v))
            ranks[v] = ranks[v] + jnp.where(ahead, 1.0, 0.0)
    if n_rows > 8 * n_pieces:
        ranks.append(jnp.zeros((n_rows - 8 * n_pieces, width), F32))
    rank = jnp.concatenate(ranks, axis=0)
    keep = (rank < float(min(SEL_TOP_N, n_sel_blocks))) & avail
    return jnp.where(keep, 1.0, 0.0)


def _proj_kernel(x_ref, w_ref, qg_ref, wuq_ref, wuk_ref, kvg_ref, tab_ref,
                 qmla_ref, mrow_ref, mrowb_ref, szm_ref, qn_ref, qrot_ref, cmp_ref, cmps_ref,
                 sel_ref, selb_ref, win_ref, winb_ref, gates_ref, szn_ref, *, feature_major):
    xb = x_ref[...].astype(BF16)

    def seg(s):
        return _dot(xb, w_ref[:, s[0]:s[0] + s[1]])

    cn, san, sbn = tab_ref[0], tab_ref[1], tab_ref[2]
    cm, sam, sbm = tab_ref[3], tab_ref[4], tab_ref[5]

    cq = seg(SEG_CQ)
    r = cq * lax.rsqrt(jnp.mean(cq * cq, axis=-1, keepdims=True) + RMS_EPS) * qg_ref[...]
    q = _dot(r.astype(BF16), wuq_ref[...])
    nope_w = MLA_HEADS * MLA_NOPE_DIM
    lane = lax.broadcasted_iota(jnp.int32, (q.shape[0], LANES), 1)
    for h in range(MLA_HEADS):
        ql = _dot(q[:, h * MLA_NOPE_DIM:(h + 1) * MLA_NOPE_DIM].astype(BF16), wuk_ref[h])
        qmla_ref[h, :, 0:LANES] = (ql * (MLA_SCALE * LOG2E)).astype(BF16)
    heads_per_slab = LANES // MLA_ROPE_DIM
    for jj in range(MLA_HEADS // heads_per_slab):
        qr = _rope_slab(q[:, nope_w + jj * LANES: nope_w + (jj + 1) * LANES], cm, sam, sbm, MLA_ROPE_DIM // 2)
        qr = qr * (MLA_SCALE * LOG2E)
        for hh in range(heads_per_slab):
            front = qr if hh == 0 else pltpu.roll(qr, LANES - hh * MLA_ROPE_DIM, 1)
            qmla_ref[jj * heads_per_slab + hh, :, LANES:2 * LANES] = (
                jnp.where(lane < MLA_ROPE_DIM, front, 0.0).astype(BF16))

    ckv = seg(SEG_CKV)
    lat = ckv * lax.rsqrt(jnp.mean(ckv * ckv, axis=-1, keepdims=True) + RMS_EPS) * kvg_ref[...]
    misc = seg(SEG_MISC)
    kr = _rope_slab(misc, cm, sam, sbm, MLA_ROPE_DIM // 2)
    if feature_major:
        mrow_ref[0:MLA_KV_LORA, :] = lat.T
        mrow_ref[MLA_KV_LORA:MLA_ROW, :] = kr.T[0:MLA_ROPE_DIM]
    else:
        mrow_ref[:, 0:MLA_KV_LORA] = lat
        mrow_ref[:, MLA_KV_LORA:MLA_ROW] = kr[:, 0:MLA_ROPE_DIM]
    mrowb_ref[:, 0:LANES] = lat.astype(BF16)
    ones_col = jnp.where(lane == MLA_ONES_LANE - LANES, 1.0, 0.0)
    mrowb_ref[:, LANES:2 * LANES] = jnp.where(lane < MLA_ROPE_DIM, kr, ones_col).astype(BF16)
    gates_ref[...] = _sigmoid(misc)

    szm_ref[...] = _silu(seg(SEG_ZMLA))
    szn_ref[...] = _silu(seg(SEG_ZNSA))

    qn = seg(SEG_QN)
    for jj in range(NSA_WIDTH // LANES):
        raw = qn[:, jj * LANES:(jj + 1) * LANES] * (NSA_SCALE * LOG2E)
        rot = _rope_slab(qn[:, jj * LANES:(jj + 1) * LANES], cn, san, sbn, NSA_HEAD_DIM // 2) * (NSA_SCALE * LOG2E)
        for src, dst in ((raw, qn_ref), (rot, qrot_ref)):
            swapped = pltpu.roll(src, NSA_HEAD_DIM, 1)
            for half in range(2):
                hd = 2 * jj + half
                g = hd // NSA_GROUP
                keep = (lane >= g * NSA_HEAD_DIM) & (lane < (g + 1) * NSA_HEAD_DIM)
                dst[hd] = jnp.where(keep, src if half == g else swapped, 0.0).astype(BF16)

    def put_rows(dst, first, second):
        if feature_major:
            dst[0:LANES, :] = first.T
            dst[LANES:2 * LANES, :] = second.T
        else:
            dst[:, 0:LANES] = first
            dst[:, LANES:2 * LANES] = second

    cmpv = seg(SEG_CMP)
    put_rows(cmp_ref, cmpv[:, 0:LANES], cmpv[:, LANES:2 * LANES])
    cmps_ref[0] = cmpv[:, 0:LANES]
    cmps_ref[1] = cmpv[:, LANES:2 * LANES]

    for src, dst, dstb in ((SEG_SEL, sel_ref, selb_ref), (SEG_WIN, win_ref, winb_ref)):
        kv = seg(src)
        k = _rope_slab(kv[:, 0:LANES], cn, san, sbn, NSA_HEAD_DIM // 2)
        v = kv[:, LANES:2 * LANES]
        put_rows(dst, k, v)
        dstb[:, 0:LANES] = k.astype(BF16)
        dstb[:, LANES:2 * LANES] = v.astype(BF16)
        dstb[:, 2 * LANES:3 * LANES] = jnp.where(lane == 0, 1.0, 0.0).astype(BF16)


def _rope_tables(pos):
    def tab(d):
        inv = 1.0 / (ROPE_THETA ** (jnp.arange(0, d, 2, dtype=F32) / d))
        ang = pos.astype(F32)[:, None] * inv[None, :]
        cos, sin = jnp.cos(ang), jnp.sin(ang)
        zero = jnp.zeros_like(sin)
        reps = LANES // d
        return (jnp.tile(jnp.concatenate([cos, cos], -1), (1, reps)),
                jnp.tile(jnp.concatenate([-sin, zero], -1), (1, reps)),
                jnp.tile(jnp.concatenate([zero, sin], -1), (1, reps)))
    return jnp.stack(tab(NSA_HEAD_DIM) + tab(MLA_ROPE_DIM))


def _project(x2d, tabs, wts, tm, feature_major=False):
    n = x2d.shape[0]
    p_rows = tabs.shape[1]
    assert n % tm == 0 and p_rows % tm == 0
    nt = p_rows // tm
    row = lambda w: pl.BlockSpec((tm, w), lambda i: (i, 0))
    full = lambda a: pl.BlockSpec(a.shape, lambda i: (0,) * a.ndim)
    if feature_major:
        out_rows = lambda w: ((n // p_rows, w, p_rows), F32, pl.BlockSpec((None, w, tm), lambda i: (i // nt, 0, i % nt)))
    else:
        out_rows = lambda w: ((n, w), F32, row(w))
    out_shapes = dict(
        qmla=((MLA_HEADS, n, MLA_KROW), BF16, pl.BlockSpec((MLA_HEADS, tm, MLA_KROW), lambda i: (0, i, 0))),
        mrow=out_rows(MLA_ROW),
        mrowb=((n, MLA_KROW), BF16, row(MLA_KROW)),
        szm=((n, MLA_WIDTH), F32, row(MLA_WIDTH)),
        qn=((NSA_HEADS, n, LANES), BF16, pl.BlockSpec((NSA_HEADS, tm, LANES), lambda i: (0, i, 0))),
        qrot=((NSA_HEADS, n, LANES), BF16, pl.BlockSpec((NSA_HEADS, tm, LANES), lambda i: (0, i, 0))),
        cmp=out_rows(NSA_KV_WIDTH),
        cmps=((2, n, LANES), F32, pl.BlockSpec((2, tm, LANES), lambda i: (0, i, 0))),
        sel=out_rows(NSA_KV_WIDTH),
        selb=((n, NSA_KROW), BF16, row(NSA_KROW)),
        win=out_rows(NSA_KV_WIDTH),
        winb=((n, NSA_KROW), BF16, row(NSA_KROW)),
        gates=((n, LANES), F32, row(LANES)),
        szn=((n, NSA_WIDTH), F32, row(NSA_WIDTH)),
    )
    names = list(out_shapes)
    outs = pl.pallas_call(
        functools.partial(_proj_kernel, feature_major=feature_major),
        grid=(n // tm,),
        in_specs=[row(D_MODEL), full(wts['w_in']), full(wts['q_norm_g']), full(wts['w_uq']), full(wts['w_uk']),
                  full(wts['kv_norm_g']), pl.BlockSpec((6, tm, LANES), lambda i: (0, i % nt, 0))],
        out_specs=[out_shapes[k][2] for k in names],
        out_shape=[jax.ShapeDtypeStruct(out_shapes[k][0], out_shapes[k][1]) for k in names],
        compiler_params=_cparams(("parallel",)),
        name="projection",
    )(x2d, wts['w_in'], wts['q_norm_g'], wts['w_uq'], wts['w_uk'], wts['kv_norm_g'], tabs)
    return dict(zip(names, outs))


def _uv_project(o, wuv_ref, rows):
    return jnp.concatenate(
        [_dot(o[h * rows:(h + 1) * rows].astype(BF16), wuv_ref[h]) for h in range(MLA_HEADS)], axis=-1)


def _flash_pipeline(n_chunks, prepare, scores, consume, s_a, s_b):
    scores(0, s_a)

    def body(i, _):
        c0 = 2 * i
        aux = prepare(c0)
        scores(c0 + 1, s_b)
        consume(c0, s_a, aux)

        @pl.when(c0 + 1 < n_chunks)
        def _():
            aux = prepare(c0 + 1)
            scores(c0 + 2, s_a)
            consume(c0 + 1, s_b, aux)
        return 0

    lax.fori_loop(0, (n_chunks + 1) // 2, body, 0)


def _mla_prompt_kernel(q_ref, k_ref, wuv_ref, o_ref, s_a, s_b, m_ref, acc_ref, *, tq, tk):
    qi = pl.program_id(1)
    rows = MLA_HEADS * tq
    q = q_ref[...].reshape(rows, MLA_KROW)
    n_chunks = (qi * tq) // tk + 1
    col = lax.broadcasted_iota(jnp.int32, (tq, tk), 1)
    q_pos = qi * tq + lax.broadcasted_iota(jnp.int32, (tq, tk), 0)
    m_ref[...] = jnp.full(m_ref.shape, NEG_INF, F32)
    acc_ref[...] = jnp.zeros(acc_ref.shape, F32)

    def chunk(c):
        return k_ref[pl.ds(pl.multiple_of(c * tk, tk), tk), :]

    def scores(c, dst):
        dst[...] = _dot_nt(q, chunk(jnp.minimum(c, n_chunks - 1)))

    def prepare(c):
        return _tile_rows(jnp.where(c * tk + col <= q_pos, 0.0, NEG_INF), MLA_HEADS)

    def consume(c, src, bias):
        m_ref[...], acc_ref[...] = _flash_step(src[...], bias, chunk(c), m_ref[...], acc_ref[...])

    _flash_pipeline(n_chunks, prepare, scores, consume, s_a, s_b)
    o_ref[...] = _uv_project(_flash_result(acc_ref[...], MLA_ONES_LANE), wuv_ref, tq)


def _mla_prompt(qmla, mrowb, wuv, batch, seq):
    tq = min(256, seq)
    tk = min(512, seq)
    nq = seq // tq
    assert tk % tq == 0 and seq % tk == 0
    return pl.pallas_call(
        functools.partial(_mla_prompt_kernel, tq=tq, tk=tk),
        grid=(batch, nq),
        in_specs=[pl.BlockSpec((MLA_HEADS, tq, MLA_KROW), lambda b, i: (0, b * nq + i, 0)),
                  pl.BlockSpec((None, seq, MLA_KROW), lambda b, i: (b, 0, 0)),
                  pl.BlockSpec(wuv.shape, lambda b, i: (0, 0, 0))],
        out_specs=pl.BlockSpec((tq, MLA_WIDTH), lambda b, i: (b * nq + i, 0)),
        out_shape=jax.ShapeDtypeStruct((batch * seq, MLA_WIDTH), F32),
        scratch_shapes=[pltpu.VMEM((MLA_HEADS * tq, tk), F32), pltpu.VMEM((MLA_HEADS * tq, tk), F32),
                        pltpu.VMEM((MLA_HEADS * tq, 1), F32), pltpu.VMEM((MLA_HEADS * tq, MLA_KROW), F32)],
        compiler_params=_cparams(("parallel", "parallel")),
        name="mla_prompt",
    )(qmla, mrowb.reshape(batch, seq, MLA_KROW), wuv)


CMP_PAIRS = CMP_STRIDE // 2
CHUNK_PITCH = 24


def _compress_slab(load_pair, wc1_ref, pe_ref, b1_ref, w2_ref, s, n_chunk):
    proj = jnp.zeros((n_chunk, 2 * LANES), F32)
    pe = jnp.zeros((8, 2 * LANES), F32)
    for j in range(CMP_PAIRS):
        w = wc1_ref[s, j]
        proj = proj + _dot(load_pair(j).astype(BF16), w)
        pe = pe + _dot(pe_ref[s, j].astype(BF16), w)
    bias = b1_ref[s] + pe[0:1, 0:LANES] + pe[1:2, LANES:2 * LANES]
    hid = proj[:, 0:LANES] + pltpu.roll(proj[:, LANES:2 * LANES], n_chunk - 1, 0) + bias
    return _dot(_silu(hid).astype(BF16), w2_ref[s])


def _compress_prompt_kernel(x_ref, wc1_ref, pe_ref, b1_ref, w2_ref, o_ref, *, n_chunk):
    for s in range(2):
        row = lambda p, s=s: x_ref[s, pl.ds(p, n_chunk, stride=CMP_STRIDE), :]
        load = lambda j, row=row: jnp.concatenate([row(2 * j), row(2 * j + 1)], axis=-1)
        o_ref[s] = _compress_slab(load, wc1_ref, pe_ref, b1_ref, w2_ref, s, n_chunk).astype(BF16)


def _compress_prompt(cmps, wts, batch, seq):
    n_chunk = seq // CMP_STRIDE
    full = lambda a: pl.BlockSpec(a.shape, lambda b: (0,) * a.ndim)
    return pl.pallas_call(
        functools.partial(_compress_prompt_kernel, n_chunk=n_chunk),
        grid=(batch,),
        in_specs=[pl.BlockSpec((2, seq, LANES), lambda b: (0, b, 0)),
                  full(wts['wc1']), full(wts['pe']), full(wts['b1']), full(wts['w2'])],
        out_specs=pl.BlockSpec((None, 2, n_chunk, LANES), lambda b: (b, 0, 0, 0)),
        out_shape=jax.ShapeDtypeStruct((batch, 2, n_chunk, LANES), BF16),
        compiler_params=_cparams(("parallel",)),
        name="compress_prompt",
    )(cmps, wts['wc1'], wts['pe'], wts['b1'], wts['w2'])


def _cmp_branch(q, kc, vc, q_pos_rows, n_cmp_blocks, impmap_ref, rows_t, imp_transposed=False):
    s = _dot_nt(q, kc)
    n = lax.broadcasted_iota(jnp.int32, (rows_t, s.shape[1]), 1)
    visible = (n * CMP_STRIDE + (CMP_BLOCK - 1) <= q_pos_rows[0:rows_t]) & (n < n_cmp_blocks)
    sm = s + _tile_rows(jnp.where(visible, 0.0, NEG_INF), NSA_HEADS)
    e = jnp.exp2(sm - jnp.max(sm, axis=-1, keepdims=True))
    any_visible = (q_pos_rows >= CMP_BLOCK - 1) & (n_cmp_blocks > 0)
    p = e * jnp.where(any_visible, 1.0 / jnp.sum(e, axis=-1, keepdims=True), 0.0)
    o_cmp = _dot(p.astype(BF16), vc)
    imps = []
    for g in range(NSA_KV_HEADS):
        base = g * NSA_GROUP * rows_t
        psum = p[base:base + rows_t]
        for hh in range(1, NSA_GROUP):
            psum = psum + p[base + hh * rows_t: base + (hh + 1) * rows_t]
        hi = psum.astype(BF16)
        lo = (psum - hi.astype(F32)).astype(BF16)
        if imp_transposed:
            imps.append(_dot_nt(impmap_ref[...], hi) + _dot_nt(impmap_ref[...], lo))
        else:
            imps.append(_dot(hi, impmap_ref[...]) + _dot(lo, impmap_ref[...]))
    return o_cmp, imps


def _mix_heads(gates, o_cmp, o_sel, o_win, rows_t):
    outs = []
    for hd in range(NSA_HEADS):
        g = hd // NSA_GROUP
        rs = slice(hd * rows_t, (hd + 1) * rows_t)
        ls = slice(g * NSA_HEAD_DIM, (g + 1) * NSA_HEAD_DIM)
        gcol = lambda br: gates[:, MISC_GATE0 + br * NSA_HEADS + hd: MISC_GATE0 + br * NSA_HEADS + hd + 1]
        outs.append(gcol(0) * o_cmp[rs, ls] + gcol(1) * o_sel[rs, ls] + gcol(2) * o_win[rs, ls])
    return jnp.concatenate(outs, axis=-1)


def _group_rows(per_group, reps):
    return jnp.concatenate([per_group[g] for g in range(NSA_KV_HEADS) for _ in range(reps)], axis=0)


def _nsa_prompt_kernel(qn_ref, qrot_ref, kvc_ref, sel_ref, win_ref, gates_ref, impmap_ref, o_ref,
                       s_a, s_b, m_ref, acc_ref, *, tq, tk, n_cmp_blocks, n_sel_blocks, wlen):
    qi = pl.program_id(1)
    rows = NSA_HEADS * tq
    q_pos_t = qi * tq + lax.broadcasted_iota(jnp.int32, (tq, 1), 0)
    q_pos_r = qi * tq + (lax.broadcasted_iota(jnp.int32, (rows, 1), 0) & (tq - 1))
    qn = qn_ref[...].reshape(rows, LANES)
    qrot = qrot_ref[...].reshape(rows, LANES)
    init = (jnp.full((rows, 1), NEG_INF, F32), jnp.zeros((rows, 2 * LANES), F32))

    w_start = pl.multiple_of(jnp.maximum(qi * tq + tq - wlen, 0), tq)
    d = q_pos_t - (w_start + lax.broadcasted_iota(jnp.int32, (tq, wlen), 1))
    bias = _tile_rows(jnp.where((d >= 0) & (d < WINDOW), 0.0, NEG_INF), NSA_HEADS)
    _, acc = _flash_step(_dot_nt(qrot, win_ref[pl.ds(w_start, wlen), 0:LANES]), bias,
                         win_ref[pl.ds(w_start, wlen), LANES:3 * LANES], *init)
    o_win = _flash_result(acc, LANES)

    o_cmp, imps_t = _cmp_branch(qn, kvc_ref[0], kvc_ref[1], q_pos_r, n_cmp_blocks, impmap_ref, tq,
                                imp_transposed=True)
    q_pos_lane = qi * tq + lax.broadcasted_iota(jnp.int32, (1, tq), 1)
    selms = [_select_blocks_t(imp_t, q_pos_lane, n_sel_blocks).T.astype(BF16) for imp_t in imps_t]

    blk_per_chunk = tk // SEL_BLOCK
    n_chunks = (qi * tq + tq + tk - 1) // tk
    e_row = lax.broadcasted_iota(jnp.int32, (LANES, tk), 0)
    e_col = lax.shift_right_logical(lax.broadcasted_iota(jnp.int32, (LANES, tk), 1), 6)
    col_t = lax.broadcasted_iota(jnp.int32, (tq, tk), 1)

    def sel_scores(c, dst):
        start = pl.multiple_of(jnp.minimum(c, n_chunks - 1) * tk, tk)
        dst[...] = _dot_nt(qrot, sel_ref[pl.ds(start, tk), 0:LANES])

    def sel_prepare(c):
        expand = jnp.where(e_row == c * blk_per_chunk + e_col, 1.0, 0.0).astype(BF16)
        causal = (c * tk + col_t) <= q_pos_t
        return _group_rows([jnp.where((_dot(sm, expand) > 0.5) & causal, 0.0, NEG_INF) for sm in selms], NSA_GROUP)

    def sel_consume(c, src, bias):
        v_ext = sel_ref[pl.ds(pl.multiple_of(c * tk, tk), tk), LANES:3 * LANES]
        m_ref[...], acc_ref[...] = _flash_step(src[...], bias, v_ext, m_ref[...], acc_ref[...])

    m_ref[...], acc_ref[...] = init
    _flash_pipeline(n_chunks, sel_prepare, sel_scores, sel_consume, s_a, s_b)
    o_sel = _flash_result(acc_ref[...], LANES)

    o_ref[...] = _mix_heads(gates_ref[...], o_cmp, o_sel, o_win, tq)


def _importance_map(nc_rows, nc, ns, width):
    i = np.arange(nc_rows)[:, None]
    j = np.arange(width)[None, :]
    lo = np.maximum(i * CMP_STRIDE, j * SEL_BLOCK)
    hi = np.minimum(i * CMP_STRIDE + CMP_BLOCK, (j + 1) * SEL_BLOCK)
    m = np.maximum(hi - lo, 0).astype(np.float32) / CMP_BLOCK
    m = m * (i < nc) * (j < ns)
    return jnp.asarray(m, dtype=BF16)


def _nsa_prompt(P, kvc, batch, seq):
    tq = min(256, seq)
    tk = min(512, seq)
    nq = seq // tq
    n_chunk = seq // CMP_STRIDE
    nc = n_chunk - CMP_BLOCK // CMP_STRIDE + 1
    ns = seq // SEL_BLOCK
    assert ns <= LANES and seq % tk == 0 and seq % CMP_STRIDE == 0 and seq % SEL_BLOCK == 0
    wlen = min(WINDOW + tq, seq)
    impmap = _importance_map(n_chunk, nc, ns, LANES).T
    blk = lambda w: pl.BlockSpec((tq, w), lambda b, i: (b * nq + i, 0))
    qblk = pl.BlockSpec((NSA_HEADS, tq, LANES), lambda b, i: (0, b * nq + i, 0))
    per_b = lambda a: pl.BlockSpec((None,) + a.shape[1:], lambda b, i: (b,) + (0,) * (a.ndim - 1))
    selb = P['selb'].reshape(batch, seq, NSA_KROW)
    winb = P['winb'].reshape(batch, seq, NSA_KROW)
    return pl.pallas_call(
        functools.partial(_nsa_prompt_kernel, tq=tq, tk=tk, n_cmp_blocks=nc, n_sel_blocks=ns, wlen=wlen),
        grid=(batch, nq),
        in_specs=[qblk, qblk, per_b(kvc), per_b(selb), per_b(winb), blk(LANES),
                  pl.BlockSpec(impmap.shape, lambda b, i: (0, 0))],
        out_specs=blk(NSA_WIDTH),
        out_shape=jax.ShapeDtypeStruct((batch * seq, NSA_WIDTH), F32),
        scratch_shapes=[pltpu.VMEM((NSA_HEADS * tq, tk), F32), pltpu.VMEM((NSA_HEADS * tq, tk), F32),
                        pltpu.VMEM((NSA_HEADS * tq, 1), F32), pltpu.VMEM((NSA_HEADS * tq, 2 * LANES), F32)],
        compiler_params=_cparams(("parallel", "parallel")),
        name="nsa_prompt",
    )(P['qn'], P['qrot'], kvc, selb, winb, P['gates'], impmap)


def _finish_kernel(x_ref, omla_ref, onsa_ref, szm_ref, szn_ref, wo_ref, g_ref, b_ref, y_ref):
    mixed = jnp.concatenate([omla_ref[...] * szm_ref[...], onsa_ref[...] * szn_ref[...]], axis=-1)
    h = ALPHA * x_ref[...] + _dot(mixed.astype(BF16), wo_ref[...])
    mu = jnp.mean(h, axis=-1, keepdims=True)
    d = h - mu
    var = jnp.mean(d * d, axis=-1, keepdims=True)
    y_ref[...] = d * lax.rsqrt(var + LN_EPS) * g_ref[...] + b_ref[...]


def _finish(x2d, omla, onsa, szm, szn, wts, tm):
    n = x2d.shape[0]
    row = lambda w: pl.BlockSpec((tm, w), lambda i: (i, 0))
    full = lambda a: pl.BlockSpec(a.shape, lambda i: (0,) * a.ndim)
    return pl.pallas_call(
        _finish_kernel,
        grid=(n // tm,),
        in_specs=[row(D_MODEL), row(MLA_WIDTH), row(NSA_WIDTH), row(MLA_WIDTH), row(NSA_WIDTH),
                  full(wts['w_o']), full(wts['ln_g']), full(wts['ln_b'])],
        out_specs=row(D_MODEL),
        out_shape=jax.ShapeDtypeStruct((n, D_MODEL), F32),
        compiler_params=_cparams(("parallel",)),
        name="finish",
    )(x2d, omla, onsa, szm, szn, wts['w_o'], wts['ln_g'], wts['ln_b'])


def _page_copies(pt_ref, b, n_pages, make_copy):
    assert n_pages % 2 == 0

    def start(i, _):
        make_copy(pt_ref[b, 2 * i], 2 * i).start(priority=0)
        make_copy(pt_ref[b, 2 * i + 1], 2 * i + 1).start(priority=1)
        return 0
    lax.fori_loop(0, n_pages // 2, start, 0, unroll=4)


def _page_wait_all(pool_ref, slot_buf, sem):
    n_pages = slot_buf.shape[0]
    pltpu.make_async_copy(pool_ref.at[pl.ds(0, n_pages)], slot_buf, sem).wait()


def _mla_decode_kernel(pt_ref, q_ref, knew_ref, pool_ref, o_ref, kbuf, sem, *, n_pages, t_new, tk):
    b = pl.program_id(0)
    nb = pl.num_programs(0)
    slot = b % 2
    past = n_pages * PAGE_SIZE

    def copy(sl):
        return lambda page, p: pltpu.make_async_copy(pool_ref.at[page], kbuf.at[sl, p], sem.at[sl])

    @pl.when(b == 0)
    def _():
        _page_copies(pt_ref, 0, n_pages, copy(0))

    @pl.when(b + 1 < nb)
    def _():
        _page_copies(pt_ref, b + 1, n_pages, copy(1 - slot))

    _page_wait_all(pool_ref, kbuf.at[slot], sem.at[slot])

    q = q_ref[...]
    rows = q.shape[0]
    pages_per_chunk = tk // PAGE_SIZE
    parts = []
    for c in range(n_pages // pages_per_chunk):
        kt = jnp.concatenate([kbuf[slot, c * pages_per_chunk + i].astype(BF16) for i in range(pages_per_chunk)],
                             axis=1)
        parts.append((_dot(q, kt), None, lambda p, kt=kt: _dot_nt(p, kt[0:MLA_KV_LORA])))
    kn = knew_ref[...]
    t_q = lax.broadcasted_iota(jnp.int32, (rows, kn.shape[0]), 0) & (t_new - 1)
    t_k = lax.broadcasted_iota(jnp.int32, (rows, kn.shape[0]), 1)
    parts.append((_dot_nt(q, kn), t_k <= t_q, lambda p: _dot(p, kn[:, 0:MLA_KV_LORA])))
    o_ref[...] = _attend_segments(parts)


def _mla_decode(page_table, q_b, knew_b, pool):
    nb, n_pages = page_table.shape
    rows = q_b.shape[1]
    t_new = rows // MLA_HEADS
    tk = 1024
    past = n_pages * PAGE_SIZE
    assert past % tk == 0 and (t_new & (t_new - 1)) == 0
    grid_spec = pltpu.PrefetchScalarGridSpec(
        num_scalar_prefetch=1,
        grid=(nb,),
        in_specs=[pl.BlockSpec((None, rows, MLA_ROW), lambda b, pt: (b, 0, 0)),
                  pl.BlockSpec((None,) + knew_b.shape[1:], lambda b, pt: (b, 0, 0)),
                  pl.BlockSpec(memory_space=pl.ANY)],
        out_specs=pl.BlockSpec((None, rows, MLA_KV_LORA), lambda b, pt: (b, 0, 0)),
        scratch_shapes=[pltpu.VMEM((2, n_pages, MLA_ROW, PAGE_SIZE), F32), pltpu.SemaphoreType.DMA((2,))],
    )
    return pl.pallas_call(
        functools.partial(_mla_decode_kernel, n_pages=n_pages, t_new=t_new, tk=tk),
        grid_spec=grid_spec,
        out_shape=jax.ShapeDtypeStruct((nb, rows, MLA_KV_LORA), F32),
        compiler_params=_cparams(("arbitrary",)),
        name="mla_decode",
    )(page_table, q_b, knew_b, pool)


def _uv_kernel(o_ref, wuv_ref, y_ref):
    rows = o_ref.shape[1]
    y_ref[...] = _uv_project(o_ref[...].reshape(MLA_HEADS * rows, MLA_KV_LORA), wuv_ref, rows)


def _uv(o_heads, wuv):
    rows = o_heads.shape[1]
    return pl.pallas_call(
        _uv_kernel,
        out_shape=jax.ShapeDtypeStruct((rows, MLA_WIDTH), F32),
        name="mla_value_up",
    )(o_heads, wuv)


def _nsa_dec_cmp_kernel(pt_ref, cnew_ref, pool_ref, wc1_ref, pe_ref, b1_ref, w2_ref, kvc_ref, pbuf, xrows_k, xrows_v,
                        sem, *, n_pages, t_new, n_chunk, n_cmp_blocks):
    b = pl.program_id(0)
    nb = pl.num_programs(0)
    slot = b % 2
    past = n_pages * PAGE_SIZE

    def copy(sl):
        return lambda page, p: pltpu.make_async_copy(pool_ref.at[page], pbuf.at[sl, p], sem.at[sl])

    @pl.when(b == 0)
    def _():
        _page_copies(pt_ref, 0, n_pages, copy(0))

    @pl.when(b + 1 < nb)
    def _():
        _page_copies(pt_ref, b + 1, n_pages, copy(1 - slot))

    _page_wait_all(pool_ref, pbuf.at[slot], sem.at[slot])

    chunks_per_page = PAGE_SIZE // CMP_STRIDE
    xrows = (xrows_k, xrows_v)

    def relayout(p, s):
        r0 = p * (chunks_per_page * CHUNK_PITCH)
        if not isinstance(p, int):
            r0 = pl.multiple_of(r0, 8)
        t = pbuf[slot, p, s * LANES:(s + 1) * LANES, :].astype(BF16).T.astype(F32)
        for c in range(chunks_per_page):
            xrows[s][pl.ds(r0 + c * CHUNK_PITCH, CMP_STRIDE), :] = t[c * CMP_STRIDE:(c + 1) * CMP_STRIDE]

    cnew = cnew_ref[...]
    base = n_pages * chunks_per_page * CHUNK_PITCH
    tail = xrows_k.shape[0] - base
    for s in range(2):
        xrows[s][pl.ds(base, tail), :] = jnp.zeros((tail, LANES), F32)
        xrows[s][pl.ds(base, t_new), :] = cnew[:, s * LANES:(s + 1) * LANES]

    def compress(s):
        row = lambda p: xrows[s][pl.ds(p, n_chunk, stride=CHUNK_PITCH), :]
        load = lambda j: jnp.concatenate([row(2 * j), row(2 * j + 1)], axis=-1)
        return _compress_slab(load, wc1_ref, pe_ref, b1_ref, w2_ref, s, n_chunk).astype(BF16)

    def relayout_keys(p, _):
        relayout(p, 0)
        return 0

    lax.fori_loop(0, n_pages, relayout_keys, 0, unroll=16)
    for p in range(n_pages):
        relayout(p, 1)
    for s in range(2):
        kvc_ref[s] = compress(s)[0:n_cmp_blocks]


def _decode_block_counts(past, t_new):
    n_chunk_true = -(-(past + t_new) // CMP_STRIDE)
    nc = n_chunk_true - CMP_BLOCK // CMP_STRIDE + 1
    ns = past // SEL_BLOCK + (-(-t_new // SEL_BLOCK))
    return -(-n_chunk_true // 8) * 8, nc, ns


def _nsa_decode_cmp(page_table, cnew_b, pool, wts, t_new):
    nb, n_pages = page_table.shape
    n_chunk, nc, _ = _decode_block_counts(n_pages * PAGE_SIZE, t_new)
    assert nc % LANES == 0, "compressed-block count must be lane aligned"
    full = lambda a: pl.BlockSpec(a.shape, lambda b, pt: (0,) * a.ndim)
    per_b = lambda a: pl.BlockSpec((None,) + a.shape[1:], lambda b, pt: (b,) + (0,) * (a.ndim - 1))
    grid_spec = pltpu.PrefetchScalarGridSpec(
        num_scalar_prefetch=1,
        grid=(nb,),
        in_specs=[per_b(cnew_b), pl.BlockSpec(memory_space=pl.ANY),
                  full(wts['wc1']), full(wts['pe']), full(wts['b1']), full(wts['w2'])],
        out_specs=pl.BlockSpec((None, 2, nc, LANES), lambda b, pt: (b, 0, 0, 0)),
        scratch_shapes=[pltpu.VMEM((2, n_pages, NSA_KV_WIDTH, PAGE_SIZE), F32),
                        pltpu.VMEM((n_chunk * CHUNK_PITCH, LANES), F32),
                        pltpu.VMEM((n_chunk * CHUNK_PITCH, LANES), F32),
                        pltpu.SemaphoreType.DMA((2,))],
    )
    return pl.pallas_call(
        functools.partial(_nsa_dec_cmp_kernel, n_pages=n_pages, t_new=t_new, n_chunk=n_chunk, n_cmp_blocks=nc),
        grid_spec=grid_spec,
        out_shape=jax.ShapeDtypeStruct((nb, 2, nc, LANES), BF16),
        compiler_params=_cparams(("arbitrary",)),
        name="nsa_decode_compress",
    )(page_table, cnew_b, pool, wts['wc1'], wts['pe'], wts['b1'], wts['w2'])


def _nsa_dec_sel_kernel(pt_ref, qn_ref, qr_ref, snew_ref, wnew_ref, kvc_ref, impmap_ref, gates_ref, wst_ref, pool_ref,
                        o_ref, kbuf, sem, *, n_pages, t_new, t_pad, tk, n_cmp_blocks, n_sel_blocks):
    b = pl.program_id(0)
    nb = pl.num_programs(0)
    slot = b % 2
    past = n_pages * PAGE_SIZE
    rows = NSA_HEADS * t_pad

    def copy(sl):
        return lambda page, p: pltpu.make_async_copy(pool_ref.at[page], kbuf.at[sl, p], sem.at[sl])

    @pl.when(b == 0)
    def _():
        _page_copies(pt_ref, 0, n_pages, copy(0))

    @pl.when(b + 1 < nb)
    def _():
        _page_copies(pt_ref, b + 1, n_pages, copy(1 - slot))

    _page_wait_all(pool_ref, kbuf.at[slot], sem.at[slot])

    blk_per_chunk = tk // SEL_BLOCK
    lane_lo = lax.broadcasted_iota(jnp.int32, (t_pad, LANES), 1) < SEL_BLOCK
    n_new = snew_ref.shape[0]
    t_q = lax.broadcasted_iota(jnp.int32, (rows, n_new), 0) & (t_pad - 1)
    t_k = lax.broadcasted_iota(jnp.int32, (rows, n_new), 1)
    new_causal = (t_k <= t_q) & (t_k < t_new)
    w_buf = wst_ref.shape[1]
    t_qw = lax.broadcasted_iota(jnp.int32, (rows, w_buf), 0) & (t_pad - 1)
    i_w = lax.broadcasted_iota(jnp.int32, (rows, w_buf), 1)
    win_mask = (t_qw + w_buf - i_w) < WINDOW
    snew = snew_ref[...]
    wnew = wnew_ref[...]
    qr = qr_ref[...]
    q_pos_r = past + (lax.broadcasted_iota(jnp.int32, (rows, 1), 0) & (t_pad - 1))
    q_pos_t = past + lax.broadcasted_iota(jnp.int32, (t_pad, 1), 0)
    o_cmp, imps = _cmp_branch(qn_ref[...], kvc_ref[0], kvc_ref[1], q_pos_r, n_cmp_blocks, impmap_ref, t_pad)
    sel_groups = [_select_blocks(imp, q_pos_t, n_sel_blocks) for imp in imps]

    parts = []
    pages_per_chunk = tk // PAGE_SIZE
    for c in range(past // tk):
        chunk_pages = range(c * pages_per_chunk, (c + 1) * pages_per_chunk)
        kt = jnp.concatenate([kbuf[slot, p, 0:LANES, :].astype(BF16) for p in chunk_pages], axis=1)
        vt = jnp.concatenate([kbuf[slot, p, LANES:2 * LANES, :].astype(BF16) for p in chunk_pages], axis=1)
        biases = []
        for sel_g in sel_groups:
            pieces = []
            for i in range(tk // LANES):
                jb = c * blk_per_chunk + 2 * i
                pieces.append(jnp.where(lane_lo, sel_g[:, jb:jb + 1], sel_g[:, jb + 1:jb + 2]))
            biases.append(jnp.where(jnp.concatenate(pieces, axis=-1) > 0.5, 0.0, NEG_INF))
        parts.append((_dot(qr, kt), _group_rows(biases, NSA_GROUP), lambda p, vt=vt: _dot_nt(p, vt)))
    nb_past = past // SEL_BLOCK
    mask = (_group_rows([sel_g[:, nb_past:nb_past + 1] for sel_g in sel_groups], NSA_GROUP) > 0.5) & new_causal
    parts.append((_dot_nt(qr, snew[:, 0:LANES].astype(BF16)), mask,
                  lambda p: _dot(p, snew[:, LANES:2 * LANES].astype(BF16))))
    o_sel = _attend_segments(parts)

    o_win = _attend_segments([
        (_dot(qr, wst_ref[0:LANES, :].astype(BF16)), win_mask,
         lambda p: _dot_nt(p, wst_ref[LANES:2 * LANES, :].astype(BF16))),
        (_dot_nt(qr, wnew[:, 0:LANES].astype(BF16)), new_causal,
         lambda p: _dot(p, wnew[:, LANES:2 * LANES].astype(BF16)))])

    o_ref[...] = _mix_heads(gates_ref[...], o_cmp, o_sel, o_win, t_pad)


def _nsa_decode_sel(page_table, qn_b, qr_b, snew_b, wnew_b, kvc, gates_b, win_state, pool, t_new, t_pad):
    nb, n_pages = page_table.shape
    past = n_pages * PAGE_SIZE
    tk = 1024
    assert past % tk == 0
    _, nc, ns = _decode_block_counts(past, t_new)
    impmap = _importance_map(nc, nc, ns, -(-ns // LANES) * LANES)
    per_b = lambda a: pl.BlockSpec((None,) + a.shape[1:], lambda b, pt: (b,) + (0,) * (a.ndim - 1))
    grid_spec = pltpu.PrefetchScalarGridSpec(
        num_scalar_prefetch=1,
        grid=(nb,),
        in_specs=[per_b(qn_b), per_b(qr_b), per_b(snew_b), per_b(wnew_b), per_b(kvc),
                  pl.BlockSpec(impmap.shape, lambda b, pt: (0, 0)), per_b(gates_b),
                  per_b(win_state), pl.BlockSpec(memory_space=pl.ANY)],
        out_specs=pl.BlockSpec((None, t_pad, NSA_WIDTH), lambda b, pt: (b, 0, 0)),
        scratch_shapes=[pltpu.VMEM((2, n_pages, NSA_KV_WIDTH, PAGE_SIZE), F32), pltpu.SemaphoreType.DMA((2,))],
    )
    return pl.pallas_call(
        functools.partial(_nsa_dec_sel_kernel, n_pages=n_pages, t_new=t_new, t_pad=t_pad, tk=tk, n_cmp_blocks=nc,
                          n_sel_blocks=ns),
        grid_spec=grid_spec,
        out_shape=jax.ShapeDtypeStruct((nb, t_pad, NSA_WIDTH), F32),
        compiler_params=_cparams(("arbitrary",)),
        name="nsa_decode_select_window",
    )(page_table, qn_b, qr_b, snew_b, wnew_b, kvc, impmap, gates_b, win_state, pool)


def _prep_weights(w_in, q_norm_g, w_uq, kv_norm_g, w_uk, w_uv, cmp_pos_emb, cmp_w1, cmp_b1, cmp_w2, w_o, ln_g, ln_b):
    cuts = np.cumsum([MLA_Q_LORA, MLA_KV_LORA, MLA_ROPE_DIM, MLA_WIDTH, NSA_WIDTH, NSA_KV_WIDTH, NSA_KV_WIDTH,
                      NSA_KV_WIDTH, 3 * NSA_HEADS])[:].tolist()
    c_q, c_kv, k_r, z_mla, q_n, cmp_kv, sel_kv, win_kv, g_br, z_nsa = jnp.split(w_in, cuts, axis=1)
    pad = jnp.zeros((D_MODEL, LANES - MLA_ROPE_DIM - 3 * NSA_HEADS), w_in.dtype)
    w_in_p = jnp.concatenate([c_q, c_kv, z_mla, q_n, cmp_kv, sel_kv, win_kv, z_nsa, k_r, g_br, pad], axis=1)
    assert w_in_p.shape[1] == IN_WIDTH_P
    uq = w_uq.reshape(MLA_Q_LORA, MLA_HEADS, MLA_NOPE_DIM + MLA_ROPE_DIM)
    w_uq_p = jnp.concatenate([uq[:, :, :MLA_NOPE_DIM].reshape(MLA_Q_LORA, -1),
                              uq[:, :, MLA_NOPE_DIM:].reshape(MLA_Q_LORA, -1)], axis=1)
    w_uk_t = jnp.transpose(w_uk.reshape(MLA_KV_LORA, MLA_HEADS, MLA_NOPE_DIM), (1, 2, 0))
    w_uv_h = jnp.transpose(w_uv.reshape(MLA_KV_LORA, MLA_HEADS, MLA_V_DIM), (1, 0, 2))
    ratio = CMP_BLOCK // CMP_STRIDE
    eye = jnp.eye(NSA_KV_HEADS, dtype=w_in.dtype)
    w1r = cmp_w1.reshape(2, ratio, CMP_STRIDE, NSA_HEAD_DIM, NSA_HEAD_DIM)
    wc1 = jnp.einsum('gh,srpde->spgdrhe', eye, w1r).reshape(2, CMP_PAIRS, 2 * LANES, ratio * LANES)
    w2 = jnp.einsum('gh,sde->sgdhe', eye, cmp_w2).reshape(2, LANES, LANES)
    pe = jnp.transpose(cmp_pos_emb, (1, 0, 2))
    pe = jnp.concatenate([pe, pe], axis=-1).reshape(2, ratio, CMP_PAIRS, 2 * LANES)
    pe = jnp.transpose(pe, (0, 2, 1, 3))
    pe = jnp.pad(pe, ((0, 0), (0, 0), (0, 8 - ratio), (0, 0)))
    b1 = jnp.concatenate([cmp_b1, cmp_b1], axis=-1).reshape(2, 1, LANES)
    return dict(w_in=w_in_p.astype(BF16), q_norm_g=q_norm_g.reshape(1, -1), w_uq=w_uq_p.astype(BF16),
                w_uk=w_uk_t.astype(BF16), kv_norm_g=kv_norm_g.reshape(1, -1), w_uv=w_uv_h.astype(BF16),
                wc1=wc1.astype(BF16), w2=w2.astype(BF16), pe=pe, b1=b1,
                w_o=w_o.astype(BF16), ln_g=ln_g.reshape(1, -1), ln_b=ln_b.reshape(1, -1))


def _prompt_layer(x, wts):
    batch, seq, _ = x.shape
    x2d = x.reshape(batch * seq, D_MODEL)
    tm = min(512, seq)
    P = _project(x2d, _rope_tables(jnp.arange(seq, dtype=jnp.int32)), wts, tm, feature_major=True)
    o_mla = _mla_prompt(P['qmla'], P['mrowb'], wts['w_uv'], batch, seq)
    kvc = _compress_prompt(P['cmps'], wts, batch, seq)
    o_nsa = _nsa_prompt(P, kvc, batch, seq)
    y = _finish(x2d, o_mla, o_nsa, P['szm'], P['szn'], wts, tm)
    kvd = (2, NSA_KV_HEADS, NSA_HEAD_DIM)
    w_keep = min(WINDOW, seq)
    token_major = lambda a: jnp.swapaxes(a, 1, 2)
    return (y.reshape(batch, seq, D_MODEL),
            token_major(P['mrow']),
            token_major(P['cmp']).reshape((batch, seq) + kvd),
            token_major(P['sel']).reshape((batch, seq) + kvd),
            token_major(P['win'][:, :, seq - w_keep:]).reshape((batch, w_keep) + kvd))


def _sample_layer(x, cache_mla, cache_cmp, cache_sel, win_state, page_table, wts):
    nb, t_new, _ = x.shape
    n_pages = page_table.shape[1]
    past = n_pages * PAGE_SIZE
    n = nb * t_new
    t_pad = 8
    x2d = x.reshape(n, D_MODEL)
    pos = past + (jnp.arange(n, dtype=jnp.int32) % t_new)
    P = _project(x2d, _rope_tables(pos), wts, n)
    kvd = (2, NSA_KV_HEADS, NSA_HEAD_DIM)

    q_b = jnp.transpose(P['qmla'][:, :, :MLA_ROW].reshape(MLA_HEADS, nb, t_new, MLA_ROW), (1, 0, 2, 3))
    q_b = q_b.reshape(nb, -1, MLA_ROW)
    knew_b = jnp.pad(P['mrowb'][:, :MLA_ROW].reshape(nb, t_new, MLA_ROW), ((0, 0), (0, 16 - t_new), (0, 0)))
    feat_major = lambda a: jnp.swapaxes(a.reshape(a.shape[0], a.shape[1], -1), 1, 2)
    o_lat = _mla_decode(page_table, q_b, knew_b, feat_major(cache_mla))
    o_heads = jnp.transpose(o_lat.reshape(nb, MLA_HEADS, t_new, MLA_KV_LORA), (1, 0, 2, 3)).reshape(MLA_HEADS, n, -1)
    o_mla = _uv(o_heads, wts['w_uv'])

    def q_rows(q):
        q = jnp.transpose(q.reshape(NSA_HEADS, nb, t_new, LANES), (1, 0, 2, 3))
        q = jnp.pad(q, ((0, 0), (0, 0), (0, t_pad - t_new), (0, 0)))
        return q.reshape(nb, NSA_HEADS * t_pad, LANES)

    cnew_b = P['cmp'].reshape(nb, t_new, NSA_KV_WIDTH)
    kvc = _nsa_decode_cmp(page_table, cnew_b, feat_major(cache_cmp), wts, t_new)
    pad_rows = lambda a, r: jnp.pad(a.reshape(nb, t_new, -1), ((0, 0), (0, r - t_new), (0, 0)))
    o_nsa = _nsa_decode_sel(page_table, q_rows(P['qn']), q_rows(P['qrot']), pad_rows(P['sel'], 16),
                            pad_rows(P['win'], 16), kvc, pad_rows(P['gates'], t_pad), feat_major(win_state),
                            feat_major(cache_sel), t_new, t_pad)
    o_nsa = o_nsa[:, :t_new].reshape(n, NSA_WIDTH)

    y = _finish(x2d, o_mla, o_nsa, P['szm'], P['szn'], wts, n)
    wst = win_state.reshape(nb, -1, NSA_KV_WIDTH)
    win_all = jnp.concatenate([wst, P['win'].reshape(nb, t_new, NSA_KV_WIDTH)], axis=1)[:, t_new:]
    return (y.reshape(nb, t_new, D_MODEL),
            P['mrow'].reshape(nb, t_new, MLA_ROW),
            P['cmp'].reshape((nb, t_new) + kvd),
            P['sel'].reshape((nb, t_new) + kvd),
            win_all.reshape((nb, win_all.shape[1]) + kvd))


def kernel(x_prompt, x_sample, cache_mla, cache_cmp_kv, cache_sel_kv, state_win_kv, page_table, w_in, q_norm_g, w_uq,
           kv_norm_g, w_uk, w_uv, cmp_pos_emb, cmp_w1, cmp_b1, cmp_w2, w_o, ln_g, ln_b):
    assert w_in.shape[0] == DEPTH
    wts = _prep_weights(w_in[0], q_norm_g[0], w_uq[0], kv_norm_g[0], w_uk[0], w_uv[0], cmp_pos_emb[0], cmp_w1[0],
                        cmp_b1[0], cmp_w2[0], w_o[0], ln_g[0], ln_b[0])
    yp, p_mla, p_cmp, p_sel, p_win = _prompt_layer(x_prompt, wts)
    drop_depth = lambda a: a.reshape(a.shape[1:])
    ys, s_mla, s_cmp, s_sel, s_win = _sample_layer(x_sample, drop_depth(cache_mla), drop_depth(cache_cmp_kv),
                                                   drop_depth(cache_sel_kv), drop_depth(state_win_kv), page_table, wts)
    add_depth = lambda a: a[None]
    return (yp, ys) + tuple(add_depth(a) for a in (p_mla, p_cmp, p_sel, p_win, s_mla, s_cmp, s_sel, s_win))
```

```python
import functools

import numpy as np
import jax
import jax.numpy as jnp
from jax import lax
from jax.experimental import pallas as pl
from jax.experimental.pallas import tpu as pltpu

F32 = jnp.float32
BF16 = jnp.bfloat16

D_MODEL = 1024
PAGE_SIZE = 128
MLA_HEADS = 8
MLA_V_DIM = 64
MLA_NOPE_DIM = 64
MLA_ROPE_DIM = 32
MLA_Q_LORA = 256
MLA_KV_LORA = 128
MLA_WIDTH = MLA_HEADS * MLA_V_DIM
MLA_ROW = MLA_KV_LORA + MLA_ROPE_DIM
NSA_HEADS = 8
NSA_KV_HEADS = 2
NSA_HEAD_DIM = 64
NSA_GROUP = NSA_HEADS // NSA_KV_HEADS
NSA_WIDTH = NSA_HEADS * NSA_HEAD_DIM
NSA_KV_WIDTH = 2 * NSA_KV_HEADS * NSA_HEAD_DIM
CMP_BLOCK = 32
CMP_STRIDE = 16
SEL_BLOCK = 64
SEL_TOP_N = 16
WINDOW = 512
ROPE_THETA = 10000.0
RMS_EPS = 1e-6
LN_EPS = 1e-5
NEG_INF = -1e30
MLA_SCALE = (MLA_NOPE_DIM + MLA_ROPE_DIM) ** -0.5
NSA_SCALE = NSA_HEAD_DIM ** -0.5
DEPTH = 1
ALPHA = (2 * DEPTH) ** 0.25

LANES = 128
LOG2E = 1.4426950408889634
MLA_KROW = 2 * LANES
MLA_ONES_LANE = MLA_ROW
NSA_KROW = 3 * LANES

SEG_CQ = (0, 256)
SEG_CKV = (256, 128)
SEG_ZMLA = (384, 512)
SEG_QN = (896, 512)
SEG_CMP = (1408, 256)
SEG_SEL = (1664, 256)
SEG_WIN = (1920, 256)
SEG_ZNSA = (2176, 512)
SEG_MISC = (2688, 128)
IN_WIDTH_P = 2816
MISC_GATE0 = MLA_ROPE_DIM

VMEM_LIMIT = 48 * 1024 * 1024

def _cparams(sem, flags=None):
    return pltpu.CompilerParams(dimension_semantics=sem, vmem_limit_bytes=VMEM_LIMIT, flags=flags)


def _sigmoid(x):
    return 1.0 / (1.0 + jnp.exp(-x))


def _silu(x):
    return x * _sigmoid(x)


def _rope_slab(x, c, sa, sb, half):
    return x * c + pltpu.roll(x, LANES - half, 1) * sa + pltpu.roll(x, half, 1) * sb


def _dot_nt(a, b):
    return lax.dot_general(a, b, (((1,), (1,)), ((), ())), preferred_element_type=F32)


def _dot(a, b):
    return jnp.dot(a, b, preferred_element_type=F32)


def _attend_segments(parts):
    is_bias = lambda mask: mask is not None and mask.dtype != jnp.bool_
    sm = [s if mask is None else (s + mask if is_bias(mask) else jnp.where(mask, s, NEG_INF))
          for s, mask, _ in parts]
    m = functools.reduce(jnp.maximum, [jnp.max(x, axis=-1, keepdims=True) for x in sm])
    l, acc = 0.0, 0.0
    for x, (_, mask, pv) in zip(sm, parts):
        p = jnp.exp2(x - m)
        if mask is not None and not is_bias(mask):
            p = jnp.where(mask, p, 0.0)
        l = l + jnp.sum(p, axis=-1, keepdims=True)
        acc = acc + pv(p.astype(BF16))
    ok = l > 0.0
    return jnp.where(ok, acc / jnp.where(ok, l, 1.0), 0.0)


def _flash_step(s, bias, v_ext, m, acc):
    sm = s if bias is None else s + bias
    m_new = jnp.maximum(m, jnp.max(sm, axis=-1, keepdims=True))
    p = jnp.exp2(sm - m_new)
    return m_new, jnp.exp2(m - m_new) * acc + _dot(p.astype(BF16), v_ext)


def _flash_result(acc, ones_lane):
    return acc[:, 0:LANES] * (1.0 / acc[:, ones_lane:ones_lane + 1])


def _tile_rows(x, reps):
    return jnp.concatenate([x] * reps, axis=0)


def _select_blocks(imp, q_pos, n_sel_blocks):
    rows, width = imp.shape
    j = lax.broadcasted_iota(jnp.int32, (rows, width), 1)
    cur = lax.shift_right_logical(q_pos, 6)
    avail = (j * SEL_BLOCK <= q_pos) & (j < n_sel_blocks)
    forced = (j == 0) | (j == cur) | (j == cur - 1)
    val = jnp.where(avail, jnp.where(forced, jnp.inf, imp), -jnp.inf)
    rank = jnp.zeros((rows, width), F32)
    for jp in range(n_sel_blocks):
        col = val[:, jp:jp + 1]
        ahead = (col > val) | ((col == val) & (j > jp))
        rank = rank + jnp.where(ahead, 1.0, 0.0)
    keep = (rank < float(min(SEL_TOP_N, n_sel_blocks))) & avail
    return jnp.where(keep, 1.0, 0.0)


def _select_blocks_t(imp_t, q_pos, n_sel_blocks):
    n_rows, width = imp_t.shape
    j = lax.broadcasted_iota(jnp.int32, (n_rows, width), 0)
    cur = lax.shift_right_logical(q_pos, 6)
    avail = (j * SEL_BLOCK <= q_pos) & (j < n_sel_blocks)
    forced = (j == 0) | (j == cur) | (j == cur - 1)
    val = jnp.where(avail, jnp.where(forced, jnp.inf, imp_t), -jnp.inf)
    n_pieces = -(-n_sel_blocks // 8)
    pieces = [val[8 * v:8 * v + 8] for v in range(n_pieces)]
    jrow = lax.broadcasted_iota(jnp.int32, (8, width), 0)
    ranks = [jnp.zeros((8, width), F32) for _ in range(n_pieces)]
    for jp in range(n_sel_blocks):
        cand = jnp.broadcast_to(val[jp:jp + 1], (8, width))
        for v in range(n_pieces):
            if 8 * v > jp:
                ahead = cand >= pieces[v]
            elif 8 * v + 7 < jp:
                ahead = cand > pieces[v]
            else:
                ahead = (cand > pieces[v]) | ((cand == pieces[v]) & (jrow > jp - 8 * v))
            ranks[v] = ranks[v] + jnp.where(ahead, 1.0, 0.0)
    if n_rows > 8 * n_pieces:
        ranks.append(jnp.zeros((n_rows - 8 * n_pieces, width), F32))
    rank = jnp.concatenate(ranks, axis=0)
    keep = (rank < float(min(SEL_TOP_N, n_sel_blocks))) & avail
    return jnp.where(keep, 1.0, 0.0)


def _proj_kernel(x_ref, w_ref, qg_ref, wuq_ref, wuk_ref, kvg_ref, tab_ref,
                 qmla_ref, mrow_ref, mrowb_ref, szm_ref, qn_ref, qrot_ref, cmp_ref, cmps_ref,
                 sel_ref, selb_ref, win_ref, winb_ref, gates_ref, szn_ref, *, feature_major):
    xb = x_ref[...].astype(BF16)

    def seg(s):
        return _dot(xb, w_ref[:, s[0]:s[0] + s[1]])

    cn, san, sbn = tab_ref[0], tab_ref[1], tab_ref[2]
    cm, sam, sbm = tab_ref[3], tab_ref[4], tab_ref[5]

    cq = seg(SEG_CQ)
    r = cq * lax.rsqrt(jnp.mean(cq * cq, axis=-1, keepdims=True) + RMS_EPS) * qg_ref[...]
    q = _dot(r.astype(BF16), wuq_ref[...])
    nope_w = MLA_HEADS * MLA_NOPE_DIM
    lane = lax.broadcasted_iota(jnp.int32, (q.shape[0], LANES), 1)
    for h in range(MLA_HEADS):
        ql = _dot(q[:, h * MLA_NOPE_DIM:(h + 1) * MLA_NOPE_DIM].astype(BF16), wuk_ref[h])
        qmla_ref[h, :, 0:LANES] = (ql * (MLA_SCALE * LOG2E)).astype(BF16)
    heads_per_slab = LANES // MLA_ROPE_DIM
    for jj in range(MLA_HEADS // heads_per_slab):
        qr = _rope_slab(q[:, nope_w + jj * LANES: nope_w + (jj + 1) * LANES], cm, sam, sbm, MLA_ROPE_DIM // 2)
        qr = qr * (MLA_SCALE * LOG2E)
        for hh in range(heads_per_slab):
            front = qr if hh == 0 else pltpu.roll(qr, LANES - hh * MLA_ROPE_DIM, 1)
            qmla_ref[jj * heads_per_slab + hh, :, LANES:2 * LANES] = (
                jnp.where(lane < MLA_ROPE_DIM, front, 0.0).astype(BF16))

    ckv = seg(SEG_CKV)
    lat = ckv * lax.rsqrt(jnp.mean(ckv * ckv, axis=-1, keepdims=True) + RMS_EPS) * kvg_ref[...]
    misc = seg(SEG_MISC)
    kr = _rope_slab(misc, cm, sam, sbm, MLA_ROPE_DIM // 2)
    if feature_major:
        mrow_ref[0:MLA_KV_LORA, :] = lat.T
        mrow_ref[MLA_KV_LORA:MLA_ROW, :] = kr.T[0:MLA_ROPE_DIM]
    else:
        mrow_ref[:, 0:MLA_KV_LORA] = lat
        mrow_ref[:, MLA_KV_LORA:MLA_ROW] = kr[:, 0:MLA_ROPE_DIM]
    mrowb_ref[:, 0:LANES] = lat.astype(BF16)
    ones_col = jnp.where(lane == MLA_ONES_LANE - LANES, 1.0, 0.0)
    mrowb_ref[:, LANES:2 * LANES] = jnp.where(lane < MLA_ROPE_DIM, kr, ones_col).astype(BF16)
    gates_ref[...] = _sigmoid(misc)

    szm_ref[...] = _silu(seg(SEG_ZMLA))
    szn_ref[...] = _silu(seg(SEG_ZNSA))

    qn = seg(SEG_QN)
    for jj in range(NSA_WIDTH // LANES):
        raw = qn[:, jj * LANES:(jj + 1) * LANES] * (NSA_SCALE * LOG2E)
        rot = _rope_slab(qn[:, jj * LANES:(jj + 1) * LANES], cn, san, sbn, NSA_HEAD_DIM // 2) * (NSA_SCALE * LOG2E)
        for src, dst in ((raw, qn_ref), (rot, qrot_ref)):
            swapped = pltpu.roll(src, NSA_HEAD_DIM, 1)
            for half in range(2):
                hd = 2 * jj + half
                g = hd // NSA_GROUP
                keep = (lane >= g * NSA_HEAD_DIM) & (lane < (g + 1) * NSA_HEAD_DIM)
                dst[hd] = jnp.where(keep, src if half == g else swapped, 0.0).astype(BF16)

    def put_rows(dst, first, second):
        if feature_major:
            dst[0:LANES, :] = first.T
            dst[LANES:2 * LANES, :] = second.T
        else:
            dst[:, 0:LANES] = first
            dst[:, LANES:2 * LANES] = second

    cmpv = seg(SEG_CMP)
    put_rows(cmp_ref, cmpv[:, 0:LANES], cmpv[:, LANES:2 * LANES])
    cmps_ref[0] = cmpv[:, 0:LANES]
    cmps_ref[1] = cmpv[:, LANES:2 * LANES]

    for src, dst, dstb in ((SEG_SEL, sel_ref, selb_ref), (SEG_WIN, win_ref, winb_ref)):
        kv = seg(src)
        k = _rope_slab(kv[:, 0:LANES], cn, san, sbn, NSA_HEAD_DIM // 2)
        v = kv[:, LANES:2 * LANES]
        put_rows(dst, k, v)
        dstb[:, 0:LANES] = k.astype(BF16)
        dstb[:, LANES:2 * LANES] = v.astype(BF16)
        dstb[:, 2 * LANES:3 * LANES] = jnp.where(lane == 0, 1.0, 0.0).astype(BF16)


def _rope_tables(pos):
    def tab(d):
        inv = 1.0 / (ROPE_THETA ** (jnp.arange(0, d, 2, dtype=F32) / d))
        ang = pos.astype(F32)[:, None] * inv[None, :]
        cos, sin = jnp.cos(ang), jnp.sin(ang)
        zero = jnp.zeros_like(sin)
        reps = LANES // d
        return (jnp.tile(jnp.concatenate([cos, cos], -1), (1, reps)),
                jnp.tile(jnp.concatenate([-sin, zero], -1), (1, reps)),
                jnp.tile(jnp.concatenate([zero, sin], -1), (1, reps)))
    return jnp.stack(tab(NSA_HEAD_DIM) + tab(MLA_ROPE_DIM))


def _project(x2d, tabs, wts, tm, feature_major=False):
    n = x2d.shape[0]
    p_rows = tabs.shape[1]
    assert n % tm == 0 and p_rows % tm == 0
    nt = p_rows // tm
    row = lambda w: pl.BlockSpec((tm, w), lambda i: (i, 0))
    full = lambda a: pl.BlockSpec(a.shape, lambda i: (0,) * a.ndim)
    if feature_major:
        out_rows = lambda w: ((n // p_rows, w, p_rows), F32, pl.BlockSpec((None, w, tm), lambda i: (i // nt, 0, i % nt)))
    else:
        out_rows = lambda w: ((n, w), F32, row(w))
    out_shapes = dict(
        qmla=((MLA_HEADS, n, MLA_KROW), BF16, pl.BlockSpec((MLA_HEADS, tm, MLA_KROW), lambda i: (0, i, 0))),
        mrow=out_rows(MLA_ROW),
        mrowb=((n, MLA_KROW), BF16, row(MLA_KROW)),
        szm=((n, MLA_WIDTH), F32, row(MLA_WIDTH)),
        qn=((NSA_HEADS, n, LANES), BF16, pl.BlockSpec((NSA_HEADS, tm, LANES), lambda i: (0, i, 0))),
        qrot=((NSA_HEADS, n, LANES), BF16, pl.BlockSpec((NSA_HEADS, tm, LANES), lambda i: (0, i, 0))),
        cmp=out_rows(NSA_KV_WIDTH),
        cmps=((2, n, LANES), F32, pl.BlockSpec((2, tm, LANES), lambda i: (0, i, 0))),
        sel=out_rows(NSA_KV_WIDTH),
        selb=((n, NSA_KROW), BF16, row(NSA_KROW)),
        win=out_rows(NSA_KV_WIDTH),
        winb=((n, NSA_KROW), BF16, row(NSA_KROW)),
        gates=((n, LANES), F32, row(LANES)),
        szn=((n, NSA_WIDTH), F32, row(NSA_WIDTH)),
    )
    names = list(out_shapes)
    outs = pl.pallas_call(
        functools.partial(_proj_kernel, feature_major=feature_major),
        grid=(n // tm,),
        in_specs=[row(D_MODEL), full(wts['w_in']), full(wts['q_norm_g']), full(wts['w_uq']), full(wts['w_uk']),
                  full(wts['kv_norm_g']), pl.BlockSpec((6, tm, LANES), lambda i: (0, i % nt, 0))],
        out_specs=[out_shapes[k][2] for k in names],
        out_shape=[jax.ShapeDtypeStruct(out_shapes[k][0], out_shapes[k][1]) for k in names],
        compiler_params=_cparams(("parallel",)),
        name="projection",
    )(x2d, wts['w_in'], wts['q_norm_g'], wts['w_uq'], wts['w_uk'], wts['kv_norm_g'], tabs)
    return dict(zip(names, outs))


def _uv_project(o, wuv_ref, rows):
    return jnp.concatenate(
        [_dot(o[h * rows:(h + 1) * rows].astype(BF16), wuv_ref[h]) for h in range(MLA_HEADS)], axis=-1)


def _flash_pipeline(n_chunks, prepare, scores, consume, s_a, s_b):
    scores(0, s_a)

    def body(i, _):
        c0 = 2 * i
        aux = prepare(c0)
        scores(c0 + 1, s_b)
        consume(c0, s_a, aux)

        @pl.when(c0 + 1 < n_chunks)
        def _():
            aux = prepare(c0 + 1)
            scores(c0 + 2, s_a)
            consume(c0 + 1, s_b, aux)
        return 0

    lax.fori_loop(0, (n_chunks + 1) // 2, body, 0)


def _mla_prompt_kernel(q_ref, k_ref, wuv_ref, o_ref, s_a, s_b, m_ref, acc_ref, *, tq, tk):
    qi = pl.program_id(1)
    rows = MLA_HEADS * tq
    q = q_ref[...].reshape(rows, MLA_KROW)
    n_chunks = (qi * tq) // tk + 1
    col = lax.broadcasted_iota(jnp.int32, (tq, tk), 1)
    q_pos = qi * tq + lax.broadcasted_iota(jnp.int32, (tq, tk), 0)
    m_ref[...] = jnp.full(m_ref.shape, NEG_INF, F32)
    acc_ref[...] = jnp.zeros(acc_ref.shape, F32)

    def chunk(c):
        return k_ref[pl.ds(pl.multiple_of(c * tk, tk), tk), :]

    def scores(c, dst):
        dst[...] = _dot_nt(q, chunk(jnp.minimum(c, n_chunks - 1)))

    def prepare(c):
        return _tile_rows(jnp.where(c * tk + col <= q_pos, 0.0, NEG_INF), MLA_HEADS)

    def consume(c, src, bias):
        m_ref[...], acc_ref[...] = _flash_step(src[...], bias, chunk(c), m_ref[...], acc_ref[...])

    _flash_pipeline(n_chunks, prepare, scores, consume, s_a, s_b)
    o_ref[...] = _uv_project(_flash_result(acc_ref[...], MLA_ONES_LANE), wuv_ref, tq)


def _mla_prompt(qmla, mrowb, wuv, batch, seq):
    tq = min(256, seq)
    tk = min(512, seq)
    nq = seq // tq
    assert tk % tq == 0 and seq % tk == 0
    return pl.pallas_call(
        functools.partial(_mla_prompt_kernel, tq=tq, tk=tk),
        grid=(batch, nq),
        in_specs=[pl.BlockSpec((MLA_HEADS, tq, MLA_KROW), lambda b, i: (0, b * nq + i, 0)),
                  pl.BlockSpec((None, seq, MLA_KROW), lambda b, i: (b, 0, 0)),
                  pl.BlockSpec(wuv.shape, lambda b, i: (0, 0, 0))],
        out_specs=pl.BlockSpec((tq, MLA_WIDTH), lambda b, i: (b * nq + i, 0)),
        out_shape=jax.ShapeDtypeStruct((batch * seq, MLA_WIDTH), F32),
        scratch_shapes=[pltpu.VMEM((MLA_HEADS * tq, tk), F32), pltpu.VMEM((MLA_HEADS * tq, tk), F32),
                        pltpu.VMEM((MLA_HEADS * tq, 1), F32), pltpu.VMEM((MLA_HEADS * tq, MLA_KROW), F32)],
        compiler_params=_cparams(("parallel", "parallel")),
        name="mla_prompt",
    )(qmla, mrowb.reshape(batch, seq, MLA_KROW), wuv)


CMP_PAIRS = CMP_STRIDE // 2
CHUNK_PITCH = 24


def _compress_slab(load_pair, wc1_ref, pe_ref, b1_ref, w2_ref, s, n_chunk):
    proj = jnp.zeros((n_chunk, 2 * LANES), F32)
    pe = jnp.zeros((8, 2 * LANES), F32)
    for j in range(CMP_PAIRS):
        w = wc1_ref[s, j]
        proj = proj + _dot(load_pair(j).astype(BF16), w)
        pe = pe + _dot(pe_ref[s, j].astype(BF16), w)
    bias = b1_ref[s] + pe[0:1, 0:LANES] + pe[1:2, LANES:2 * LANES]
    hid = proj[:, 0:LANES] + pltpu.roll(proj[:, LANES:2 * LANES], n_chunk - 1, 0) + bias
    return _dot(_silu(hid).astype(BF16), w2_ref[s])


def _compress_prompt_kernel(x_ref, wc1_ref, pe_ref, b1_ref, w2_ref, o_ref, *, n_chunk):
    for s in range(2):
        row = lambda p, s=s: x_ref[s, pl.ds(p, n_chunk, stride=CMP_STRIDE), :]
        load = lambda j, row=row: jnp.concatenate([row(2 * j), row(2 * j + 1)], axis=-1)
        o_ref[s] = _compress_slab(load, wc1_ref, pe_ref, b1_ref, w2_ref, s, n_chunk).astype(BF16)


def _compress_prompt(cmps, wts, batch, seq):
    n_chunk = seq // CMP_STRIDE
    full = lambda a: pl.BlockSpec(a.shape, lambda b: (0,) * a.ndim)
    return pl.pallas_call(
        functools.partial(_compress_prompt_kernel, n_chunk=n_chunk),
        grid=(batch,),
        in_specs=[pl.BlockSpec((2, seq, LANES), lambda b: (0, b, 0)),
                  full(wts['wc1']), full(wts['pe']), full(wts['b1']), full(wts['w2'])],
        out_specs=pl.BlockSpec((None, 2, n_chunk, LANES), lambda b: (b, 0, 0, 0)),
        out_shape=jax.ShapeDtypeStruct((batch, 2, n_chunk, LANES), BF16),
        compiler_params=_cparams(("parallel",)),
        name="compress_prompt",
    )(cmps, wts['wc1'], wts['pe'], wts['b1'], wts['w2'])


def _cmp_branch(q, kc, vc, q_pos_rows, n_cmp_blocks, impmap_ref, rows_t, imp_transposed=False):
    s = _dot_nt(q, kc)
    n = lax.broadcasted_iota(jnp.int32, (rows_t, s.shape[1]), 1)
    visible = (n * CMP_STRIDE + (CMP_BLOCK - 1) <= q_pos_rows[0:rows_t]) & (n < n_cmp_blocks)
    sm = s + _tile_rows(jnp.where(visible, 0.0, NEG_INF), NSA_HEADS)
    e = jnp.exp2(sm - jnp.max(sm, axis=-1, keepdims=True))
    any_visible = (q_pos_rows >= CMP_BLOCK - 1) & (n_cmp_blocks > 0)
    p = e * jnp.where(any_visible, 1.0 / jnp.sum(e, axis=-1, keepdims=True), 0.0)
    o_cmp = _dot(p.astype(BF16), vc)
    imps = []
    for g in range(NSA_KV_HEADS):
        base = g * NSA_GROUP * rows_t
        psum = p[base:base + rows_t]
        for hh in range(1, NSA_GROUP):
            psum = psum + p[base + hh * rows_t: base + (hh + 1) * rows_t]
        hi = psum.astype(BF16)
        lo = (psum - hi.astype(F32)).astype(BF16)
        if imp_transposed:
            imps.append(_dot_nt(impmap_ref[...], hi) + _dot_nt(impmap_ref[...], lo))
        else:
            imps.append(_dot(hi, impmap_ref[...]) + _dot(lo, impmap_ref[...]))
    return o_cmp, imps


def _mix_heads(gates, o_cmp, o_sel, o_win, rows_t):
    outs = []
    for hd in range(NSA_HEADS):
        g = hd // NSA_GROUP
        rs = slice(hd * rows_t, (hd + 1) * rows_t)
        ls = slice(g * NSA_HEAD_DIM, (g + 1) * NSA_HEAD_DIM)
        gcol = lambda br: gates[:, MISC_GATE0 + br * NSA_HEADS + hd: MISC_GATE0 + br * NSA_HEADS + hd + 1]
        outs.append(gcol(0) * o_cmp[rs, ls] + gcol(1) * o_sel[rs, ls] + gcol(2) * o_win[rs, ls])
    return jnp.concatenate(outs, axis=-1)


def _group_rows(per_group, reps):
    return jnp.concatenate([per_group[g] for g in range(NSA_KV_HEADS) for _ in range(reps)], axis=0)


def _nsa_prompt_kernel(qn_ref, qrot_ref, kvc_ref, sel_ref, win_ref, gates_ref, impmap_ref, o_ref,
                       s_a, s_b, m_ref, acc_ref, *, tq, tk, n_cmp_blocks, n_sel_blocks, wlen):
    qi = pl.program_id(1)
    rows = NSA_HEADS * tq
    q_pos_t = qi * tq + lax.broadcasted_iota(jnp.int32, (tq, 1), 0)
    q_pos_r = qi * tq + (lax.broadcasted_iota(jnp.int32, (rows, 1), 0) & (tq - 1))
    qn = qn_ref[...].reshape(rows, LANES)
    qrot = qrot_ref[...].reshape(rows, LANES)
    init = (jnp.full((rows, 1), NEG_INF, F32), jnp.zeros((rows, 2 * LANES), F32))

    w_start = pl.multiple_of(jnp.maximum(qi * tq + tq - wlen, 0), tq)
    d = q_pos_t - (w_start + lax.broadcasted_iota(jnp.int32, (tq, wlen), 1))
    bias = _tile_rows(jnp.where((d >= 0) & (d < WINDOW), 0.0, NEG_INF), NSA_HEADS)
    _, acc = _flash_step(_dot_nt(qrot, win_ref[pl.ds(w_start, wlen), 0:LANES]), bias,
                         win_ref[pl.ds(w_start, wlen), LANES:3 * LANES], *init)
    o_win = _flash_result(acc, LANES)

    o_cmp, imps_t = _cmp_branch(qn, kvc_ref[0], kvc_ref[1], q_pos_r, n_cmp_blocks, impmap_ref, tq,
                                imp_transposed=True)
    q_pos_lane = qi * tq + lax.broadcasted_iota(jnp.int32, (1, tq), 1)
    selms = [_select_blocks_t(imp_t, q_pos_lane, n_sel_blocks).T.astype(BF16) for imp_t in imps_t]

    blk_per_chunk = tk // SEL_BLOCK
    n_chunks = (qi * tq + tq + tk - 1) // tk
    e_row = lax.broadcasted_iota(jnp.int32, (LANES, tk), 0)
    e_col = lax.shift_right_logical(lax.broadcasted_iota(jnp.int32, (LANES, tk), 1), 6)
    col_t = lax.broadcasted_iota(jnp.int32, (tq, tk), 1)

    def sel_scores(c, dst):
        start = pl.multiple_of(jnp.minimum(c, n_chunks - 1) * tk, tk)
        dst[...] = _dot_nt(qrot, sel_ref[pl.ds(start, tk), 0:LANES])

    def sel_prepare(c):
        expand = jnp.where(e_row == c * blk_per_chunk + e_col, 1.0, 0.0).astype(BF16)
        causal = (c * tk + col_t) <= q_pos_t
        return _group_rows([jnp.where((_dot(sm, expand) > 0.5) & causal, 0.0, NEG_INF) for sm in selms], NSA_GROUP)

    def sel_consume(c, src, bias):
        v_ext = sel_ref[pl.ds(pl.multiple_of(c * tk, tk), tk), LANES:3 * LANES]
        m_ref[...], acc_ref[...] = _flash_step(src[...], bias, v_ext, m_ref[...], acc_ref[...])

    m_ref[...], acc_ref[...] = init
    _flash_pipeline(n_chunks, sel_prepare, sel_scores, sel_consume, s_a, s_b)
    o_sel = _flash_result(acc_ref[...], LANES)

    o_ref[...] = _mix_heads(gates_ref[...], o_cmp, o_sel, o_win, tq)


def _importance_map(nc_rows, nc, ns, width):
    i = np.arange(nc_rows)[:, None]
    j = np.arange(width)[None, :]
    lo = np.maximum(i * CMP_STRIDE, j * SEL_BLOCK)
    hi = np.minimum(i * CMP_STRIDE + CMP_BLOCK, (j + 1) * SEL_BLOCK)
    m = np.maximum(hi - lo, 0).astype(np.float32) / CMP_BLOCK
    m = m * (i < nc) * (j < ns)
    return jnp.asarray(m, dtype=BF16)


def _nsa_prompt(P, kvc, batch, seq):
    tq = min(256, seq)
    tk = min(512, seq)
    nq = seq // tq
    n_chunk = seq // CMP_STRIDE
    nc = n_chunk - CMP_BLOCK // CMP_STRIDE + 1
    ns = seq // SEL_BLOCK
    assert ns <= LANES and seq % tk == 0 and seq % CMP_STRIDE == 0 and seq % SEL_BLOCK == 0
    wlen = min(WINDOW + tq, seq)
    impmap = _importance_map(n_chunk, nc, ns, LANES).T
    blk = lambda w: pl.BlockSpec((tq, w), lambda b, i: (b * nq + i, 0))
    qblk = pl.BlockSpec((NSA_HEADS, tq, LANES), lambda b, i: (0, b * nq + i, 0))
    per_b = lambda a: pl.BlockSpec((None,) + a.shape[1:], lambda b, i: (b,) + (0,) * (a.ndim - 1))
    selb = P['selb'].reshape(batch, seq, NSA_KROW)
    winb = P['winb'].reshape(batch, seq, NSA_KROW)
    return pl.pallas_call(
        functools.partial(_nsa_prompt_kernel, tq=tq, tk=tk, n_cmp_blocks=nc, n_sel_blocks=ns, wlen=wlen),
        grid=(batch, nq),
        in_specs=[qblk, qblk, per_b(kvc), per_b(selb), per_b(winb), blk(LANES),
                  pl.BlockSpec(impmap.shape, lambda b, i: (0, 0))],
        out_specs=blk(NSA_WIDTH),
        out_shape=jax.ShapeDtypeStruct((batch * seq, NSA_WIDTH), F32),
        scratch_shapes=[pltpu.VMEM((NSA_HEADS * tq, tk), F32), pltpu.VMEM((NSA_HEADS * tq, tk), F32),
                        pltpu.VMEM((NSA_HEADS * tq, 1), F32), pltpu.VMEM((NSA_HEADS * tq, 2 * LANES), F32)],
        compiler_params=_cparams(("parallel", "parallel")),
        name="nsa_prompt",
    )(P['qn'], P['qrot'], kvc, selb, winb, P['gates'], impmap)


def _finish_kernel(x_ref, omla_ref, onsa_ref, szm_ref, szn_ref, wo_ref, g_ref, b_ref, y_ref):
    mixed = jnp.concatenate([omla_ref[...] * szm_ref[...], onsa_ref[...] * szn_ref[...]], axis=-1)
    h = ALPHA * x_ref[...] + _dot(mixed.astype(BF16), wo_ref[...])
    mu = jnp.mean(h, axis=-1, keepdims=True)
    d = h - mu
    var = jnp.mean(d * d, axis=-1, keepdims=True)
    y_ref[...] = d * lax.rsqrt(var + LN_EPS) * g_ref[...] + b_ref[...]


def _finish(x2d, omla, onsa, szm, szn, wts, tm):
    n = x2d.shape[0]
    row = lambda w: pl.BlockSpec((tm, w), lambda i: (i, 0))
    full = lambda a: pl.BlockSpec(a.shape, lambda i: (0,) * a.ndim)
    return pl.pallas_call(
        _finish_kernel,
        grid=(n // tm,),
        in_specs=[row(D_MODEL), row(MLA_WIDTH), row(NSA_WIDTH), row(MLA_WIDTH), row(NSA_WIDTH),
                  full(wts['w_o']), full(wts['ln_g']), full(wts['ln_b'])],
        out_specs=row(D_MODEL),
        out_shape=jax.ShapeDtypeStruct((n, D_MODEL), F32),
        compiler_params=_cparams(("parallel",)),
        name="finish",
    )(x2d, omla, onsa, szm, szn, wts['w_o'], wts['ln_g'], wts['ln_b'])


def _page_copies(pt_ref, b, n_pages, make_copy):
    assert n_pages % 2 == 0

    def start(i, _):
        make_copy(pt_ref[b, 2 * i], 2 * i).start(priority=0)
        make_copy(pt_ref[b, 2 * i + 1], 2 * i + 1).start(priority=1)
        return 0
    lax.fori_loop(0, n_pages // 2, start, 0, unroll=4)


def _page_wait_all(pool_ref, slot_buf, sem):
    n_pages = slot_buf.shape[0]
    pltpu.make_async_copy(pool_ref.at[pl.ds(0, n_pages)], slot_buf, sem).wait()


def _mla_decode_kernel(pt_ref, q_ref, knew_ref, qn_ref, kvc_ref, impmap_ref, pool_ref, o_ref, ocmp_ref, sel_ref, kbuf,
                       sem, *, n_pages, t_new, t_pad, tk, n_cmp_blocks, n_sel_blocks):
    b = pl.program_id(0)
    nb = pl.num_programs(0)
    slot = b % 2
    past = n_pages * PAGE_SIZE

    def copy(sl):
        return lambda page, p: pltpu.make_async_copy(pool_ref.at[page], kbuf.at[sl, p], sem.at[sl])

    @pl.when(b == 0)
    def _():
        _page_copies(pt_ref, 0, n_pages, copy(0))

    @pl.when(b + 1 < nb)
    def _():
        _page_copies(pt_ref, b + 1, n_pages, copy(1 - slot))

    nsa_rows = NSA_HEADS * t_pad
    q_pos_r = past + (lax.broadcasted_iota(jnp.int32, (nsa_rows, 1), 0) & (t_pad - 1))
    q_pos_t = past + lax.broadcasted_iota(jnp.int32, (t_pad, 1), 0)
    o_cmp, imps = _cmp_branch(qn_ref[...], kvc_ref[0], kvc_ref[1], q_pos_r, n_cmp_blocks, impmap_ref, t_pad)
    ocmp_ref[...] = o_cmp
    for g in range(NSA_KV_HEADS):
        sel_ref[g] = _select_blocks(imps[g], q_pos_t, n_sel_blocks)

    _page_wait_all(pool_ref, kbuf.at[slot], sem.at[slot])

    q = q_ref[...]
    rows = q.shape[0]
    pages_per_chunk = tk // PAGE_SIZE
    parts = []
    for c in range(n_pages // pages_per_chunk):
        kt = jnp.concatenate([kbuf[slot, c * pages_per_chunk + i].astype(BF16) for i in range(pages_per_chunk)],
                             axis=1)
        parts.append((_dot(q, kt), None, lambda p, kt=kt: _dot_nt(p, kt[0:MLA_KV_LORA])))
    kn = knew_ref[...]
    t_q = lax.broadcasted_iota(jnp.int32, (rows, kn.shape[0]), 0) & (t_new - 1)
    t_k = lax.broadcasted_iota(jnp.int32, (rows, kn.shape[0]), 1)
    parts.append((_dot_nt(q, kn), t_k <= t_q, lambda p: _dot(p, kn[:, 0:MLA_KV_LORA])))
    o_ref[...] = _attend_segments(parts)


def _mla_decode(page_table, q_b, knew_b, qn_b, kvc, pool, t_pad):
    nb, n_pages = page_table.shape
    rows = q_b.shape[1]
    t_new = rows // MLA_HEADS
    tk = 1024
    past = n_pages * PAGE_SIZE
    assert past % tk == 0 and (t_new & (t_new - 1)) == 0
    _, nc, ns = _decode_block_counts(past, t_new)
    sel_w = -(-ns // LANES) * LANES
    impmap = _importance_map(nc, nc, ns, sel_w)
    nsa_rows = NSA_HEADS * t_pad
    per_b = lambda a: pl.BlockSpec((None,) + a.shape[1:], lambda b, pt: (b,) + (0,) * (a.ndim - 1))
    grid_spec = pltpu.PrefetchScalarGridSpec(
        num_scalar_prefetch=1,
        grid=(nb,),
        in_specs=[per_b(q_b), per_b(knew_b), per_b(qn_b), per_b(kvc),
                  pl.BlockSpec(impmap.shape, lambda b, pt: (0, 0)), pl.BlockSpec(memory_space=pl.ANY)],
        out_specs=[pl.BlockSpec((None, rows, MLA_KV_LORA), lambda b, pt: (b, 0, 0)),
                   pl.BlockSpec((None, nsa_rows, LANES), lambda b, pt: (b, 0, 0)),
                   pl.BlockSpec((None, NSA_KV_HEADS, t_pad, sel_w), lambda b, pt: (b, 0, 0, 0))],
        scratch_shapes=[pltpu.VMEM((2, n_pages, MLA_ROW, PAGE_SIZE), F32), pltpu.SemaphoreType.DMA((2,))],
    )
    return pl.pallas_call(
        functools.partial(_mla_decode_kernel, n_pages=n_pages, t_new=t_new, t_pad=t_pad, tk=tk, n_cmp_blocks=nc,
                          n_sel_blocks=ns),
        grid_spec=grid_spec,
        out_shape=[jax.ShapeDtypeStruct((nb, rows, MLA_KV_LORA), F32),
                   jax.ShapeDtypeStruct((nb, nsa_rows, LANES), F32),
                   jax.ShapeDtypeStruct((nb, NSA_KV_HEADS, t_pad, sel_w), F32)],
        compiler_params=_cparams(("arbitrary",)),
        name="mla_decode",
    )(page_table, q_b, knew_b, qn_b, kvc, impmap, pool)


def _uv_kernel(o_ref, wuv_ref, y_ref):
    rows = o_ref.shape[1]
    y_ref[...] = _uv_project(o_ref[...].reshape(MLA_HEADS * rows, MLA_KV_LORA), wuv_ref, rows)


def _uv(o_heads, wuv):
    rows = o_heads.shape[1]
    return pl.pallas_call(
        _uv_kernel,
        out_shape=jax.ShapeDtypeStruct((rows, MLA_WIDTH), F32),
        name="mla_value_up",
    )(o_heads, wuv)


def _nsa_dec_cmp_kernel(pt_ref, cnew_ref, pool_ref, wc1_ref, pe_ref, b1_ref, w2_ref, kvc_ref, pbuf, xrows_k, xrows_v,
                        sem, *, n_pages, t_new, n_chunk, n_cmp_blocks):
    b = pl.program_id(0)
    nb = pl.num_programs(0)
    slot = b % 2
    past = n_pages * PAGE_SIZE

    def copy(sl):
        return lambda page, p: pltpu.make_async_copy(pool_ref.at[page], pbuf.at[sl, p], sem.at[sl])

    @pl.when(b == 0)
    def _():
        _page_copies(pt_ref, 0, n_pages, copy(0))

    @pl.when(b + 1 < nb)
    def _():
        _page_copies(pt_ref, b + 1, n_pages, copy(1 - slot))

    _page_wait_all(pool_ref, pbuf.at[slot], sem.at[slot])

    chunks_per_page = PAGE_SIZE // CMP_STRIDE
    xrows = (xrows_k, xrows_v)

    def relayout(p, s):
        r0 = p * (chunks_per_page * CHUNK_PITCH)
        if not isinstance(p, int):
            r0 = pl.multiple_of(r0, 8)
        t = pbuf[slot, p, s * LANES:(s + 1) * LANES, :].astype(BF16).T.astype(F32)
        for c in range(chunks_per_page):
            xrows[s][pl.ds(r0 + c * CHUNK_PITCH, CMP_STRIDE), :] = t[c * CMP_STRIDE:(c + 1) * CMP_STRIDE]

    cnew = cnew_ref[...]
    base = n_pages * chunks_per_page * CHUNK_PITCH
    tail = xrows_k.shape[0] - base
    for s in range(2):
        xrows[s][pl.ds(base, tail), :] = jnp.zeros((tail, LANES), F32)
        xrows[s][pl.ds(base, t_new), :] = cnew[:, s * LANES:(s + 1) * LANES]

    def compress(s):
        row = lambda p: xrows[s][pl.ds(p, n_chunk, stride=CHUNK_PITCH), :]
        load = lambda j: jnp.concatenate([row(2 * j), row(2 * j + 1)], axis=-1)
        return _compress_slab(load, wc1_ref, pe_ref, b1_ref, w2_ref, s, n_chunk).astype(BF16)

    def relayout_keys(p, _):
        relayout(p, 0)
        return 0

    lax.fori_loop(0, n_pages, relayout_keys, 0, unroll=16)
    for p in range(n_pages):
        relayout(p, 1)
    for s in range(2):
        kvc_ref[s] = compress(s)[0:n_cmp_blocks]


def _decode_block_counts(past, t_new):
    n_chunk_true = -(-(past + t_new) // CMP_STRIDE)
    nc = n_chunk_true - CMP_BLOCK // CMP_STRIDE + 1
    ns = past // SEL_BLOCK + (-(-t_new // SEL_BLOCK))
    return -(-n_chunk_true // 8) * 8, nc, ns


def _nsa_decode_cmp(page_table, cnew_b, pool, wts, t_new):
    nb, n_pages = page_table.shape
    n_chunk, nc, _ = _decode_block_counts(n_pages * PAGE_SIZE, t_new)
    assert nc % LANES == 0, "compressed-block count must be lane aligned"
    full = lambda a: pl.BlockSpec(a.shape, lambda b, pt: (0,) * a.ndim)
    per_b = lambda a: pl.BlockSpec((None,) + a.shape[1:], lambda b, pt: (b,) + (0,) * (a.ndim - 1))
    grid_spec = pltpu.PrefetchScalarGridSpec(
        num_scalar_prefetch=1,
        grid=(nb,),
        in_specs=[per_b(cnew_b), pl.BlockSpec(memory_space=pl.ANY),
                  full(wts['wc1']), full(wts['pe']), full(wts['b1']), full(wts['w2'])],
        out_specs=pl.BlockSpec((None, 2, nc, LANES), lambda b, pt: (b, 0, 0, 0)),
        scratch_shapes=[pltpu.VMEM((2, n_pages, NSA_KV_WIDTH, PAGE_SIZE), F32),
                        pltpu.VMEM((n_chunk * CHUNK_PITCH, LANES), F32),
                        pltpu.VMEM((n_chunk * CHUNK_PITCH, LANES), F32),
                        pltpu.SemaphoreType.DMA((2,))],
    )
    return pl.pallas_call(
        functools.partial(_nsa_dec_cmp_kernel, n_pages=n_pages, t_new=t_new, n_chunk=n_chunk, n_cmp_blocks=nc),
        grid_spec=grid_spec,
        out_shape=jax.ShapeDtypeStruct((nb, 2, nc, LANES), BF16),
        compiler_params=_cparams(("arbitrary",)),
        name="nsa_decode_compress",
    )(page_table, cnew_b, pool, wts['wc1'], wts['pe'], wts['b1'], wts['w2'])


def _nsa_dec_sel_kernel(pt_ref, qr_ref, snew_ref, wnew_ref, selg_ref, ocmp_ref, gates_ref, wst_ref, pool_ref,
                        o_ref, kbuf, sem, *, n_pages, t_new, t_pad, tk):
    b = pl.program_id(0)
    nb = pl.num_programs(0)
    slot = b % 2
    past = n_pages * PAGE_SIZE
    rows = NSA_HEADS * t_pad

    def copy(sl):
        return lambda page, p: pltpu.make_async_copy(pool_ref.at[page], kbuf.at[sl, p], sem.at[sl])

    @pl.when(b == 0)
    def _():
        _page_copies(pt_ref, 0, n_pages, copy(0))

    @pl.when(b + 1 < nb)
    def _():
        _page_copies(pt_ref, b + 1, n_pages, copy(1 - slot))

    _page_wait_all(pool_ref, kbuf.at[slot], sem.at[slot])

    blk_per_chunk = tk // SEL_BLOCK
    lane_lo = lax.broadcasted_iota(jnp.int32, (t_pad, LANES), 1) < SEL_BLOCK
    n_new = snew_ref.shape[0]
    t_q = lax.broadcasted_iota(jnp.int32, (rows, n_new), 0) & (t_pad - 1)
    t_k = lax.broadcasted_iota(jnp.int32, (rows, n_new), 1)
    new_causal = (t_k <= t_q) & (t_k < t_new)
    w_buf = wst_ref.shape[1]
    t_qw = lax.broadcasted_iota(jnp.int32, (rows, w_buf), 0) & (t_pad - 1)
    i_w = lax.broadcasted_iota(jnp.int32, (rows, w_buf), 1)
    win_mask = (t_qw + w_buf - i_w) < WINDOW
    snew = snew_ref[...]
    wnew = wnew_ref[...]
    qr = qr_ref[...]
    sel_groups = [selg_ref[g] for g in range(NSA_KV_HEADS)]
    o_cmp = ocmp_ref[...]

    parts = []
    pages_per_chunk = tk // PAGE_SIZE
    for c in range(past // tk):
        chunk_pages = range(c * pages_per_chunk, (c + 1) * pages_per_chunk)
        kt = jnp.concatenate([kbuf[slot, p, 0:LANES, :].astype(BF16) for p in chunk_pages], axis=1)
        vt = jnp.concatenate([kbuf[slot, p, LANES:2 * LANES, :].astype(BF16) for p in chunk_pages], axis=1)
        biases = []
        for sel_g in sel_groups:
            pieces = []
            for i in range(tk // LANES):
                jb = c * blk_per_chunk + 2 * i
                pieces.append(jnp.where(lane_lo, sel_g[:, jb:jb + 1], sel_g[:, jb + 1:jb + 2]))
            biases.append(jnp.where(jnp.concatenate(pieces, axis=-1) > 0.5, 0.0, NEG_INF))
        parts.append((_dot(qr, kt), _group_rows(biases, NSA_GROUP), lambda p, vt=vt: _dot_nt(p, vt)))
    nb_past = past // SEL_BLOCK
    mask = (_group_rows([sel_g[:, nb_past:nb_past + 1] for sel_g in sel_groups], NSA_GROUP) > 0.5) & new_causal
    parts.append((_dot_nt(qr, snew[:, 0:LANES].astype(BF16)), mask,
                  lambda p: _dot(p, snew[:, LANES:2 * LANES].astype(BF16))))
    o_sel = _attend_segments(parts)

    o_win = _attend_segments([
        (_dot(qr, wst_ref[0:LANES, :].astype(BF16)), win_mask,
         lambda p: _dot_nt(p, wst_ref[LANES:2 * LANES, :].astype(BF16))),
        (_dot_nt(qr, wnew[:, 0:LANES].astype(BF16)), new_causal,
         lambda p: _dot(p, wnew[:, LANES:2 * LANES].astype(BF16)))])

    o_ref[...] = _mix_heads(gates_ref[...], o_cmp, o_sel, o_win, t_pad)


def _nsa_decode_sel(page_table, qr_b, snew_b, wnew_b, sel_groups, ocmp, gates_b, win_state, pool, t_new, t_pad):
    nb, n_pages = page_table.shape
    past = n_pages * PAGE_SIZE
    tk = 1024
    assert past % tk == 0
    per_b = lambda a: pl.BlockSpec((None,) + a.shape[1:], lambda b, pt: (b,) + (0,) * (a.ndim - 1))
    grid_spec = pltpu.PrefetchScalarGridSpec(
        num_scalar_prefetch=1,
        grid=(nb,),
        in_specs=[per_b(qr_b), per_b(snew_b), per_b(wnew_b), per_b(sel_groups), per_b(ocmp), per_b(gates_b),
                  per_b(win_state), pl.BlockSpec(memory_space=pl.ANY)],
        out_specs=pl.BlockSpec((None, t_pad, NSA_WIDTH), lambda b, pt: (b, 0, 0)),
        scratch_shapes=[pltpu.VMEM((2, n_pages, NSA_KV_WIDTH, PAGE_SIZE), F32), pltpu.SemaphoreType.DMA((2,))],
    )
    return pl.pallas_call(
        functools.partial(_nsa_dec_sel_kernel, n_pages=n_pages, t_new=t_new, t_pad=t_pad, tk=tk),
        grid_spec=grid_spec,
        out_shape=jax.ShapeDtypeStruct((nb, t_pad, NSA_WIDTH), F32),
        compiler_params=_cparams(("arbitrary",)),
        name="nsa_decode_select_window",
    )(page_table, qr_b, snew_b, wnew_b, sel_groups, ocmp, gates_b, win_state, pool)


def _prep_weights(w_in, q_norm_g, w_uq, kv_norm_g, w_uk, w_uv, cmp_pos_emb, cmp_w1, cmp_b1, cmp_w2, w_o, ln_g, ln_b):
    cuts = np.cumsum([MLA_Q_LORA, MLA_KV_LORA, MLA_ROPE_DIM, MLA_WIDTH, NSA_WIDTH, NSA_KV_WIDTH, NSA_KV_WIDTH,
                      NSA_KV_WIDTH, 3 * NSA_HEADS])[:].tolist()
    c_q, c_kv, k_r, z_mla, q_n, cmp_kv, sel_kv, win_kv, g_br, z_nsa = jnp.split(w_in, cuts, axis=1)
    pad = jnp.zeros((D_MODEL, LANES - MLA_ROPE_DIM - 3 * NSA_HEADS), w_in.dtype)
    w_in_p = jnp.concatenate([c_q, c_kv, z_mla, q_n, cmp_kv, sel_kv, win_kv, z_nsa, k_r, g_br, pad], axis=1)
    assert w_in_p.shape[1] == IN_WIDTH_P
    uq = w_uq.reshape(MLA_Q_LORA, MLA_HEADS, MLA_NOPE_DIM + MLA_ROPE_DIM)
    w_uq_p = jnp.concatenate([uq[:, :, :MLA_NOPE_DIM].reshape(MLA_Q_LORA, -1),
                              uq[:, :, MLA_NOPE_DIM:].reshape(MLA_Q_LORA, -1)], axis=1)
    w_uk_t = jnp.transpose(w_uk.reshape(MLA_KV_LORA, MLA_HEADS, MLA_NOPE_DIM), (1, 2, 0))
    w_uv_h = jnp.transpose(w_uv.reshape(MLA_KV_LORA, MLA_HEADS, MLA_V_DIM), (1, 0, 2))
    ratio = CMP_BLOCK // CMP_STRIDE
    eye = jnp.eye(NSA_KV_HEADS, dtype=w_in.dtype)
    w1r = cmp_w1.reshape(2, ratio, CMP_STRIDE, NSA_HEAD_DIM, NSA_HEAD_DIM)
    wc1 = jnp.einsum('gh,srpde->spgdrhe', eye, w1r).reshape(2, CMP_PAIRS, 2 * LANES, ratio * LANES)
    w2 = jnp.einsum('gh,sde->sgdhe', eye, cmp_w2).reshape(2, LANES, LANES)
    pe = jnp.transpose(cmp_pos_emb, (1, 0, 2))
    pe = jnp.concatenate([pe, pe], axis=-1).reshape(2, ratio, CMP_PAIRS, 2 * LANES)
    pe = jnp.transpose(pe, (0, 2, 1, 3))
    pe = jnp.pad(pe, ((0, 0), (0, 0), (0, 8 - ratio), (0, 0)))
    b1 = jnp.concatenate([cmp_b1, cmp_b1], axis=-1).reshape(2, 1, LANES)
    return dict(w_in=w_in_p.astype(BF16), q_norm_g=q_norm_g.reshape(1, -1), w_uq=w_uq_p.astype(BF16),
                w_uk=w_uk_t.astype(BF16), kv_norm_g=kv_norm_g.reshape(1, -1), w_uv=w_uv_h.astype(BF16),
                wc1=wc1.astype(BF16), w2=w2.astype(BF16), pe=pe, b1=b1,
                w_o=w_o.astype(BF16), ln_g=ln_g.reshape(1, -1), ln_b=ln_b.reshape(1, -1))


def _prompt_layer(x, wts):
    batch, seq, _ = x.shape
    x2d = x.reshape(batch * seq, D_MODEL)
    tm = min(512, seq)
    P = _project(x2d, _rope_tables(jnp.arange(seq, dtype=jnp.int32)), wts, tm, feature_major=True)
    o_mla = _mla_prompt(P['qmla'], P['mrowb'], wts['w_uv'], batch, seq)
    kvc = _compress_prompt(P['cmps'], wts, batch, seq)
    o_nsa = _nsa_prompt(P, kvc, batch, seq)
    y = _finish(x2d, o_mla, o_nsa, P['szm'], P['szn'], wts, tm)
    kvd = (2, NSA_KV_HEADS, NSA_HEAD_DIM)
    w_keep = min(WINDOW, seq)
    token_major = lambda a: jnp.swapaxes(a, 1, 2)
    return (y.reshape(batch, seq, D_MODEL),
            token_major(P['mrow']),
            token_major(P['cmp']).reshape((batch, seq) + kvd),
            token_major(P['sel']).reshape((batch, seq) + kvd),
            token_major(P['win'][:, :, seq - w_keep:]).reshape((batch, w_keep) + kvd))


def _sample_layer(x, cache_mla, cache_cmp, cache_sel, win_state, page_table, wts):
    nb, t_new, _ = x.shape
    n_pages = page_table.shape[1]
    past = n_pages * PAGE_SIZE
    n = nb * t_new
    t_pad = 8
    x2d = x.reshape(n, D_MODEL)
    pos = past + (jnp.arange(n, dtype=jnp.int32) % t_new)
    P = _project(x2d, _rope_tables(pos), wts, n)
    kvd = (2, NSA_KV_HEADS, NSA_HEAD_DIM)

    q_b = jnp.transpose(P['qmla'][:, :, :MLA_ROW].reshape(MLA_HEADS, nb, t_new, MLA_ROW), (1, 0, 2, 3))
    q_b = q_b.reshape(nb, -1, MLA_ROW)
    knew_b = jnp.pad(P['mrowb'][:, :MLA_ROW].reshape(nb, t_new, MLA_ROW), ((0, 0), (0, 16 - t_new), (0, 0)))
    feat_major = lambda a: jnp.swapaxes(a.reshape(a.shape[0], a.shape[1], -1), 1, 2)
    def q_rows(q):
        q = jnp.transpose(q.reshape(NSA_HEADS, nb, t_new, LANES), (1, 0, 2, 3))
        q = jnp.pad(q, ((0, 0), (0, 0), (0, t_pad - t_new), (0, 0)))
        return q.reshape(nb, NSA_HEADS * t_pad, LANES)

    cnew_b = P['cmp'].reshape(nb, t_new, NSA_KV_WIDTH)
    kvc = _nsa_decode_cmp(page_table, cnew_b, feat_major(cache_cmp), wts, t_new)
    o_lat, ocmp, sel_groups = _mla_decode(page_table, q_b, knew_b, q_rows(P['qn']), kvc, feat_major(cache_mla), t_pad)
    o_heads = jnp.transpose(o_lat.reshape(nb, MLA_HEADS, t_new, MLA_KV_LORA), (1, 0, 2, 3)).reshape(MLA_HEADS, n, -1)
    o_mla = _uv(o_heads, wts['w_uv'])
    pad_rows = lambda a, r: jnp.pad(a.reshape(nb, t_new, -1), ((0, 0), (0, r - t_new), (0, 0)))
    o_nsa = _nsa_decode_sel(page_table, q_rows(P['qrot']), pad_rows(P['sel'], 16), pad_rows(P['win'], 16),
                            sel_groups, ocmp, pad_rows(P['gates'], t_pad), feat_major(win_state),
                            feat_major(cache_sel), t_new, t_pad)
    o_nsa = o_nsa[:, :t_new].reshape(n, NSA_WIDTH)

    y = _finish(x2d, o_mla, o_nsa, P['szm'], P['szn'], wts, n)
    wst = win_state.reshape(nb, -1, NSA_KV_WIDTH)
    win_all = jnp.concatenate([wst, P['win'].reshape(nb, t_new, NSA_KV_WIDTH)], axis=1)[:, t_new:]
    return (y.reshape(nb, t_new, D_MODEL),
            P['mrow'].reshape(nb, t_new, MLA_ROW),
            P['cmp'].reshape((nb, t_new) + kvd),
            P['sel'].reshape((nb, t_new) + kvd),
            win_all.reshape((nb, win_all.shape[1]) + kvd))


def kernel(x_prompt, x_sample, cache_mla, cache_cmp_kv, cache_sel_kv, state_win_kv, page_table, w_in, q_norm_g, w_uq,
           kv_norm_g, w_uk, w_uv, cmp_pos_emb, cmp_w1, cmp_b1, cmp_w2, w_o, ln_g, ln_b):
    assert w_in.shape[0] == DEPTH
    wts = _prep_weights(w_in[0], q_norm_g[0], w_uq[0], kv_norm_g[0], w_uk[0], w_uv[0], cmp_pos_emb[0], cmp_w1[0],
                        cmp_b1[0], cmp_w2[0], w_o[0], ln_g[0], ln_b[0])
    yp, p_mla, p_cmp, p_sel, p_win = _prompt_layer(x_prompt, wts)
    drop_depth = lambda a: a.reshape(a.shape[1:])
    ys, s_mla, s_cmp, s_sel, s_win = _sample_layer(x_sample, drop_depth(cache_mla), drop_depth(cache_cmp_kv),
                                                   drop_depth(cache_sel_kv), drop_depth(state_win_kv), page_table, wts)
    add_depth = lambda a: a[None]
    return (yp, ys) + tuple(add_depth(a) for a in (p_mla, p_cmp, p_sel, p_win, s_mla, s_cmp, s_sel, s_win))
```
